```python
import jax, jax.numpy as jnp
from jax import lax
import numpy as np

D_MODEL = 1024
BATCH = 16
SEQ = 2048
DEPTH = 2

GRID_W = 64
CTX_LEN = 256
N_MIXERS = 2
N_A_LAYERS = (DEPTH + N_MIXERS - 1) // N_MIXERS
N_B_LAYERS = DEPTH // N_MIXERS

D_RNN = D_MODEL
CONV_W = 4
LRU_BLOCKS = 8
LRU_BLOCK = D_RNN // LRU_BLOCKS
LRU_C = 8.0

MLA_HEADS = 16
QK_NOPE = 64
QK_ROPE = 32
QK_DIM = QK_NOPE + QK_ROPE
V_DIM = 64
Q_LORA = 384
KV_LORA = 256
ROPE_FREQS = QK_ROPE // 4
ROPE_BASE = 10000.0
Q_BLOCK = 128
ATTN_SCALE = QK_DIM ** -0.5

N_EXPERTS = 16
D_EXPERT = 1024
CAPACITY_FACTOR = 2

EPS = 1e-6

kernel_name = "hybrid_rglru_mla_ecmoe_dit"


def rmsnorm(x, g):
    xf = x.astype(jnp.float32)
    y = xf * lax.rsqrt(jnp.mean(xf * xf, axis=-1, keepdims=True) + EPS)
    return (y * g.astype(jnp.float32)).astype(x.dtype)


def adaln(cvec, w, b):
    m = jax.nn.silu(cvec) @ w + b
    return jnp.split(m[..., None, :], 6, axis=-1)


def modulate(h, shift, scale):
    return h * (1 + scale) + shift


def axial_rope(rows):
    row = jnp.broadcast_to(jnp.arange(rows)[:, None], (rows, GRID_W)).reshape(-1)
    col = jnp.broadcast_to(jnp.arange(GRID_W)[None, :], (rows, GRID_W)).reshape(-1)
    pos = jnp.stack([row, col], axis=-1).astype(jnp.float32)
    freq = ROPE_BASE ** (-jnp.arange(ROPE_FREQS, dtype=jnp.float32) / ROPE_FREQS)
    ang = pos[:, :, None] * freq
    return jnp.cos(ang), jnp.sin(ang)


def apply_rope(x, cos, sin):
    xr = x.reshape(x.shape[:-1] + (2, 2, ROPE_FREQS))
    x1 = xr[..., 0, :]
    x2 = xr[..., 1, :]
    out = jnp.stack([x1 * cos - x2 * sin, x1 * sin + x2 * cos], axis=-2)
    return out.reshape(x.shape).astype(x.dtype)


def dwconv_centred(x, w, b):
    T = x.shape[1]
    xp = jnp.pad(x, ((0, 0), (CONV_W // 2, CONV_W - 1 - CONV_W // 2), (0, 0)))
    y = xp[:, 0:T] * w[0]
    for k in range(1, CONV_W):
        y = y + xp[:, k:k + T] * w[k]
    return y + b


def rglru_direction(xc, w_g, b_g, lam, h0, reverse):
    B, T, _ = xc.shape
    xb = xc.reshape(B, T, LRU_BLOCKS, LRU_BLOCK)
    g = jnp.einsum('btnk,gnkj->gbtnj', xb, w_g).reshape(2, B, T, D_RNN) + b_g[:, None, None, :]
    r = jax.nn.sigmoid(g[0].astype(jnp.float32))
    ig = jax.nn.sigmoid(g[1].astype(jnp.float32))
    log_a = -LRU_C * r * jax.nn.softplus(-lam.astype(jnp.float32))
    a = jnp.exp(log_a)
    u = jnp.sqrt(-jnp.expm1(2.0 * log_a)) * (ig * xc.astype(jnp.float32))

    def step(h, au):
        at, ut = au
        h = at * h + ut
        return h, h

    h_last, hs = lax.scan(step, h0, (jnp.moveaxis(a, 1, 0), jnp.moveaxis(u, 1, 0)), reverse=reverse)
    return jnp.moveaxis(hs, 0, 1), h_last


def rglru_mixer(h_lat, h_ctx, w_in, conv_w, conv_b, gate_w, gate_b, lam, w_out, need_ctx):
    def branches(h):
        z = h @ w_in
        return jax.nn.gelu(z[..., :D_RNN]), dwconv_centred(z[..., D_RNN:], conv_w, conv_b)

    B = h_lat.shape[0]
    g_ctx, xc_ctx = branches(h_ctx)
    g_lat, xc_lat = branches(h_lat)
    y_lat = 0.0
    y_ctx = 0.0
    for d in range(2):
        h0 = jnp.zeros((B, D_RNN), jnp.float32)
        yc, hc = rglru_direction(xc_ctx, gate_w[d], gate_b[d], lam[d], h0, reverse=(d == 1))
        yl, _ = rglru_direction(xc_lat, gate_w[d], gate_b[d], lam[d], hc, reverse=(d == 1))
        y_lat = y_lat + yl
        y_ctx = y_ctx + yc
    out_lat = (y_lat.astype(h_lat.dtype) * g_lat) @ w_out
    out_ctx = (y_ctx.astype(h_ctx.dtype) * g_ctx) @ w_out if need_ctx else None
    return out_lat, out_ctx


def mla_qkv(h, w_in, q_norm, kv_norm, w_uq, w_ukv, cos, sin):
    B, L, _ = h.shape
    z = h @ w_in
    cq = z[..., :Q_LORA]
    ckv = z[..., Q_LORA:Q_LORA + KV_LORA]
    kr = z[..., Q_LORA + KV_LORA:]
    q = (rmsnorm(cq, q_norm) @ w_uq).reshape(B, L, MLA_HEADS, QK_DIM)
    kv = (rmsnorm(ckv, kv_norm) @ w_ukv).reshape(B, L, MLA_HEADS, QK_NOPE + V_DIM)
    q_nope, q_rope = q[..., :QK_NOPE], q[..., QK_NOPE:]
    k_nope, v = kv[..., :QK_NOPE], kv[..., QK_NOPE:]
    if cos is not None:
        q_rope = apply_rope(q_rope, cos[:, None], sin[:, None])
        kr = apply_rope(kr, cos, sin)
    k_rope = jnp.broadcast_to(kr[:, :, None, :], (B, L, MLA_HEADS, QK_ROPE))
    return (jnp.concatenate([q_nope, q_rope], axis=-1),
            jnp.concatenate([k_nope, k_rope], axis=-1), v)


def attend(q, k, v):
    s = jnp.einsum('bqhd,bkhd->bhqk', q, k).astype(jnp.float32) * ATTN_SCALE
    p = jax.nn.softmax(s, axis=-1).astype(v.dtype)
    return jnp.einsum('bhqk,bkhd->bqhd', p, v)


def mla_mixer(h_lat, h_ctx, w_in, q_norm, kv_norm, w_uq, w_ukv, w_o, cos, sin, need_ctx):
    B, T, _ = h_lat.shape
    q_lat, k_lat, v_lat = mla_qkv(h_lat, w_in, q_norm, kv_norm, w_uq, w_ukv, cos, sin)
    q_ctx, k_ctx, v_ctx = mla_qkv(h_ctx, w_in, q_norm, kv_norm, w_uq, w_ukv, None, None)
    k_all = jnp.concatenate([k_lat, k_ctx], axis=1)
    v_all = jnp.concatenate([v_lat, v_ctx], axis=1)
    nb = T // Q_BLOCK
    qb = q_lat.reshape(B, nb, Q_BLOCK, MLA_HEADS, QK_DIM).swapaxes(0, 1)
    o = lax.map(lambda qq: attend(qq, k_all, v_all), qb)
    out_lat = o.swapaxes(0, 1).reshape(B, T, MLA_HEADS * V_DIM) @ w_o
    out_ctx = None
    if need_ctx:
        Lc = h_ctx.shape[1]
        out_ctx = attend(q_ctx, k_ctx, v_ctx).reshape(B, Lc, MLA_HEADS * V_DIM) @ w_o
    return out_lat, out_ctx


def ec_moe(h, router, w_up, w_down):
    B, T, _ = h.shape
    cap = CAPACITY_FACTOR * T // N_EXPERTS
    aff = jax.nn.softmax((h @ router).astype(jnp.float32), axis=-1)
    gate, idx = lax.top_k(jnp.swapaxes(aff, 1, 2), cap)
    bidx = jnp.arange(B)[:, None, None]
    xg = h[bidx, idx]
    up = jnp.einsum('becd,edf->becf', xg, w_up)
    hid = jax.nn.silu(up[..., :D_EXPERT]) * up[..., D_EXPERT:]
    y = jnp.einsum('becf,efd->becd', hid, w_down) * gate[..., None].astype(h.dtype)
    return jnp.zeros_like(h).at[bidx, idx].add(y)


def setup_inputs(seed: int = 0) -> dict:
    key = jax.random.key(seed)
    ks = jax.random.split(key, 32)
    f32 = jnp.float32
    nrm = lambda k, shape, s: jax.random.normal(k, shape, f32) * s
    a0 = jax.random.uniform(ks[12], (N_A_LAYERS, 2, D_RNN), f32, 0.9, 0.999)
    return {
        "x": nrm(ks[0], (BATCH, SEQ, D_MODEL), 1.0),
        "c": nrm(ks[1], (BATCH, D_MODEL), 1.0),
        "ctx": nrm(ks[2], (BATCH, CTX_LEN, D_MODEL), 1.0),
        "c_ctx": nrm(ks[3], (D_MODEL,), 1.0),
        "ada_w": nrm(ks[4], (DEPTH, D_MODEL, 6 * D_MODEL), 0.5 * D_MODEL ** -0.5),
        "ada_b": nrm(ks[5], (DEPTH, 6 * D_MODEL), 0.01),
        "norm_mix": 1.0 + nrm(ks[6], (DEPTH, D_MODEL), 0.01),
        "norm_ffn": 1.0 + nrm(ks[7], (DEPTH, D_MODEL), 0.01),
        "norm_final": 1.0 + nrm(ks[8], (D_MODEL,), 0.01),
        "lru_w_in": nrm(ks[9], (N_A_LAYERS, D_MODEL, 2 * D_RNN), D_MODEL ** -0.5),
        "lru_conv_w": nrm(ks[10], (N_A_LAYERS, CONV_W, D_RNN), CONV_W ** -0.5),
        "lru_conv_b": nrm(ks[11], (N_A_LAYERS, D_RNN), 0.01),
        "lru_gate_w": nrm(ks[13], (N_A_LAYERS, 2, 2, LRU_BLOCKS, LRU_BLOCK, LRU_BLOCK), LRU_BLOCK ** -0.5),
        "lru_gate_b": nrm(ks[14], (N_A_LAYERS, 2, 2, D_RNN), 0.01),
        "lru_lambda": jnp.log(a0) - jnp.log1p(-a0),
        "lru_w_out": nrm(ks[15], (N_A_LAYERS, D_RNN, D_MODEL), D_RNN ** -0.5),
        "mla_w_in": nrm(ks[16], (N_B_LAYERS, D_MODEL, Q_LORA + KV_LORA + QK_ROPE), D_MODEL ** -0.5),
        "mla_q_norm": 1.0 + nrm(ks[17], (N_B_LAYERS, Q_LORA), 0.01),
        "mla_kv_norm": 1.0 + nrm(ks[18], (N_B_LAYERS, KV_LORA), 0.01),
        "mla_w_uq": nrm(ks[19], (N_B_LAYERS, Q_LORA, MLA_HEADS * QK_DIM), Q_LORA ** -0.5),
        "mla_w_ukv": nrm(ks[20], (N_B_LAYERS, KV_LORA, MLA_HEADS * (QK_NOPE + V_DIM)), KV_LORA ** -0.5),
        "mla_w_o": nrm(ks[21], (N_B_LAYERS, MLA_HEADS * V_DIM, D_MODEL), (MLA_HEADS * V_DIM) ** -0.5),
        "moe_router": nrm(ks[22], (DEPTH, D_MODEL, N_EXPERTS), D_MODEL ** -0.5),
        "moe_w_up": nrm(ks[23], (DEPTH, N_EXPERTS, D_MODEL, 2 * D_EXPERT), D_MODEL ** -0.5),
        "moe_w_down": nrm(ks[24], (DEPTH, N_EXPERTS, D_EXPERT, D_MODEL), D_EXPERT ** -0.5),
    }


def reference(x, c, ctx, c_ctx, ada_w, ada_b, norm_mix, norm_ffn, norm_final,
              lru_w_in, lru_conv_w, lru_conv_b, lru_gate_w, lru_gate_b, lru_lambda, lru_w_out,
              mla_w_in, mla_q_norm, mla_kv_norm, mla_w_uq, mla_w_ukv, mla_w_o,
              moe_router, moe_w_up, moe_w_down):
    T = x.shape[1]
    ROWS = T // GRID_W
    cos, sin = axial_rope(ROWS)
    for i in range(DEPTH):
        need_ctx = i < DEPTH - 1
        s1, sc1, g1, s2, sc2, g2 = adaln(c, ada_w[i], ada_b[i])
        s1c, sc1c, g1c, s2c, sc2c, g2c = adaln(c_ctx, ada_w[i], ada_b[i])
        h_lat = modulate(rmsnorm(x, norm_mix[i]), s1, sc1)
        h_ctx = modulate(rmsnorm(ctx, norm_mix[i]), s1c, sc1c)
        j = i // N_MIXERS
        if i % N_MIXERS == 0:
            o_lat, o_ctx = rglru_mixer(h_lat, h_ctx, lru_w_in[j], lru_conv_w[j], lru_conv_b[j],
                                       lru_gate_w[j], lru_gate_b[j], lru_lambda[j], lru_w_out[j], need_ctx)
        else:
            o_lat, o_ctx = mla_mixer(h_lat, h_ctx, mla_w_in[j], mla_q_norm[j], mla_kv_norm[j],
                                     mla_w_uq[j], mla_w_ukv[j], mla_w_o[j], cos, sin, need_ctx)
        x = x + g1 * o_lat
        x = x + g2 * ec_moe(modulate(rmsnorm(x, norm_ffn[i]), s2, sc2),
                            moe_router[i], moe_w_up[i], moe_w_down[i])
        if need_ctx:
            ctx = ctx + g1c * o_ctx
            ctx = ctx + g2c * ec_moe(modulate(rmsnorm(ctx, norm_ffn[i]), s2c, sc2c),
                                     moe_router[i], moe_w_up[i], moe_w_down[i])
    return rmsnorm(x, norm_final)
```

```python
import functools

import numpy as np
import jax
import jax.numpy as jnp
from jax import lax
from jax.experimental import pallas as pl
from jax.experimental.pallas import tpu as pltpu

F32 = jnp.float32
BF16 = jnp.bfloat16
I32 = jnp.int32

D_MODEL = 1024
DEPTH = 2
GRID_W = 64
D_RNN = D_MODEL
CONV_W = 4
LRU_BLOCKS = 8
LRU_BLOCK = D_RNN // LRU_BLOCKS
LRU_C = 8.0
MLA_HEADS = 16
QK_NOPE = 64
QK_ROPE = 32
QK_DIM = QK_NOPE + QK_ROPE
V_DIM = 64
Q_LORA = 384
KV_LORA = 256
ROPE_FREQS = QK_ROPE // 4
ROPE_BASE = 10000.0
ATTN_SCALE = QK_DIM ** -0.5
N_EXPERTS = 16
D_EXPERT = 1024
CAPACITY_FACTOR = 2
EPS = 1e-6

LANES = 128
HEAD_PAD = LANES
VMEM_LIMIT = 56 * 1024 * 1024
NEG_BIG = -1e30
HALO = 8
MOE_ROWS = 2048
SCAN_TT = 64


def _cparams(sem):
    return pltpu.CompilerParams(dimension_semantics=sem, vmem_limit_bytes=VMEM_LIMIT)


def _sigmoid(x):
    return 0.5 * (jnp.tanh(0.5 * x) + 1.0)


def _silu(x):
    return x * _sigmoid(x)


def _gelu_tanh(x):
    return 0.5 * x * (1.0 + jnp.tanh(0.7978845608028654 * (x + 0.044715 * (x * x * x))))


def _rms(x, gamma):
    return x * lax.rsqrt(jnp.mean(x * x, axis=-1, keepdims=True) + EPS) * gamma


def _rms_mod(x, gamma, shift, scale):
    return _rms(x, gamma) * (1.0 + scale) + shift


def _dot(a, b):
    return jnp.dot(a, b, preferred_element_type=F32)


def _adaln_kernel(c_ref, w_ref, b_ref, o_ref):
    s = _silu(c_ref[...]).astype(BF16)
    o_ref[0] = _dot(s, w_ref[0].astype(BF16)) + b_ref[0]


def _adaln(cc, ada_w, ada_b):
    rows = cc.shape[0]
    tn = 1024
    return pl.pallas_call(
        _adaln_kernel,
        grid=(DEPTH, 6 * D_MODEL // tn),
        in_specs=[pl.BlockSpec((rows, D_MODEL), lambda l, j: (0, 0)),
                  pl.BlockSpec((1, D_MODEL, tn), lambda l, j: (l, 0, j)),
                  pl.BlockSpec((1, 1, tn), lambda l, j: (l, 0, j))],
        out_specs=pl.BlockSpec((1, rows, tn), lambda l, j: (l, 0, j)),
        out_shape=jax.ShapeDtypeStruct((DEPTH, rows, 6 * D_MODEL), F32),
        compiler_params=_cparams(("arbitrary", "arbitrary")),
        name="adaln",
    )(cc, ada_w, ada_b.reshape(DEPTH, 1, 6 * D_MODEL))


def _lru_in_kernel(xp_ref, x_ref, xn_ref, gam_ref, sh_ref, sc_ref, w_ref, cw_ref, cb_ref,
                   g_ref, xc_ref, *, tt):
    i = pl.program_id(1)
    n = pl.num_programs(1)
    xe = jnp.concatenate([xp_ref[0], x_ref[0], xn_ref[0]], axis=0)
    he = _rms_mod(xe, gam_ref[...], sh_ref[0], sc_ref[0]).astype(BF16)
    z = _dot(he, w_ref[...])
    g_ref[0] = _gelu_tanh(z[HALO:HALO + tt, :D_RNN]).astype(BF16)
    z2 = z[:, D_RNN:]
    row = lax.broadcasted_iota(I32, (tt + 2 * HALO, 1), 0)
    valid = jnp.logical_and(jnp.logical_or(row >= HALO, i > 0),
                            jnp.logical_or(row < tt + HALO, i < n - 1))
    z2 = jnp.where(valid, z2, 0.0)
    cw = cw_ref[...]
    base = HALO - CONV_W // 2
    acc = z2[base:base + tt] * cw[0:1]
    for k in range(1, CONV_W):
        acc = acc + z2[base + k:base + k + tt] * cw[k:k + 1]
    xc_ref[0] = (acc + cb_ref[...]).astype(BF16)


def _lru_in(x, gamma, shift, scale, w_in, conv_w, conv_b, tt):
    B, L, D = x.shape
    nt = L // tt
    hb = tt // HALO
    last = L // HALO - 1
    vec = pl.BlockSpec((1, D), lambda b, i: (0, 0))
    mod = pl.BlockSpec((1, 1, D), lambda b, i: (b, 0, 0))
    out = pl.BlockSpec((1, tt, D_RNN), lambda b, i: (b, i, 0))
    return pl.pallas_call(
        functools.partial(_lru_in_kernel, tt=tt),
        grid=(B, nt),
        in_specs=[pl.BlockSpec((1, HALO, D), lambda b, i: (b, jnp.maximum(i * hb - 1, 0), 0)),
                  pl.BlockSpec((1, tt, D), lambda b, i: (b, i, 0)),
                  pl.BlockSpec((1, HALO, D), lambda b, i: (b, jnp.minimum((i + 1) * hb, last), 0)),
                  vec, mod, mod,
                  pl.BlockSpec((D, 2 * D_RNN), lambda b, i: (0, 0)),
                  pl.BlockSpec((CONV_W, D_RNN), lambda b, i: (0, 0)),
                  pl.BlockSpec((1, D_RNN), lambda b, i: (0, 0))],
        out_specs=[out, out],
        out_shape=[jax.ShapeDtypeStruct((B, L, D_RNN), BF16)] * 2,
        compiler_params=_cparams(("arbitrary", "arbitrary")),
        name="lru_in",
    )(x, x, x, gamma, shift, scale, w_in, conv_w, conv_b)


def _scan_kernel(xc_ctx_ref, xc_lat_ref, wg_ref, bg_ref, lam_ref, y_ctx_ref, y_lat_ref,
                 a_s, u_s, y_s, h_s, *, reverse, n_ctx):
    i = pl.program_id(0)
    nb, tt, _ = a_s.shape

    @pl.when(i == 0)
    def _():
        h_s[...] = jnp.zeros_like(h_s)

    def tile(x_ref, y_ref):
        x2 = x_ref[...].reshape(nb * tt, D_RNN)
        nlam = -lam_ref[...]
        sp = jnp.maximum(nlam, 0.0) + jnp.log1p(jnp.exp(-jnp.abs(nlam)))
        for n in range(LRU_BLOCKS):
            sl = slice(n * LRU_BLOCK, (n + 1) * LRU_BLOCK)
            xb = x2[:, sl]
            g = _dot(xb, wg_ref[n])
            r = _sigmoid(g[:, :LRU_BLOCK] + bg_ref[0:1, sl])
            ig = _sigmoid(g[:, LRU_BLOCK:] + bg_ref[1:2, sl])
            a = jnp.exp((-LRU_C) * r * sp[:, sl])
            u = jnp.sqrt(1.0 - a * a) * (ig * xb.astype(F32))
            a_s[:, :, sl] = a.reshape(nb, tt, LRU_BLOCK)
            u_s[:, :, sl] = u.reshape(nb, tt, LRU_BLOCK)
        h = h_s[...]
        steps = range(tt - 1, -1, -1) if reverse else range(tt)
        for t in steps:
            h = a_s[:, t, :] * h + u_s[:, t, :]
            y_s[:, t, :] = h
        h_s[...] = h
        y_ref[...] = y_s[...].astype(BF16)

    @pl.when(i < n_ctx)
    def _():
        tile(xc_ctx_ref, y_ctx_ref)

    @pl.when(i >= n_ctx)
    def _():
        tile(xc_lat_ref, y_lat_ref)


def _lru_scan(xc_ctx, xc_lat, wg, bg, lam, reverse):
    B, Lc, _ = xc_ctx.shape
    L = xc_lat.shape[1]
    tt = SCAN_TT
    n_ctx = Lc // tt
    n_lat = L // tt

    def ctx_map(i):
        j = jnp.minimum(i, n_ctx - 1)
        return (0, (n_ctx - 1 - j) if reverse else j, 0)

    def lat_map(i):
        j = jnp.maximum(i - n_ctx, 0)
        return (0, (n_lat - 1 - j) if reverse else j, 0)

    return pl.pallas_call(
        functools.partial(_scan_kernel, reverse=reverse, n_ctx=n_ctx),
        grid=(n_ctx + n_lat,),
        in_specs=[pl.BlockSpec((B, tt, D_RNN), ctx_map),
                  pl.BlockSpec((B, tt, D_RNN), lat_map),
                  pl.BlockSpec((LRU_BLOCKS, LRU_BLOCK, 2 * LRU_BLOCK), lambda i: (0, 0, 0)),
                  pl.BlockSpec((2, D_RNN), lambda i: (0, 0)),
                  pl.BlockSpec((1, D_RNN), lambda i: (0, 0))],
        out_specs=[pl.BlockSpec((B, tt, D_RNN), ctx_map),
                   pl.BlockSpec((B, tt, D_RNN), lat_map)],
        out_shape=[jax.ShapeDtypeStruct((B, Lc, D_RNN), BF16),
                   jax.ShapeDtypeStruct((B, L, D_RNN), BF16)],
        scratch_shapes=[pltpu.VMEM((B, tt, D_RNN), F32),
                        pltpu.VMEM((B, tt, D_RNN), F32),
                        pltpu.VMEM((B, tt, D_RNN), F32),
                        pltpu.VMEM((B, D_RNN), F32)],
        compiler_params=_cparams(("arbitrary",)),
        name="lru_scan_rev" if reverse else "lru_scan_fwd",
    )(xc_ctx, xc_lat, wg, bg, lam)


def _mix_out_kernel(*refs, lru):
    if lru:
        (yf_ref, yb_ref, g_ref, x_ref, w_ref, g1_ref, gam_ref, sh_ref, sc_ref, r_ref,
         xo_ref, h_ref, aff_ref) = refs
        y = yf_ref[0].astype(F32) + yb_ref[0].astype(F32)
        lhs = (y * g_ref[0].astype(F32)).astype(BF16)
    else:
        (a_ref, x_ref, w_ref, g1_ref, gam_ref, sh_ref, sc_ref, r_ref,
         xo_ref, h_ref, aff_ref) = refs
        lhs = a_ref[0]
    xn = x_ref[0] + g1_ref[0] * _dot(lhs, w_ref[...])
    xo_ref[0] = xn
    h = _rms_mod(xn, gam_ref[...], sh_ref[0], sc_ref[0]).astype(BF16)
    h_ref[0] = h
    logits = _dot(h, r_ref[...])
    lane = lax.broadcasted_iota(I32, logits.shape, 1)
    logits = jnp.where(lane < N_EXPERTS, logits, NEG_BIG)
    p = jnp.exp(logits - jnp.max(logits, axis=-1, keepdims=True))
    aff_ref[0] = p / jnp.sum(p, axis=-1, keepdims=True)


def _mix_out(acts, x, w_out, g1, gamma, shift, scale, router_pad, tm):
    B, L, D = x.shape
    row = pl.BlockSpec((1, tm, D), lambda b, i: (b, i, 0))
    vec = pl.BlockSpec((1, D), lambda b, i: (0, 0))
    mod = pl.BlockSpec((1, 1, D), lambda b, i: (b, 0, 0))
    return pl.pallas_call(
        functools.partial(_mix_out_kernel, lru=len(acts) == 3),
        grid=(B, L // tm),
        in_specs=[row] * len(acts) + [
            row,
            pl.BlockSpec((D, D), lambda b, i: (0, 0)),
            mod, vec, mod, mod,
            pl.BlockSpec((D, LANES), lambda b, i: (0, 0))],
        out_specs=[row, row, pl.BlockSpec((1, tm, LANES), lambda b, i: (b, i, 0))],
        out_shape=[jax.ShapeDtypeStruct((B, L, D), F32),
                   jax.ShapeDtypeStruct((B, L, D), BF16),
                   jax.ShapeDtypeStruct((B, L, LANES), F32)],
        compiler_params=_cparams(("arbitrary", "arbitrary")),
        name="mix_out_lru" if len(acts) == 3 else "mix_out_attn",
    )(*acts, x, w_out, g1, gamma, shift, scale, router_pad)


def _topk_kernel(a_ref, offs_ref, slot_ref, *, cap, rb):
    L, ncol = a_ref.shape
    bits = lax.bitcast_convert_type(a_ref[...], I32)

    def search(k, thr):
        cand = thr | lax.shift_left(jnp.int32(1), 30 - k)
        cnt = jnp.sum((bits >= cand).astype(I32), axis=0, keepdims=True)
        return jnp.where(cnt >= cap, cand, thr)

    thr = lax.fori_loop(0, 31, search, jnp.zeros((1, ncol), I32))
    need = (cap - jnp.sum((bits > thr).astype(I32), axis=0, keepdims=True)).astype(F32)
    eq_b = (bits == thr).astype(F32).astype(BF16)
    col = lax.broadcasted_iota(I32, (rb, L), 1)
    rowi = lax.broadcasted_iota(I32, (rb, L), 0)

    def before(r0):
        return (col < rowi + r0).astype(F32).astype(BF16)

    sel_blocks = []
    for r0 in range(0, L, rb):
        blk = bits[r0:r0 + rb]
        eq_rank = _dot(before(r0), eq_b)
        take = jnp.logical_or(blk > thr, jnp.logical_and(blk == thr, eq_rank < need))
        sel_blocks.append(take.astype(F32))
    sel = jnp.concatenate(sel_blocks, axis=0) if len(sel_blocks) > 1 else sel_blocks[0]
    sel_b = sel.astype(BF16)
    offs = offs_ref[...]
    for k, r0 in enumerate(range(0, L, rb)):
        pos = _dot(before(r0), sel_b).astype(I32) + offs
        slot_ref[r0:r0 + rb, :] = jnp.where(sel_blocks[k] > 0.0, pos, -1)


def _topk_slots(aff_cols, offs, cap):
    L, ncol = aff_cols.shape
    return pl.pallas_call(
        functools.partial(_topk_kernel, cap=cap, rb=min(L, 256)),
        grid=(1,),
        in_specs=[pl.BlockSpec((L, ncol), lambda i: (0, 0)),
                  pl.BlockSpec((1, ncol), lambda i: (0, 0))],
        out_specs=pl.BlockSpec((L, ncol), lambda i: (0, 0)),
        out_shape=jax.ShapeDtypeStruct((L, ncol), I32),
        compiler_params=_cparams(("arbitrary",)),
        name="topk_slots",
    )(aff_cols, offs)


def _moe_kernel(h_ref, slot_ref, gate_ref, wu_ref, wd_ref, o_ref, *, n_slots, chunk):
    e = pl.program_id(1)
    rows = h_ref.shape[1]
    slot = slot_ref[0, 0]
    hit = lax.broadcasted_iota(I32, (n_slots, rows), 0) == slot
    onehot = hit.astype(F32).astype(BF16)
    gate_s = jnp.sum(jnp.where(hit, gate_ref[0, 0], 0.0), axis=1, keepdims=True)
    xg = _dot(onehot, h_ref[0]).astype(BF16)
    up = _dot(xg, wu_ref[0])
    hid = (_silu(up[:, :D_EXPERT]) * up[:, D_EXPERT:]).astype(BF16)
    yg = (_dot(hid, wd_ref[0]) * gate_s).astype(BF16)
    for c in range(rows // chunk):
        sl = slice(c * chunk, (c + 1) * chunk)
        comb = lax.dot_general(onehot[:, sl], yg, (((0,), (0,)), ((), ())),
                               preferred_element_type=F32)

        @pl.when(e == 0)
        def _():
            o_ref[0, sl, :] = comb

        @pl.when(e > 0)
        def _():
            o_ref[0, sl, :] += comb


def _moe_experts(h_g, slot_g, gate_g, w_up, w_down, n_slots):
    ng, rows, D = h_g.shape
    route = pl.BlockSpec((1, 1, 1, rows), lambda g, e: (g, e, 0, 0))
    return pl.pallas_call(
        functools.partial(_moe_kernel, n_slots=n_slots, chunk=512),
        grid=(ng, N_EXPERTS),
        in_specs=[pl.BlockSpec((1, rows, D), lambda g, e: (g, 0, 0)),
                  route, route,
                  pl.BlockSpec((1, D, 2 * D_EXPERT), lambda g, e: (e, 0, 0)),
                  pl.BlockSpec((1, D_EXPERT, D), lambda g, e: (e, 0, 0))],
        out_specs=pl.BlockSpec((1, rows, D), lambda g, e: (g, 0, 0)),
        out_shape=jax.ShapeDtypeStruct((ng, rows, D), F32),
        compiler_params=_cparams(("arbitrary", "arbitrary")),
        name="moe_experts",
    )(h_g, slot_g, gate_g, w_up, w_down)


def _ec_moe(h, aff, w_up, w_down):
    B, L, D = h.shape
    E = N_EXPERTS
    cap = CAPACITY_FACTOR * L // E
    G = MOE_ROWS // L
    ng = B // G
    cols = aff[:, :, :E].transpose(1, 0, 2).reshape(L, B * E)
    offs = jnp.repeat((jnp.arange(B, dtype=I32) % G) * cap, E).reshape(1, B * E)
    slot = _topk_slots(cols, offs, cap)

    def group(a):
        return a.reshape(L, ng, G, E).transpose(1, 3, 2, 0).reshape(ng, E, 1, G * L)

    out = _moe_experts(h.reshape(ng, G * L, D), group(slot), group(cols), w_up, w_down, G * cap)
    return out.reshape(B, L, D)


def _mla_in_kernel(*refs, need_q):
    (x_ref, moe_ref, g2_ref, gam_ref, sh_ref, sc_ref, win_ref, qn_ref, kvn_ref,
     wqa_ref, wqb_ref, wk_ref, wv_ref, cos_ref, sin_ref) = refs[:15]
    outs = refs[15:]
    x1 = x_ref[0] + g2_ref[0] * moe_ref[0]
    h = _rms_mod(x1, gam_ref[...], sh_ref[0], sc_ref[0]).astype(BF16)
    z = _dot(h, win_ref[...])
    cos = cos_ref[...]
    sin = sin_ref[...]
    kv0 = Q_LORA + KV_LORA
    ckv = _rms(z[:, Q_LORA:kv0], kvn_ref[...]).astype(BF16)
    k_rope = z[:, kv0:kv0 + HEAD_PAD] * cos + z[:, kv0 + HEAD_PAD:kv0 + 2 * HEAD_PAD] * sin
    k_nope = _dot(ckv, wk_ref[...])
    if need_q:
        x1_ref, q_ref, k_ref, v_ref = outs
        x1_ref[0] = x1
        cq = _rms(z[:, :Q_LORA], qn_ref[...]).astype(BF16)
        qa = _dot(cq, wqa_ref[...])
        qb = _dot(cq, wqb_ref[...])
    else:
        k_ref, v_ref = outs
    for hh in range(MLA_HEADS):
        sl = slice(hh * HEAD_PAD, (hh + 1) * HEAD_PAD)
        k_ref[0, :, sl] = (k_nope[:, sl] + k_rope).astype(BF16)
        if need_q:
            q_ref[0, :, sl] = (qa[:, sl] * cos + qb[:, sl] * sin).astype(BF16)
    v_ref[0] = _dot(ckv, wv_ref[...]).astype(BF16)


def _mla_in(x, moe, g2, gamma, shift, scale, w, cos_t, sin_t, need_q, tm):
    B, L, D = x.shape
    row = pl.BlockSpec((1, tm, D), lambda b, i: (b, i, 0))
    mod = pl.BlockSpec((1, 1, D), lambda b, i: (b, 0, 0))

    def full(a):
        return pl.BlockSpec(a.shape, lambda b, i: (0,) * a.ndim)

    hp = MLA_HEADS * HEAD_PAD
    wide = pl.BlockSpec((1, tm, hp), lambda b, i: (b, i, 0))
    tab = pl.BlockSpec((tm, HEAD_PAD), lambda b, i: (i, 0))
    out_specs = [wide, row]
    out_shape = [jax.ShapeDtypeStruct((B, L, hp), BF16),
                 jax.ShapeDtypeStruct((B, L, MLA_HEADS * V_DIM), BF16)]
    if need_q:
        out_specs = [row, wide] + out_specs
        out_shape = [jax.ShapeDtypeStruct((B, L, D), F32),
                     jax.ShapeDtypeStruct((B, L, hp), BF16)] + out_shape
    weights = [w["win"], w["qn"], w["kvn"], w["wqa"], w["wqb"], w["wk"], w["wv"]]
    return pl.pallas_call(
        functools.partial(_mla_in_kernel, need_q=need_q),
        grid=(B, L // tm),
        in_specs=[row, row, mod, full(gamma), mod, mod] + [full(a) for a in weights] + [tab, tab],
        out_specs=out_specs,
        out_shape=out_shape,
        compiler_params=_cparams(("arbitrary", "arbitrary")),
        name="mla_in_lat" if need_q else "mla_in_ctx",
    )(x, moe, g2, gamma, shift, scale, *weights, cos_t, sin_t)


def _attn_kernel(q_ref, kl_ref, kc_ref, vl_ref, vc_ref, o_ref):
    nt = (((1,), (1,)), ((), ()))
    outs = []
    for j in range(2):
        sl = slice(j * HEAD_PAD, (j + 1) * HEAD_PAD)
        q = q_ref[0, :, sl]
        s1 = lax.dot_general(q, kl_ref[0, :, sl], nt, preferred_element_type=F32) * ATTN_SCALE
        s2 = lax.dot_general(q, kc_ref[0, :, sl], nt, preferred_element_type=F32) * ATTN_SCALE
        m = jnp.maximum(jnp.max(s1, axis=-1, keepdims=True), jnp.max(s2, axis=-1, keepdims=True))
        p1 = jnp.exp(s1 - m)
        p2 = jnp.exp(s2 - m)
        denom = jnp.sum(p1, axis=-1, keepdims=True) + jnp.sum(p2, axis=-1, keepdims=True)
        o = _dot(p1.astype(BF16), vl_ref[0]) + _dot(p2.astype(BF16), vc_ref[0])
        outs.append(o / denom)
    lane = lax.broadcasted_iota(I32, outs[0].shape, 1)
    o_ref[0] = jnp.where(lane < V_DIM, outs[0], outs[1]).astype(BF16)


def _attention(q, k_lat, k_ctx, v_lat, v_ctx, tq):
    B, L, _ = q.shape
    Lc = k_ctx.shape[1]
    pair = 2 * HEAD_PAD
    return pl.pallas_call(
        _attn_kernel,
        grid=(B, MLA_HEADS // 2, L // tq),
        in_specs=[pl.BlockSpec((1, tq, pair), lambda b, h, i: (b, i, h)),
                  pl.BlockSpec((1, L, pair), lambda b, h, i: (b, 0, h)),
                  pl.BlockSpec((1, Lc, pair), lambda b, h, i: (b, 0, h)),
                  pl.BlockSpec((1, L, 2 * V_DIM), lambda b, h, i: (b, 0, h)),
                  pl.BlockSpec((1, Lc, 2 * V_DIM), lambda b, h, i: (b, 0, h))],
        out_specs=pl.BlockSpec((1, tq, 2 * V_DIM), lambda b, h, i: (b, i, h)),
        out_shape=jax.ShapeDtypeStruct((B, L, MLA_HEADS * V_DIM), BF16),
        compiler_params=_cparams(("arbitrary", "arbitrary", "arbitrary")),
        name="attention",
    )(q, k_lat, k_ctx, v_lat, v_ctx)


def _final_kernel(x_ref, moe_ref, g2_ref, gam_ref, o_ref):
    o_ref[0] = _rms(x_ref[0] + g2_ref[0] * moe_ref[0], gam_ref[...])


def _final(x, moe, g2, gamma, tm):
    B, L, D = x.shape
    row = pl.BlockSpec((1, tm, D), lambda b, i: (b, i, 0))
    return pl.pallas_call(
        _final_kernel,
        grid=(B, L // tm),
        in_specs=[row, row, pl.BlockSpec((1, 1, D), lambda b, i: (b, 0, 0)),
                  pl.BlockSpec((1, D), lambda b, i: (0, 0))],
        out_specs=row,
        out_shape=jax.ShapeDtypeStruct((B, L, D), F32),
        compiler_params=_cparams(("arbitrary", "arbitrary")),
        name="final_norm",
    )(x, moe, g2, gamma)


def _rope_tables(rows):
    t = np.arange(rows * GRID_W)
    pos = np.stack([t // GRID_W, t % GRID_W], axis=-1).astype(np.float32)
    freq = jnp.asarray(ROPE_BASE, F32) ** (-jnp.arange(ROPE_FREQS, dtype=F32) / ROPE_FREQS)
    ang = jnp.asarray(pos)[:, :, None] * freq
    cos, sin = jnp.cos(ang), jnp.sin(ang)
    cos_r = jnp.stack([cos, cos], axis=2).reshape(-1, QK_ROPE)
    sin_r = jnp.stack([-sin, sin], axis=2).reshape(-1, QK_ROPE)
    T = cos_r.shape[0]
    pad = HEAD_PAD - QK_DIM
    cos_t = jnp.concatenate([jnp.ones((T, QK_NOPE), F32), cos_r, jnp.zeros((T, pad), F32)], axis=1)
    sin_t = jnp.concatenate([jnp.zeros((T, QK_NOPE), F32), sin_r, jnp.zeros((T, pad), F32)], axis=1)
    return cos_t, sin_t


def _rope_partner():
    p = np.arange(QK_ROPE)
    half = (p % (2 * ROPE_FREQS)) // ROPE_FREQS
    return np.where(half == 0, p + ROPE_FREQS, p - ROPE_FREQS)


def _mla_weights(w_in, q_norm, kv_norm, w_uq, w_ukv):
    partner = _rope_partner()
    pad = HEAD_PAD - QK_DIM
    kv0 = Q_LORA + KV_LORA
    kr = w_in[:, kv0:]
    zl = jnp.zeros((D_MODEL, QK_NOPE), F32)
    zr = jnp.zeros((D_MODEL, pad), F32)
    win = jnp.concatenate([w_in[:, :kv0], zl, kr, zr, zl, kr[:, partner], zr], axis=1)
    wq = w_uq.reshape(Q_LORA, MLA_HEADS, QK_DIM)
    zq = jnp.zeros((Q_LORA, MLA_HEADS, pad), F32)
    wqa = jnp.concatenate([wq, zq], axis=2)
    wqb = jnp.concatenate([jnp.zeros((Q_LORA, MLA_HEADS, QK_NOPE), F32),
                           wq[:, :, QK_NOPE:][:, :, partner], zq], axis=2)
    wkv = w_ukv.reshape(KV_LORA, MLA_HEADS, QK_NOPE + V_DIM)
    wk = jnp.concatenate([wkv[:, :, :QK_NOPE],
                          jnp.zeros((KV_LORA, MLA_HEADS, HEAD_PAD - QK_NOPE), F32)], axis=2)
    wv = wkv[:, :, QK_NOPE:]
    hp = MLA_HEADS * HEAD_PAD
    return {"win": win.astype(BF16),
            "qn": q_norm.reshape(1, Q_LORA), "kvn": kv_norm.reshape(1, KV_LORA),
            "wqa": wqa.reshape(Q_LORA, hp).astype(BF16),
            "wqb": wqb.reshape(Q_LORA, hp).astype(BF16),
            "wk": wk.reshape(KV_LORA, hp).astype(BF16),
            "wv": wv.reshape(KV_LORA, MLA_HEADS * V_DIM).astype(BF16)}


def _pad_router(r):
    return jnp.pad(r, ((0, 0), (0, LANES - N_EXPERTS))).astype(BF16)


def kernel(x, c, ctx, c_ctx, ada_w, ada_b, norm_mix, norm_ffn, norm_final, lru_w_in, lru_conv_w, lru_conv_b, lru_gate_w, lru_gate_b, lru_lambda, lru_w_out, mla_w_in, mla_q_norm, mla_kv_norm, mla_w_uq, mla_w_ukv, mla_w_o, moe_router, moe_w_up, moe_w_down):
    B, T, D = x.shape
    Lc = ctx.shape[1]

    rows = ((B + 1 + 7) // 8) * 8
    cc = jnp.concatenate([c, c_ctx[None, :], jnp.zeros((rows - B - 1, D), F32)], axis=0)
    ada = _adaln(cc, ada_w, ada_b).reshape(DEPTH, rows, 6, D)

    def mods(i):
        lat = [ada[i, :B, k].reshape(B, 1, D) for k in range(6)]
        cx = [jnp.broadcast_to(ada[i, B, k].reshape(1, 1, D), (B, 1, D)) for k in range(6)]
        return lat, cx

    vec = lambda a: a.reshape(1, -1)
    w_up = moe_w_up.astype(BF16)
    w_down = moe_w_down.astype(BF16)

    (s1, sc1, g1, s2, sc2, g2), (s1c, sc1c, g1c, s2c, sc2c, g2c) = mods(0)
    w_in = lru_w_in[0].astype(BF16)
    cw, cb = lru_conv_w[0], vec(lru_conv_b[0])
    g_lat, xc_lat = _lru_in(x, vec(norm_mix[0]), s1, sc1, w_in, cw, cb, tt=512)
    g_ctx, xc_ctx = _lru_in(ctx, vec(norm_mix[0]), s1c, sc1c, w_in, cw, cb, tt=Lc)
    ys = []
    for d in range(2):
        gw = lru_gate_w[0, d]
        wg = jnp.concatenate([gw[0], gw[1]], axis=-1).astype(BF16)
        ys.append(_lru_scan(xc_ctx, xc_lat, wg, lru_gate_b[0, d], vec(lru_lambda[0, d]),
                            reverse=(d == 1)))
    w_out = lru_w_out[0].astype(BF16)
    router = _pad_router(moe_router[0])
    gf = vec(norm_ffn[0])
    x_a, h_lat, aff_lat = _mix_out((ys[0][1], ys[1][1], g_lat), x, w_out, g1, gf, s2, sc2, router, tm=512)
    c_a, h_ctx, aff_ctx = _mix_out((ys[0][0], ys[1][0], g_ctx), ctx, w_out, g1c, gf, s2c, sc2c, router, tm=Lc)
    moe_lat = _ec_moe(h_lat, aff_lat, w_up[0], w_down[0])
    moe_ctx = _ec_moe(h_ctx, aff_ctx, w_up[0], w_down[0])

    (s1, sc1, g1, s2b, sc2b, g2b), (s1c, sc1c, _, _, _, _) = mods(1)
    w = _mla_weights(mla_w_in[0], mla_q_norm[0], mla_kv_norm[0], mla_w_uq[0], mla_w_ukv[0])
    cos_t, sin_t = _rope_tables(T // GRID_W)
    one_t = jnp.concatenate([jnp.ones((Lc, QK_DIM), F32), jnp.zeros((Lc, HEAD_PAD - QK_DIM), F32)], axis=1)
    gm = vec(norm_mix[1])
    x_b, q, k_lat, v_lat = _mla_in(x_a, moe_lat, g2, gm, s1, sc1, w, cos_t, sin_t, True, tm=512)
    k_ctx, v_ctx = _mla_in(c_a, moe_ctx, g2c, gm, s1c, sc1c, w, one_t, jnp.zeros_like(one_t), False, tm=Lc)
    attn = _attention(q, k_lat, k_ctx, v_lat, v_ctx, tq=512)
    x_c, h_lat, aff_lat = _mix_out((attn,), x_b, mla_w_o[0].astype(BF16), g1, vec(norm_ffn[1]),
                                   s2b, sc2b, _pad_router(moe_router[1]), tm=512)
    moe_lat = _ec_moe(h_lat, aff_lat, w_up[1], w_down[1])
    return _final(x_c, moe_lat, g2b, vec(norm_final), tm=512)
```

```python
import functools
import math

import numpy as np
import jax
import jax.numpy as jnp
from jax import lax
from jax.experimental import pallas as pl
from jax.experimental.pallas import tpu as pltpu

F32 = jnp.float32
BF16 = jnp.bfloat16
I32 = jnp.int32

D_MODEL = 1024
DEPTH = 2
GRID_W = 64
D_RNN = D_MODEL
CONV_W = 4
LRU_BLOCKS = 8
LRU_BLOCK = D_RNN // LRU_BLOCKS
LRU_C = 8.0
MLA_HEADS = 16
QK_NOPE = 64
QK_ROPE = 32
QK_DIM = QK_NOPE + QK_ROPE
V_DIM = 64
Q_LORA = 384
KV_LORA = 256
ROPE_FREQS = QK_ROPE // 4
ROPE_BASE = 10000.0
ATTN_SCALE = QK_DIM ** -0.5
N_EXPERTS = 16
D_EXPERT = 1024
CAPACITY_FACTOR = 2
EPS = 1e-6

LANES = 128
HEAD_PAD = LANES
VMEM_LIMIT = 56 * 1024 * 1024
NEG_BIG = -1e30
HALO = 8
MOE_ROWS = 2048
SCAN_TT = 64
ATTN_QB = 128
EXP2_SCALE = ATTN_SCALE * math.log2(math.e)


def _cparams(sem):
    return pltpu.CompilerParams(dimension_semantics=sem, vmem_limit_bytes=VMEM_LIMIT)


def _sigmoid(x):
    return 0.5 * (jnp.tanh(0.5 * x) + 1.0)


def _silu(x):
    return x * _sigmoid(x)


def _gelu_tanh(x):
    return 0.5 * x * (1.0 + jnp.tanh(0.7978845608028654 * (x + 0.044715 * (x * x * x))))


def _rms(x, gamma):
    return x * lax.rsqrt(jnp.mean(x * x, axis=-1, keepdims=True) + EPS) * gamma


def _rms_mod(x, gamma, shift, scale):
    return _rms(x, gamma) * (1.0 + scale) + shift


def _dot(a, b):
    return jnp.dot(a, b, preferred_element_type=F32)


def _dot_nt(a, b):
    return lax.dot_general(a, b, (((1,), (1,)), ((), ())), preferred_element_type=F32)


def _adaln_kernel(c_ref, w_ref, b_ref, o_ref):
    s = _silu(c_ref[...]).astype(BF16)
    o_ref[0] = _dot(s, w_ref[0].astype(BF16)) + b_ref[0]


def _adaln(cc, ada_w, ada_b):
    rows = cc.shape[0]
    tn = 1024
    return pl.pallas_call(
        _adaln_kernel,
        grid=(DEPTH, 6 * D_MODEL // tn),
        in_specs=[pl.BlockSpec((rows, D_MODEL), lambda l, j: (0, 0)),
                  pl.BlockSpec((1, D_MODEL, tn), lambda l, j: (l, 0, j)),
                  pl.BlockSpec((1, 1, tn), lambda l, j: (l, 0, j))],
        out_specs=pl.BlockSpec((1, rows, tn), lambda l, j: (l, 0, j)),
        out_shape=jax.ShapeDtypeStruct((DEPTH, rows, 6 * D_MODEL), F32),
        compiler_params=_cparams(("arbitrary", "arbitrary")),
        name="adaln",
    )(cc, ada_w, ada_b.reshape(DEPTH, 1, 6 * D_MODEL))


def _lru_in_kernel(xp_ref, x_ref, xn_ref, gam_ref, sh_ref, sc_ref, w_ref, cw_ref, cb_ref,
                   g_ref, xc_ref, *, tt):
    i = pl.program_id(1)
    n = pl.num_programs(1)
    xe = jnp.concatenate([xp_ref[0], x_ref[0], xn_ref[0]], axis=0)
    he = _rms_mod(xe, gam_ref[...], sh_ref[0], sc_ref[0]).astype(BF16)
    z = _dot(he, w_ref[...])
    g_ref[0] = _gelu_tanh(z[HALO:HALO + tt, :D_RNN]).astype(BF16)
    z2 = z[:, D_RNN:]
    row = lax.broadcasted_iota(I32, (tt + 2 * HALO, 1), 0)
    valid = jnp.logical_and(jnp.logical_or(row >= HALO, i > 0),
                            jnp.logical_or(row < tt + HALO, i < n - 1))
    z2 = jnp.where(valid, z2, 0.0)
    cw = cw_ref[...]
    base = HALO - CONV_W // 2
    acc = z2[base:base + tt] * cw[0:1]
    for k in range(1, CONV_W):
        acc = acc + z2[base + k:base + k + tt] * cw[k:k + 1]
    xc_ref[0] = (acc + cb_ref[...]).astype(BF16)


def _lru_in(x, gamma, shift, scale, w_in, conv_w, conv_b, tt):
    B, L, D = x.shape
    nt = L // tt
    hb = tt // HALO
    last = L // HALO - 1
    vec = pl.BlockSpec((1, D), lambda b, i: (0, 0))
    mod = pl.BlockSpec((1, 1, D), lambda b, i: (b, 0, 0))
    out = pl.BlockSpec((1, tt, D_RNN), lambda b, i: (b, i, 0))
    return pl.pallas_call(
        functools.partial(_lru_in_kernel, tt=tt),
        grid=(B, nt),
        in_specs=[pl.BlockSpec((1, HALO, D), lambda b, i: (b, jnp.maximum(i * hb - 1, 0), 0)),
                  pl.BlockSpec((1, tt, D), lambda b, i: (b, i, 0)),
                  pl.BlockSpec((1, HALO, D), lambda b, i: (b, jnp.minimum((i + 1) * hb, last), 0)),
                  vec, mod, mod,
                  pl.BlockSpec((D, 2 * D_RNN), lambda b, i: (0, 0)),
                  pl.BlockSpec((CONV_W, D_RNN), lambda b, i: (0, 0)),
                  pl.BlockSpec((1, D_RNN), lambda b, i: (0, 0))],
        out_specs=[out, out],
        out_shape=[jax.ShapeDtypeStruct((B, L, D_RNN), BF16)] * 2,
        compiler_params=_cparams(("arbitrary", "arbitrary")),
        name="lru_in",
    )(x, x, x, gamma, shift, scale, w_in, conv_w, conv_b)


def _scan_kernel(xc_ctx_ref, xc_lat_ref, wg_ref, bg_ref, lam_ref, y_ctx_ref, y_lat_ref,
                 a_s, u_s, y_s, h_s, *, reverse, n_ctx):
    i = pl.program_id(0)
    nb, tt, _ = a_s.shape

    @pl.when(i == 0)
    def _():
        h_s[...] = jnp.zeros_like(h_s)

    def tile(x_ref, y_ref):
        x2 = x_ref[...].reshape(nb * tt, D_RNN)
        nlam = -lam_ref[...]
        sp = jnp.maximum(nlam, 0.0) + jnp.log1p(jnp.exp(-jnp.abs(nlam)))
        for n in range(LRU_BLOCKS):
            sl = slice(n * LRU_BLOCK, (n + 1) * LRU_BLOCK)
            xb = x2[:, sl]
            g = _dot(xb, wg_ref[n])
            r = _sigmoid(g[:, :LRU_BLOCK] + bg_ref[0:1, sl])
            ig = _sigmoid(g[:, LRU_BLOCK:] + bg_ref[1:2, sl])
            a = jnp.exp((-LRU_C) * r * sp[:, sl])
            u = jnp.sqrt(1.0 - a * a) * (ig * xb.astype(F32))
            a_s[:, :, sl] = a.reshape(nb, tt, LRU_BLOCK)
            u_s[:, :, sl] = u.reshape(nb, tt, LRU_BLOCK)
        h = h_s[...]
        steps = range(tt - 1, -1, -1) if reverse else range(tt)
        for t in steps:
            h = a_s[:, t, :] * h + u_s[:, t, :]
            y_s[:, t, :] = h
        h_s[...] = h
        y_ref[...] = y_s[...].astype(BF16)

    @pl.when(i < n_ctx)
    def _():
        tile(xc_ctx_ref, y_ctx_ref)

    @pl.when(i >= n_ctx)
    def _():
        tile(xc_lat_ref, y_lat_ref)


def _lru_scan(xc_ctx, xc_lat, wg, bg, lam, reverse):
    B, Lc, _ = xc_ctx.shape
    L = xc_lat.shape[1]
    tt = SCAN_TT
    n_ctx = Lc // tt
    n_lat = L // tt

    def ctx_map(i):
        j = jnp.minimum(i, n_ctx - 1)
        return (0, (n_ctx - 1 - j) if reverse else j, 0)

    def lat_map(i):
        j = jnp.maximum(i - n_ctx, 0)
        return (0, (n_lat - 1 - j) if reverse else j, 0)

    return pl.pallas_call(
        functools.partial(_scan_kernel, reverse=reverse, n_ctx=n_ctx),
        grid=(n_ctx + n_lat,),
        in_specs=[pl.BlockSpec((B, tt, D_RNN), ctx_map),
                  pl.BlockSpec((B, tt, D_RNN), lat_map),
                  pl.BlockSpec((LRU_BLOCKS, LRU_BLOCK, 2 * LRU_BLOCK), lambda i: (0, 0, 0)),
                  pl.BlockSpec((2, D_RNN), lambda i: (0, 0)),
                  pl.BlockSpec((1, D_RNN), lambda i: (0, 0))],
        out_specs=[pl.BlockSpec((B, tt, D_RNN), ctx_map),
                   pl.BlockSpec((B, tt, D_RNN), lat_map)],
        out_shape=[jax.ShapeDtypeStruct((B, Lc, D_RNN), BF16),
                   jax.ShapeDtypeStruct((B, L, D_RNN), BF16)],
        scratch_shapes=[pltpu.VMEM((B, tt, D_RNN), F32),
                        pltpu.VMEM((B, tt, D_RNN), F32),
                        pltpu.VMEM((B, tt, D_RNN), F32),
                        pltpu.VMEM((B, D_RNN), F32)],
        compiler_params=_cparams(("arbitrary",)),
        name="lru_scan_rev" if reverse else "lru_scan_fwd",
    )(xc_ctx, xc_lat, wg, bg, lam)


def _mix_out_kernel(*refs, lru):
    if lru:
        (yf_ref, yb_ref, g_ref, x_ref, w_ref, g1_ref, gam_ref, sh_ref, sc_ref, r_ref,
         xo_ref, h_ref, aff_ref) = refs
        y = yf_ref[0].astype(F32) + yb_ref[0].astype(F32)
        lhs = (y * g_ref[0].astype(F32)).astype(BF16)
    else:
        (a_ref, x_ref, w_ref, g1_ref, gam_ref, sh_ref, sc_ref, r_ref,
         xo_ref, h_ref, aff_ref) = refs
        lhs = a_ref[0]
    xn = x_ref[0] + g1_ref[0] * _dot(lhs, w_ref[...])
    xo_ref[0] = xn
    h = _rms_mod(xn, gam_ref[...], sh_ref[0], sc_ref[0]).astype(BF16)
    h_ref[0] = h
    logits = _dot(h, r_ref[...])
    lane = lax.broadcasted_iota(I32, logits.shape, 1)
    logits = jnp.where(lane < N_EXPERTS, logits, NEG_BIG)
    p = jnp.exp(logits - jnp.max(logits, axis=-1, keepdims=True))
    aff_ref[0] = p / jnp.sum(p, axis=-1, keepdims=True)


def _mix_out(acts, x, w_out, g1, gamma, shift, scale, router_pad, tm):
    B, L, D = x.shape
    row = pl.BlockSpec((1, tm, D), lambda b, i: (b, i, 0))
    vec = pl.BlockSpec((1, D), lambda b, i: (0, 0))
    mod = pl.BlockSpec((1, 1, D), lambda b, i: (b, 0, 0))
    return pl.pallas_call(
        functools.partial(_mix_out_kernel, lru=len(acts) == 3),
        grid=(B, L // tm),
        in_specs=[row] * len(acts) + [
            row,
            pl.BlockSpec((D, D), lambda b, i: (0, 0)),
            mod, vec, mod, mod,
            pl.BlockSpec((D, LANES), lambda b, i: (0, 0))],
        out_specs=[row, row, pl.BlockSpec((1, tm, LANES), lambda b, i: (b, i, 0))],
        out_shape=[jax.ShapeDtypeStruct((B, L, D), F32),
                   jax.ShapeDtypeStruct((B, L, D), BF16),
                   jax.ShapeDtypeStruct((B, L, LANES), F32)],
        compiler_params=_cparams(("arbitrary", "arbitrary")),
        name="mix_out_lru" if len(acts) == 3 else "mix_out_attn",
    )(*acts, x, w_out, g1, gamma, shift, scale, router_pad)


def _topk_kernel(a_ref, offs_ref, slot_ref, *, cap, rb):
    L, ncol = a_ref.shape
    bits = lax.bitcast_convert_type(a_ref[...], I32)

    def search(k, thr):
        cand = thr | lax.shift_left(jnp.int32(1), 30 - k)
        cnt = jnp.sum((bits >= cand).astype(I32), axis=0, keepdims=True)
        return jnp.where(cnt >= cap, cand, thr)

    thr = lax.fori_loop(0, 31, search, jnp.zeros((1, ncol), I32))
    need = (cap - jnp.sum((bits > thr).astype(I32), axis=0, keepdims=True)).astype(F32)
    eq_b = (bits == thr).astype(F32).astype(BF16)
    col = lax.broadcasted_iota(I32, (rb, L), 1)
    rowi = lax.broadcasted_iota(I32, (rb, L), 0)

    def before(r0):
        return (col < rowi + r0).astype(F32).astype(BF16)

    sel_blocks = []
    for r0 in range(0, L, rb):
        blk = bits[r0:r0 + rb]
        eq_rank = _dot(before(r0), eq_b)
        take = jnp.logical_or(blk > thr, jnp.logical_and(blk == thr, eq_rank < need))
        sel_blocks.append(take.astype(F32))
    sel = jnp.concatenate(sel_blocks, axis=0) if len(sel_blocks) > 1 else sel_blocks[0]
    sel_b = sel.astype(BF16)
    offs = offs_ref[...]
    for k, r0 in enumerate(range(0, L, rb)):
        pos = _dot(before(r0), sel_b).astype(I32) + offs
        slot_ref[r0:r0 + rb, :] = jnp.where(sel_blocks[k] > 0.0, pos, -1)


def _topk_slots(aff_cols, offs, cap):
    L, ncol = aff_cols.shape
    return pl.pallas_call(
        functools.partial(_topk_kernel, cap=cap, rb=min(L, 256)),
        grid=(1,),
        in_specs=[pl.BlockSpec((L, ncol), lambda i: (0, 0)),
                  pl.BlockSpec((1, ncol), lambda i: (0, 0))],
        out_specs=pl.BlockSpec((L, ncol), lambda i: (0, 0)),
        out_shape=jax.ShapeDtypeStruct((L, ncol), I32),
        compiler_params=_cparams(("arbitrary",)),
        name="topk_slots",
    )(aff_cols, offs)


def _moe_ffn_kernel(h_ref, slot_ref, gate_ref, wu_ref, wd_ref, yg_ref, wu_s, wd_s, *, n_slots):
    rows = h_ref.shape[1]

    @pl.when(pl.program_id(1) == 0)
    def _():
        wu_s[...] = wu_ref[0].astype(BF16)
        wd_s[...] = wd_ref[0].astype(BF16)

    hit = lax.broadcasted_iota(I32, (n_slots, rows), 0) == slot_ref[0, 0]
    onehot = hit.astype(F32).astype(BF16)
    gate_s = jnp.sum(jnp.where(hit, gate_ref[0, 0], 0.0), axis=1, keepdims=True)
    xg = _dot(onehot, h_ref[0]).astype(BF16)
    up = _dot(xg, wu_s[...])
    hid = (_silu(up[:, :D_EXPERT]) * up[:, D_EXPERT:]).astype(BF16)
    yg_ref[0, 0] = (_dot(hid, wd_s[...]) * gate_s).astype(BF16)


def _moe_ffn(h_g, slot_g, gate_g, w_up, w_down, n_slots):
    ng, rows, D = h_g.shape
    route = pl.BlockSpec((1, 1, 1, rows), lambda e, g: (g, e, 0, 0))
    return pl.pallas_call(
        functools.partial(_moe_ffn_kernel, n_slots=n_slots),
        grid=(N_EXPERTS, ng),
        in_specs=[pl.BlockSpec((1, rows, D), lambda e, g: (g, 0, 0)),
                  route, route,
                  pl.BlockSpec((1, D, 2 * D_EXPERT), lambda e, g: (e, 0, 0)),
                  pl.BlockSpec((1, D_EXPERT, D), lambda e, g: (e, 0, 0))],
        out_specs=pl.BlockSpec((1, 1, n_slots, D), lambda e, g: (g, e, 0, 0)),
        out_shape=jax.ShapeDtypeStruct((ng, N_EXPERTS, n_slots, D), BF16),
        scratch_shapes=[pltpu.VMEM((D, 2 * D_EXPERT), BF16),
                        pltpu.VMEM((D_EXPERT, D), BF16)],
        compiler_params=_cparams(("arbitrary", "arbitrary")),
        name="moe_ffn",
    )(h_g, slot_g, gate_g, w_up, w_down)


def _moe_combine_kernel(yg_ref, slot_ref, x_ref, g2_ref, gam_ref, o_ref, *, final):
    _, n_e, n_slots, D = yg_ref.shape
    chunk = x_ref.shape[1]
    slots = slot_ref[0]
    lane = lax.broadcasted_iota(I32, (chunk, n_slots), 1)
    pieces = [(slots[:, e:e + 1] == lane).astype(F32).astype(BF16) for e in range(n_e)]
    onehot_t = jnp.concatenate(pieces, axis=1)
    comb = _dot(onehot_t, yg_ref[0].reshape(n_e * n_slots, D))
    xn = x_ref[0] + g2_ref[0] * comb
    o_ref[0] = _rms(xn, gam_ref[...]) if final else xn


def _moe_combine(yg, slot_t, x_g, g2, gamma, L, final):
    ng, n_e, n_slots, D = yg.shape
    rows = x_g.shape[1]
    chunk = min(512, L)
    row = pl.BlockSpec((1, chunk, D), lambda g, c: (g, c, 0))
    return pl.pallas_call(
        functools.partial(_moe_combine_kernel, final=final),
        grid=(ng, rows // chunk),
        in_specs=[pl.BlockSpec((1, n_e, n_slots, D), lambda g, c: (g, 0, 0, 0)),
                  pl.BlockSpec((1, chunk, LANES), lambda g, c: (g, c, 0)),
                  row,
                  pl.BlockSpec((1, 1, D), lambda g, c: ((g * rows + c * chunk) // L, 0, 0)),
                  pl.BlockSpec((1, D), lambda g, c: (0, 0))],
        out_specs=row,
        out_shape=jax.ShapeDtypeStruct((ng, rows, D), F32),
        compiler_params=_cparams(("arbitrary", "arbitrary")),
        name="moe_combine_final" if final else "moe_combine",
    )(yg, slot_t, x_g, g2, gamma)


def _ec_moe(x, h, aff, w_up, w_down, g2, gamma, final):
    B, L, D = h.shape
    E = N_EXPERTS
    cap = CAPACITY_FACTOR * L // E
    G = MOE_ROWS // L
    ng = B // G
    cols = aff[:, :, :E].transpose(1, 0, 2).reshape(L, B * E)
    offs = jnp.repeat((jnp.arange(B, dtype=I32) % G) * cap, E).reshape(1, B * E)
    slot = _topk_slots(cols, offs, cap)

    def by_lane(a):
        return a.reshape(L, ng, G, E).transpose(1, 3, 2, 0).reshape(ng, E, 1, G * L)

    yg = _moe_ffn(h.reshape(ng, G * L, D), by_lane(slot), by_lane(cols), w_up, w_down, G * cap)
    slot_t = slot.reshape(L, ng, G, E).transpose(1, 2, 0, 3).reshape(ng, G * L, E)
    slot_t = jnp.pad(slot_t, ((0, 0), (0, 0), (0, LANES - E)), constant_values=-1)
    out = _moe_combine(yg, slot_t, x.reshape(ng, G * L, D), g2, gamma, L, final)
    return out.reshape(B, L, D)


def _mla_in_kernel(*refs, need_q):
    (x_ref, gam_ref, sh_ref, sc_ref, win_ref, qn_ref, kvn_ref,
     wqa_ref, wqb_ref, wk_ref, wv_ref, vone_ref, cos_ref, sin_ref) = refs[:14]
    outs = refs[14:]
    h = _rms_mod(x_ref[0], gam_ref[...], sh_ref[0], sc_ref[0]).astype(BF16)
    z = _dot(h, win_ref[...])
    cos = cos_ref[...]
    sin = sin_ref[...]
    kv0 = Q_LORA + KV_LORA
    ckv = _rms(z[:, Q_LORA:kv0], kvn_ref[...]).astype(BF16)
    k_rope = z[:, kv0:kv0 + HEAD_PAD] * cos + z[:, kv0 + HEAD_PAD:kv0 + 2 * HEAD_PAD] * sin
    k_nope = _dot(ckv, wk_ref[...])
    if need_q:
        q_ref, k_ref, v_ref = outs
        cq = _rms(z[:, :Q_LORA], qn_ref[...]).astype(BF16)
        qa = _dot(cq, wqa_ref[...])
        qb = _dot(cq, wqb_ref[...])
    else:
        k_ref, v_ref = outs
    for hh in range(MLA_HEADS):
        sl = slice(hh * HEAD_PAD, (hh + 1) * HEAD_PAD)
        k_ref[0, :, sl] = (k_nope[:, sl] + k_rope).astype(BF16)
        if need_q:
            q_ref[0, :, sl] = (qa[:, sl] * cos + qb[:, sl] * sin).astype(BF16)
    v_ref[0] = (_dot(ckv, wv_ref[...]) + vone_ref[...]).astype(BF16)


def _mla_in(x, gamma, shift, scale, w, cos_t, sin_t, need_q, tm):
    B, L, D = x.shape
    row = pl.BlockSpec((1, tm, D), lambda b, i: (b, i, 0))
    mod = pl.BlockSpec((1, 1, D), lambda b, i: (b, 0, 0))

    def full(a):
        return pl.BlockSpec(a.shape, lambda b, i: (0,) * a.ndim)

    hp = MLA_HEADS * HEAD_PAD
    wide = pl.BlockSpec((1, tm, hp), lambda b, i: (b, i, 0))
    tab = pl.BlockSpec((tm, HEAD_PAD), lambda b, i: (i, 0))
    out_specs = [wide, wide]
    out_shape = [jax.ShapeDtypeStruct((B, L, hp), BF16)] * 2
    if need_q:
        out_specs = [wide] + out_specs
        out_shape = [jax.ShapeDtypeStruct((B, L, hp), BF16)] + out_shape
    weights = [w["win"], w["qn"], w["kvn"], w["wqa"], w["wqb"], w["wk"], w["wv"], w["vone"]]
    return pl.pallas_call(
        functools.partial(_mla_in_kernel, need_q=need_q),
        grid=(B, L // tm),
        in_specs=[row, full(gamma), mod, mod] + [full(a) for a in weights] + [tab, tab],
        out_specs=out_specs,
        out_shape=out_shape,
        compiler_params=_cparams(("arbitrary", "arbitrary")),
        name="mla_in_lat" if need_q else "mla_in_ctx",
    )(x, gamma, shift, scale, *weights, cos_t, sin_t)


def _attn_kernel(q_ref, kl_ref, kc_ref, vl_ref, vc_ref, o_ref):
    tq = q_ref.shape[1]
    blocks = [(j, r) for r in range(0, tq, ATTN_QB) for j in range(2)]

    def scores(j, r):
        sl = slice(j * HEAD_PAD, (j + 1) * HEAD_PAD)
        q = q_ref[0, r:r + ATTN_QB, sl]
        return _dot_nt(q, kl_ref[0, :, sl]), _dot_nt(q, kc_ref[0, :, sl])

    def weighted(j, s1, s2):
        sl = slice(j * HEAD_PAD, (j + 1) * HEAD_PAD)
        m = jnp.maximum(jnp.max(s1, axis=-1, keepdims=True), jnp.max(s2, axis=-1, keepdims=True))
        p1 = jnp.exp2((s1 - m) * EXP2_SCALE).astype(BF16)
        p2 = jnp.exp2((s2 - m) * EXP2_SCALE).astype(BF16)
        return _dot(p1, vl_ref[0, :, sl]) + _dot(p2, vc_ref[0, :, sl])

    pending = scores(*blocks[0])
    acc = {}
    for n, (j, r) in enumerate(blocks):
        s1, s2 = pending
        if n + 1 < len(blocks):
            pending = scores(*blocks[n + 1])
        acc[(j, r)] = weighted(j, s1, s2)
    lane = lax.broadcasted_iota(I32, (ATTN_QB, 2 * V_DIM), 1)
    for r in range(0, tq, ATTN_QB):
        even = acc[(0, r)]
        odd = acc[(1, r)]
        even = even / even[:, V_DIM:V_DIM + 1]
        odd = odd / odd[:, 0:1]
        o_ref[0, r:r + ATTN_QB, :] = jnp.where(lane < V_DIM, even, odd).astype(BF16)


def _attention(q, k_lat, k_ctx, v_lat, v_ctx, tq):
    B, L, _ = q.shape
    Lc = k_ctx.shape[1]
    pair = 2 * HEAD_PAD
    return pl.pallas_call(
        _attn_kernel,
        grid=(B, MLA_HEADS // 2, L // tq),
        in_specs=[pl.BlockSpec((1, tq, pair), lambda b, h, i: (b, i, h)),
                  pl.BlockSpec((1, L, pair), lambda b, h, i: (b, 0, h)),
                  pl.BlockSpec((1, Lc, pair), lambda b, h, i: (b, 0, h)),
                  pl.BlockSpec((1, L, pair), lambda b, h, i: (b, 0, h)),
                  pl.BlockSpec((1, Lc, pair), lambda b, h, i: (b, 0, h))],
        out_specs=pl.BlockSpec((1, tq, 2 * V_DIM), lambda b, h, i: (b, i, h)),
        out_shape=jax.ShapeDtypeStruct((B, L, MLA_HEADS * V_DIM), BF16),
        compiler_params=_cparams(("arbitrary", "arbitrary", "arbitrary")),
        name="attention",
    )(q, k_lat, k_ctx, v_lat, v_ctx)


def _rope_tables(rows):
    t = np.arange(rows * GRID_W)
    pos = np.stack([t // GRID_W, t % GRID_W], axis=-1).astype(np.float32)
    freq = jnp.asarray(ROPE_BASE, F32) ** (-jnp.arange(ROPE_FREQS, dtype=F32) / ROPE_FREQS)
    ang = jnp.asarray(pos)[:, :, None] * freq
    cos, sin = jnp.cos(ang), jnp.sin(ang)
    cos_r = jnp.stack([cos, cos], axis=2).reshape(-1, QK_ROPE)
    sin_r = jnp.stack([-sin, sin], axis=2).reshape(-1, QK_ROPE)
    T = cos_r.shape[0]
    pad = HEAD_PAD - QK_DIM
    cos_t = jnp.concatenate([jnp.ones((T, QK_NOPE), F32), cos_r, jnp.zeros((T, pad), F32)], axis=1)
    sin_t = jnp.concatenate([jnp.zeros((T, QK_NOPE), F32), sin_r, jnp.zeros((T, pad), F32)], axis=1)
    return cos_t, sin_t


def _rope_partner():
    p = np.arange(QK_ROPE)
    half = (p % (2 * ROPE_FREQS)) // ROPE_FREQS
    return np.where(half == 0, p + ROPE_FREQS, p - ROPE_FREQS)


def _mla_weights(w_in, q_norm, kv_norm, w_uq, w_ukv):
    partner = _rope_partner()
    pad = HEAD_PAD - QK_DIM
    kv0 = Q_LORA + KV_LORA
    kr = w_in[:, kv0:]
    zl = jnp.zeros((D_MODEL, QK_NOPE), F32)
    zr = jnp.zeros((D_MODEL, pad), F32)
    win = jnp.concatenate([w_in[:, :kv0], zl, kr, zr, zl, kr[:, partner], zr], axis=1)
    wq = w_uq.reshape(Q_LORA, MLA_HEADS, QK_DIM)
    zq = jnp.zeros((Q_LORA, MLA_HEADS, pad), F32)
    wqa = jnp.concatenate([wq, zq], axis=2)
    wqb = jnp.concatenate([jnp.zeros((Q_LORA, MLA_HEADS, QK_NOPE), F32),
                           wq[:, :, QK_NOPE:][:, :, partner], zq], axis=2)
    wkv = w_ukv.reshape(KV_LORA, MLA_HEADS, QK_NOPE + V_DIM)
    wk = jnp.concatenate([wkv[:, :, :QK_NOPE],
                          jnp.zeros((KV_LORA, MLA_HEADS, HEAD_PAD - QK_NOPE), F32)], axis=2)
    zv = jnp.zeros((KV_LORA, MLA_HEADS // 2, HEAD_PAD - V_DIM), F32)
    wv2 = wkv[:, :, QK_NOPE:].reshape(KV_LORA, MLA_HEADS // 2, 2, V_DIM)
    wv = jnp.stack([jnp.concatenate([wv2[:, :, 0], zv], axis=2),
                    jnp.concatenate([zv, wv2[:, :, 1]], axis=2)], axis=2)
    hp = MLA_HEADS * HEAD_PAD
    vone = np.zeros((MLA_HEADS // 2, 2, HEAD_PAD), np.float32)
    vone[:, 0, V_DIM] = 1.0
    vone[:, 1, 0] = 1.0
    return {"win": win.astype(BF16),
            "qn": q_norm.reshape(1, Q_LORA), "kvn": kv_norm.reshape(1, KV_LORA),
            "wqa": wqa.reshape(Q_LORA, hp).astype(BF16),
            "wqb": wqb.reshape(Q_LORA, hp).astype(BF16),
            "wk": wk.reshape(KV_LORA, hp).astype(BF16),
            "wv": wv.reshape(KV_LORA, hp).astype(BF16),
            "vone": jnp.asarray(vone.reshape(1, hp))}


def _pad_router(r):
    return jnp.pad(r, ((0, 0), (0, LANES - N_EXPERTS))).astype(BF16)


def kernel(x, c, ctx, c_ctx, ada_w, ada_b, norm_mix, norm_ffn, norm_final, lru_w_in, lru_conv_w, lru_conv_b, lru_gate_w, lru_gate_b, lru_lambda, lru_w_out, mla_w_in, mla_q_norm, mla_kv_norm, mla_w_uq, mla_w_ukv, mla_w_o, moe_router, moe_w_up, moe_w_down):
    B, T, D = x.shape
    Lc = ctx.shape[1]

    rows = ((B + 1 + 7) // 8) * 8
    cc = jnp.concatenate([c, c_ctx[None, :], jnp.zeros((rows - B - 1, D), F32)], axis=0)
    ada = _adaln(cc, ada_w, ada_b).reshape(DEPTH, rows, 6, D)

    def mods(i):
        lat = [ada[i, :B, k].reshape(B, 1, D) for k in range(6)]
        cx = [jnp.broadcast_to(ada[i, B, k].reshape(1, 1, D), (B, 1, D)) for k in range(6)]
        return lat, cx

    vec = lambda a: a.reshape(1, -1)

    (s1, sc1, g1, s2, sc2, g2), (s1c, sc1c, g1c, s2c, sc2c, g2c) = mods(0)
    w_in = lru_w_in[0].astype(BF16)
    cw, cb = lru_conv_w[0], vec(lru_conv_b[0])
    g_lat, xc_lat = _lru_in(x, vec(norm_mix[0]), s1, sc1, w_in, cw, cb, tt=512)
    g_ctx, xc_ctx = _lru_in(ctx, vec(norm_mix[0]), s1c, sc1c, w_in, cw, cb, tt=Lc)
    ys = []
    for d in range(2):
        gw = lru_gate_w[0, d]
        wg = jnp.concatenate([gw[0], gw[1]], axis=-1).astype(BF16)
        ys.append(_lru_scan(xc_ctx, xc_lat, wg, lru_gate_b[0, d], vec(lru_lambda[0, d]),
                            reverse=(d == 1)))
    w_out = lru_w_out[0].astype(BF16)
    router = _pad_router(moe_router[0])
    gf = vec(norm_ffn[0])
    x_a, h_lat, aff_lat = _mix_out((ys[0][1], ys[1][1], g_lat), x, w_out, g1, gf, s2, sc2, router, tm=512)
    c_a, h_ctx, aff_ctx = _mix_out((ys[0][0], ys[1][0], g_ctx), ctx, w_out, g1c, gf, s2c, sc2c, router, tm=Lc)
    x_b = _ec_moe(x_a, h_lat, aff_lat, moe_w_up[0], moe_w_down[0], g2, gf, final=False)
    c_b = _ec_moe(c_a, h_ctx, aff_ctx, moe_w_up[0], moe_w_down[0], g2c, gf, final=False)

    (s1, sc1, g1, s2, sc2, g2), (s1c, sc1c, _, _, _, _) = mods(1)
    w = _mla_weights(mla_w_in[0], mla_q_norm[0], mla_kv_norm[0], mla_w_uq[0], mla_w_ukv[0])
    cos_t, sin_t = _rope_tables(T // GRID_W)
    one_t = jnp.concatenate([jnp.ones((Lc, QK_DIM), F32), jnp.zeros((Lc, HEAD_PAD - QK_DIM), F32)], axis=1)
    gm = vec(norm_mix[1])
    q, k_lat, v_lat = _mla_in(x_b, gm, s1, sc1, w, cos_t, sin_t, True, tm=512)
    k_ctx, v_ctx = _mla_in(c_b, gm, s1c, sc1c, w, one_t, jnp.zeros_like(one_t), False, tm=Lc)
    attn = _attention(q, k_lat, k_ctx, v_lat, v_ctx, tq=512)
    x_c, h_lat, aff_lat = _mix_out((attn,), x_b, mla_w_o[0].astype(BF16), g1, vec(norm_ffn[1]),
                                   s2, sc2, _pad_router(moe_router[1]), tm=512)
    return _ec_moe(x_c, h_lat, aff_lat, moe_w_up[1], moe_w_down[1], g2, vec(norm_final), final=True)
```

```python
import functools
import math

import numpy as np
import jax
import jax.numpy as jnp
from jax import lax
from jax.experimental import pallas as pl
from jax.experimental.pallas import tpu as pltpu

F32 = jnp.float32
BF16 = jnp.bfloat16
I32 = jnp.int32

D_MODEL = 1024
DEPTH = 2
GRID_W = 64
D_RNN = D_MODEL
CONV_W = 4
LRU_BLOCKS = 8
LRU_BLOCK = D_RNN // LRU_BLOCKS
LRU_C = 8.0
MLA_HEADS = 16
QK_NOPE = 64
QK_ROPE = 32
QK_DIM = QK_NOPE + QK_ROPE
V_DIM = 64
Q_LORA = 384
KV_LORA = 256
ROPE_FREQS = QK_ROPE // 4
ROPE_BASE = 10000.0
ATTN_SCALE = QK_DIM ** -0.5
N_EXPERTS = 16
D_EXPERT = 1024
CAPACITY_FACTOR = 2
EPS = 1e-6

LANES = 128
HEAD_PAD = LANES
VMEM_LIMIT = 56 * 1024 * 1024
NEG_BIG = -1e30
HALO = 8
MOE_ROWS = 2048
SCAN_TT = 64
ATTN_QB = 256
EXP2_SCALE = ATTN_SCALE * math.log2(math.e)


def _cparams(sem):
    return pltpu.CompilerParams(dimension_semantics=sem, vmem_limit_bytes=VMEM_LIMIT)


def _sigmoid(x):
    return 0.5 * (jnp.tanh(0.5 * x) + 1.0)


def _silu(x):
    return x * _sigmoid(x)


def _gelu_tanh(x):
    return 0.5 * x * (1.0 + jnp.tanh(0.7978845608028654 * (x + 0.044715 * (x * x * x))))


def _rms(x, gamma):
    return x * lax.rsqrt(jnp.mean(x * x, axis=-1, keepdims=True) + EPS) * gamma


def _rms_mod(x, gamma, shift, scale):
    return _rms(x, gamma) * (1.0 + scale) + shift


def _dot(a, b):
    return jnp.dot(a, b, preferred_element_type=F32)


def _dot_nt(a, b):
    return lax.dot_general(a, b, (((1,), (1,)), ((), ())), preferred_element_type=F32)


def _adaln_kernel(c_ref, w_ref, b_ref, o_ref):
    s = _silu(c_ref[...]).astype(BF16)
    o_ref[0] = _dot(s, w_ref[0].astype(BF16)) + b_ref[0]


def _adaln(cc, ada_w, ada_b):
    rows = cc.shape[0]
    tn = 1024
    return pl.pallas_call(
        _adaln_kernel,
        grid=(DEPTH, 6 * D_MODEL // tn),
        in_specs=[pl.BlockSpec((rows, D_MODEL), lambda l, j: (0, 0)),
                  pl.BlockSpec((1, D_MODEL, tn), lambda l, j: (l, 0, j)),
                  pl.BlockSpec((1, 1, tn), lambda l, j: (l, 0, j))],
        out_specs=pl.BlockSpec((1, rows, tn), lambda l, j: (l, 0, j)),
        out_shape=jax.ShapeDtypeStruct((DEPTH, rows, 6 * D_MODEL), F32),
        compiler_params=_cparams(("arbitrary", "arbitrary")),
        name="adaln",
    )(cc, ada_w, ada_b.reshape(DEPTH, 1, 6 * D_MODEL))


def _lru_in_kernel(xp_ref, x_ref, xn_ref, gam_ref, sh_ref, sc_ref, w_ref, cw_ref, cb_ref,
                   g_ref, xc_ref, *, tt):
    i = pl.program_id(1)
    n = pl.num_programs(1)
    xe = jnp.concatenate([xp_ref[0], x_ref[0], xn_ref[0]], axis=0)
    he = _rms_mod(xe, gam_ref[...], sh_ref[0], sc_ref[0]).astype(BF16)
    z = _dot(he, w_ref[...])
    g_ref[0] = _gelu_tanh(z[HALO:HALO + tt, :D_RNN]).astype(BF16)
    z2 = z[:, D_RNN:]
    row = lax.broadcasted_iota(I32, (tt + 2 * HALO, 1), 0)
    valid = jnp.logical_and(jnp.logical_or(row >= HALO, i > 0),
                            jnp.logical_or(row < tt + HALO, i < n - 1))
    z2 = jnp.where(valid, z2, 0.0)
    cw = cw_ref[...]
    base = HALO - CONV_W // 2
    acc = z2[base:base + tt] * cw[0:1]
    for k in range(1, CONV_W):
        acc = acc + z2[base + k:base + k + tt] * cw[k:k + 1]
    xc_ref[0] = (acc + cb_ref[...]).astype(BF16)


def _lru_in(x, gamma, shift, scale, w_in, conv_w, conv_b, tt):
    B, L, D = x.shape
    nt = L // tt
    hb = tt // HALO
    last = L // HALO - 1
    vec = pl.BlockSpec((1, D), lambda b, i: (0, 0))
    mod = pl.BlockSpec((1, 1, D), lambda b, i: (b, 0, 0))
    out = pl.BlockSpec((1, tt, D_RNN), lambda b, i: (b, i, 0))
    return pl.pallas_call(
        functools.partial(_lru_in_kernel, tt=tt),
        grid=(B, nt),
        in_specs=[pl.BlockSpec((1, HALO, D), lambda b, i: (b, jnp.maximum(i * hb - 1, 0), 0)),
                  pl.BlockSpec((1, tt, D), lambda b, i: (b, i, 0)),
                  pl.BlockSpec((1, HALO, D), lambda b, i: (b, jnp.minimum((i + 1) * hb, last), 0)),
                  vec, mod, mod,
                  pl.BlockSpec((D, 2 * D_RNN), lambda b, i: (0, 0)),
                  pl.BlockSpec((CONV_W, D_RNN), lambda b, i: (0, 0)),
                  pl.BlockSpec((1, D_RNN), lambda b, i: (0, 0))],
        out_specs=[out, out],
        out_shape=[jax.ShapeDtypeStruct((B, L, D_RNN), BF16)] * 2,
        compiler_params=_cparams(("arbitrary", "arbitrary")),
        name="lru_in",
    )(x, x, x, gamma, shift, scale, w_in, conv_w, conv_b)


def _scan_kernel(xc_ctx_ref, xc_lat_ref, wg_ref, bg_ref, lam_ref, y_ctx_ref, y_lat_ref,
                 a_s, u_s, y_s, h_s, *, reverse, n_ctx):
    i = pl.program_id(0)
    nb, tt, _ = a_s.shape

    @pl.when(i == 0)
    def _():
        h_s[...] = jnp.zeros_like(h_s)

    def tile(x_ref, y_ref):
        x2 = x_ref[...].reshape(nb * tt, D_RNN)
        nlam = -lam_ref[...]
        sp = jnp.maximum(nlam, 0.0) + jnp.log1p(jnp.exp(-jnp.abs(nlam)))
        k2 = sp * (-0.5 * LRU_C * math.log2(math.e))
        for n in range(LRU_BLOCKS):
            sl = slice(n * LRU_BLOCK, (n + 1) * LRU_BLOCK)
            xb = x2[:, sl]
            g = _dot(xb, wg_ref[n])
            tr = jnp.tanh(g[:, :LRU_BLOCK] + bg_ref[0:1, sl])
            ti = jnp.tanh(g[:, LRU_BLOCK:] + bg_ref[1:2, sl])
            a = jnp.exp2(k2[:, sl] * tr + k2[:, sl])
            v = 1.0 - a * a
            root = jnp.where(v > 0.0, v * lax.rsqrt(v), 0.0)
            u = root * ((ti + 1.0) * xb.astype(F32))
            a_s[:, :, sl] = a.reshape(nb, tt, LRU_BLOCK)
            u_s[:, :, sl] = u.reshape(nb, tt, LRU_BLOCK)
        h = h_s[...]
        steps = range(tt - 1, -1, -1) if reverse else range(tt)
        for t in steps:
            h = a_s[:, t, :] * h + u_s[:, t, :]
            y_s[:, t, :] = h
        h_s[...] = h
        y_ref[...] = y_s[...].astype(BF16)

    @pl.when(i < n_ctx)
    def _():
        tile(xc_ctx_ref, y_ctx_ref)

    @pl.when(i >= n_ctx)
    def _():
        tile(xc_lat_ref, y_lat_ref)


def _lru_scan(xc_ctx, xc_lat, wg, bg, lam, reverse):
    B, Lc, _ = xc_ctx.shape
    L = xc_lat.shape[1]
    tt = SCAN_TT
    n_ctx = Lc // tt
    n_lat = L // tt

    def ctx_map(i):
        j = jnp.minimum(i, n_ctx - 1)
        return (0, (n_ctx - 1 - j) if reverse else j, 0)

    def lat_map(i):
        j = jnp.maximum(i - n_ctx, 0)
        return (0, (n_lat - 1 - j) if reverse else j, 0)

    return pl.pallas_call(
        functools.partial(_scan_kernel, reverse=reverse, n_ctx=n_ctx),
        grid=(n_ctx + n_lat,),
        in_specs=[pl.BlockSpec((B, tt, D_RNN), ctx_map),
                  pl.BlockSpec((B, tt, D_RNN), lat_map),
                  pl.BlockSpec((LRU_BLOCKS, LRU_BLOCK, 2 * LRU_BLOCK), lambda i: (0, 0, 0)),
                  pl.BlockSpec((2, D_RNN), lambda i: (0, 0)),
                  pl.BlockSpec((1, D_RNN), lambda i: (0, 0))],
        out_specs=[pl.BlockSpec((B, tt, D_RNN), ctx_map),
                   pl.BlockSpec((B, tt, D_RNN), lat_map)],
        out_shape=[jax.ShapeDtypeStruct((B, Lc, D_RNN), BF16),
                   jax.ShapeDtypeStruct((B, L, D_RNN), BF16)],
        scratch_shapes=[pltpu.VMEM((B, tt, D_RNN), F32),
                        pltpu.VMEM((B, tt, D_RNN), F32),
                        pltpu.VMEM((B, tt, D_RNN), F32),
                        pltpu.VMEM((B, D_RNN), F32)],
        compiler_params=_cparams(("arbitrary",)),
        name="lru_scan_rev" if reverse else "lru_scan_fwd",
    )(xc_ctx, xc_lat, wg, bg, lam)


def _mix_out_kernel(*refs, lru):
    if lru:
        (yf_ref, yb_ref, g_ref, x_ref, w_ref, g1_ref, gam_ref, sh_ref, sc_ref, r_ref,
         xo_ref, h_ref, aff_ref) = refs
        y = yf_ref[0].astype(F32) + yb_ref[0].astype(F32)
        lhs = (y * g_ref[0].astype(F32)).astype(BF16)
    else:
        (a_ref, x_ref, w_ref, g1_ref, gam_ref, sh_ref, sc_ref, r_ref,
         xo_ref, h_ref, aff_ref) = refs
        lhs = a_ref[0]
    xn = x_ref[0] + g1_ref[0] * _dot(lhs, w_ref[...])
    xo_ref[0] = xn
    h = _rms_mod(xn, gam_ref[...], sh_ref[0], sc_ref[0]).astype(BF16)
    h_ref[0] = h
    logits = _dot(h, r_ref[...])
    lane = lax.broadcasted_iota(I32, logits.shape, 1)
    logits = jnp.where(lane < N_EXPERTS, logits, NEG_BIG)
    p = jnp.exp(logits - jnp.max(logits, axis=-1, keepdims=True))
    aff_ref[0] = p / jnp.sum(p, axis=-1, keepdims=True)


def _mix_out(acts, x, w_out, g1, gamma, shift, scale, router_pad, tm):
    B, L, D = x.shape
    row = pl.BlockSpec((1, tm, D), lambda b, i: (b, i, 0))
    vec = pl.BlockSpec((1, D), lambda b, i: (0, 0))
    mod = pl.BlockSpec((1, 1, D), lambda b, i: (b, 0, 0))
    return pl.pallas_call(
        functools.partial(_mix_out_kernel, lru=len(acts) == 3),
        grid=(B, L // tm),
        in_specs=[row] * len(acts) + [
            row,
            pl.BlockSpec((D, D), lambda b, i: (0, 0)),
            mod, vec, mod, mod,
            pl.BlockSpec((D, LANES), lambda b, i: (0, 0))],
        out_specs=[row, row, pl.BlockSpec((1, tm, LANES), lambda b, i: (b, i, 0))],
        out_shape=[jax.ShapeDtypeStruct((B, L, D), F32),
                   jax.ShapeDtypeStruct((B, L, D), BF16),
                   jax.ShapeDtypeStruct((B, L, LANES), F32)],
        compiler_params=_cparams(("arbitrary", "arbitrary")),
        name="mix_out_lru" if len(acts) == 3 else "mix_out_attn",
    )(*acts, x, w_out, g1, gamma, shift, scale, router_pad)


def _topk_kernel(a_ref, offs_ref, slot_ref, *, cap, rb):
    L, ncol = a_ref.shape
    a = a_ref[...]

    def as_f32(bits):
        return lax.bitcast_convert_type(bits, F32)

    def search(k, thr):
        cand = thr | lax.shift_left(jnp.int32(1), 30 - k)
        cnt = jnp.sum((a >= as_f32(cand)).astype(I32), axis=0, keepdims=True)
        return jnp.where(cnt >= cap, cand, thr)

    thr = lax.fori_loop(0, 31, search, jnp.zeros((1, ncol), I32))
    lo = as_f32(thr)
    hi = as_f32(thr + 1)
    need = (cap - jnp.sum((a >= hi).astype(I32), axis=0, keepdims=True)).astype(F32)
    eq_b = jnp.logical_and(a >= lo, a < hi).astype(F32).astype(BF16)
    col = lax.broadcasted_iota(I32, (rb, L), 1)
    rowi = lax.broadcasted_iota(I32, (rb, L), 0)

    def before(r0):
        return (col < rowi + r0).astype(F32).astype(BF16)

    sel_blocks = []
    for r0 in range(0, L, rb):
        blk = a[r0:r0 + rb]
        eq_rank = _dot(before(r0), eq_b)
        take = jnp.logical_or(blk >= hi, jnp.logical_and(blk >= lo, eq_rank < need))
        sel_blocks.append(take.astype(F32))
    sel = jnp.concatenate(sel_blocks, axis=0) if len(sel_blocks) > 1 else sel_blocks[0]
    sel_b = sel.astype(BF16)
    offs = offs_ref[...]
    for k, r0 in enumerate(range(0, L, rb)):
        pos = _dot(before(r0), sel_b).astype(I32) + offs
        slot_ref[r0:r0 + rb, :] = jnp.where(sel_blocks[k] > 0.0, pos, -1)


def _topk_slots(aff_cols, offs, cap):
    L, ncol = aff_cols.shape
    return pl.pallas_call(
        functools.partial(_topk_kernel, cap=cap, rb=min(L, 256)),
        grid=(1,),
        in_specs=[pl.BlockSpec((L, ncol), lambda i: (0, 0)),
                  pl.BlockSpec((1, ncol), lambda i: (0, 0))],
        out_specs=pl.BlockSpec((L, ncol), lambda i: (0, 0)),
        out_shape=jax.ShapeDtypeStruct((L, ncol), I32),
        compiler_params=_cparams(("arbitrary",)),
        name="topk_slots",
    )(aff_cols, offs)


def _moe_ffn_kernel(h_ref, slot_ref, gate_ref, wu_ref, wd_ref, yg_ref, wu_s, wd_s, *, n_slots):
    rows = h_ref.shape[1]

    @pl.when(pl.program_id(1) == 0)
    def _():
        wu_s[...] = wu_ref[0, 0].astype(BF16)
        wd_s[...] = wd_ref[0, 0].astype(BF16)

    hit = lax.broadcasted_iota(I32, (n_slots, rows), 0) == slot_ref[0, 0]
    onehot = hit.astype(F32).astype(BF16)
    gate_s = jnp.sum(jnp.where(hit, gate_ref[0, 0], 0.0), axis=1, keepdims=True)
    xg = _dot(onehot, h_ref[0]).astype(BF16)
    up = _dot(xg, wu_s[...])
    hid = (_silu(up[:, :D_EXPERT]) * up[:, D_EXPERT:]).astype(BF16)
    yg_ref[0, 0] = (_dot(hid, wd_s[...]) * gate_s).astype(BF16)


def _moe_ffn(h_g, slot_g, gate_g, w_up, w_down, layer, n_slots):
    ng, rows, D = h_g.shape
    route = pl.BlockSpec((1, 1, 1, rows), lambda e, g: (g, e, 0, 0))
    return pl.pallas_call(
        functools.partial(_moe_ffn_kernel, n_slots=n_slots),
        grid=(N_EXPERTS, ng),
        in_specs=[pl.BlockSpec((1, rows, D), lambda e, g: (g, 0, 0)),
                  route, route,
                  pl.BlockSpec((1, 1, D, 2 * D_EXPERT), lambda e, g: (layer, e, 0, 0)),
                  pl.BlockSpec((1, 1, D_EXPERT, D), lambda e, g: (layer, e, 0, 0))],
        out_specs=pl.BlockSpec((1, 1, n_slots, D), lambda e, g: (g, e, 0, 0)),
        out_shape=jax.ShapeDtypeStruct((ng, N_EXPERTS, n_slots, D), BF16),
        scratch_shapes=[pltpu.VMEM((D, 2 * D_EXPERT), BF16),
                        pltpu.VMEM((D_EXPERT, D), BF16)],
        compiler_params=_cparams(("arbitrary", "arbitrary")),
        name="moe_ffn",
    )(h_g, slot_g, gate_g, w_up, w_down)


def _moe_combine_kernel(yg_ref, slot_ref, x_ref, g2_ref, gam_ref, o_ref, *, final):
    _, n_e, n_slots, D = yg_ref.shape
    chunk = x_ref.shape[1]
    slots = slot_ref[0]
    lane = lax.broadcasted_iota(I32, (chunk, n_slots), 1)
    pieces = [(slots[:, e:e + 1] == lane).astype(F32).astype(BF16) for e in range(n_e)]
    onehot_t = jnp.concatenate(pieces, axis=1)
    comb = _dot(onehot_t, yg_ref[0].reshape(n_e * n_slots, D))
    xn = x_ref[0] + g2_ref[0] * comb
    o_ref[0] = _rms(xn, gam_ref[...]) if final else xn


def _moe_combine(yg, slot_t, x_g, g2, gamma, L, final):
    ng, n_e, n_slots, D = yg.shape
    rows = x_g.shape[1]
    chunk = min(512, L)
    row = pl.BlockSpec((1, chunk, D), lambda g, c: (g, c, 0))
    return pl.pallas_call(
        functools.partial(_moe_combine_kernel, final=final),
        grid=(ng, rows // chunk),
        in_specs=[pl.BlockSpec((1, n_e, n_slots, D), lambda g, c: (g, 0, 0, 0)),
                  pl.BlockSpec((1, chunk, LANES), lambda g, c: (g, c, 0)),
                  row,
                  pl.BlockSpec((1, 1, D), lambda g, c: ((g * rows + c * chunk) // L, 0, 0)),
                  pl.BlockSpec((1, D), lambda g, c: (0, 0))],
        out_specs=row,
        out_shape=jax.ShapeDtypeStruct((ng, rows, D), F32),
        compiler_params=_cparams(("arbitrary", "arbitrary")),
        name="moe_combine_final" if final else "moe_combine",
    )(yg, slot_t, x_g, g2, gamma)


def _ec_moe(x, h, aff, w_up, w_down, layer, g2, gamma, final):
    B, L, D = h.shape
    E = N_EXPERTS
    cap = CAPACITY_FACTOR * L // E
    G = MOE_ROWS // L
    ng = B // G
    cols = aff[:, :, :E].transpose(1, 0, 2).reshape(L, B * E)
    offs = jnp.repeat((jnp.arange(B, dtype=I32) % G) * cap, E).reshape(1, B * E)
    slot = _topk_slots(cols, offs, cap)

    def by_lane(a):
        return a.reshape(L, ng, G, E).transpose(1, 3, 2, 0).reshape(ng, E, 1, G * L)

    yg = _moe_ffn(h.reshape(ng, G * L, D), by_lane(slot), by_lane(cols), w_up, w_down, layer, G * cap)
    slot_t = slot.reshape(L, ng, G, E).transpose(1, 2, 0, 3).reshape(ng, G * L, E)
    slot_t = jnp.pad(slot_t, ((0, 0), (0, 0), (0, LANES - E)), constant_values=-1)
    out = _moe_combine(yg, slot_t, x.reshape(ng, G * L, D), g2, gamma, L, final)
    return out.reshape(B, L, D)


def _mla_in_kernel(*refs, need_q):
    (x_ref, gam_ref, sh_ref, sc_ref, win_ref, qn_ref, kvn_ref,
     wqa_ref, wqb_ref, wk_ref, wv_ref, vone_ref, cos_ref, sin_ref) = refs[:14]
    outs = refs[14:]
    h = _rms_mod(x_ref[0], gam_ref[...], sh_ref[0], sc_ref[0]).astype(BF16)
    z = _dot(h, win_ref[...])
    cos = cos_ref[...]
    sin = sin_ref[...]
    kv0 = Q_LORA + KV_LORA
    ckv = _rms(z[:, Q_LORA:kv0], kvn_ref[...]).astype(BF16)
    k_rope = z[:, kv0:kv0 + HEAD_PAD] * cos + z[:, kv0 + HEAD_PAD:kv0 + 2 * HEAD_PAD] * sin
    k_nope = _dot(ckv, wk_ref[...])
    if need_q:
        q_ref, k_ref, v_ref = outs
        cq = _rms(z[:, :Q_LORA], qn_ref[...]).astype(BF16)
        qa = _dot(cq, wqa_ref[...])
        qb = _dot(cq, wqb_ref[...])
    else:
        k_ref, v_ref = outs
    for hh in range(MLA_HEADS):
        sl = slice(hh * HEAD_PAD, (hh + 1) * HEAD_PAD)
        k_ref[0, :, sl] = (k_nope[:, sl] + k_rope).astype(BF16)
        if need_q:
            q_ref[0, :, sl] = (qa[:, sl] * cos + qb[:, sl] * sin).astype(BF16)
    v_ref[0] = (_dot(ckv, wv_ref[...]) + vone_ref[...]).astype(BF16)


def _mla_in(x, gamma, shift, scale, w, cos_t, sin_t, need_q, tm):
    B, L, D = x.shape
    row = pl.BlockSpec((1, tm, D), lambda b, i: (b, i, 0))
    mod = pl.BlockSpec((1, 1, D), lambda b, i: (b, 0, 0))

    def full(a):
        return pl.BlockSpec(a.shape, lambda b, i: (0,) * a.ndim)

    hp = MLA_HEADS * HEAD_PAD
    wide = pl.BlockSpec((1, tm, hp), lambda b, i: (b, i, 0))
    tab = pl.BlockSpec((tm, HEAD_PAD), lambda b, i: (i, 0))
    out_specs = [wide, wide]
    out_shape = [jax.ShapeDtypeStruct((B, L, hp), BF16)] * 2
    if need_q:
        out_specs = [wide] + out_specs
        out_shape = [jax.ShapeDtypeStruct((B, L, hp), BF16)] + out_shape
    weights = [w["win"], w["qn"], w["kvn"], w["wqa"], w["wqb"], w["wk"], w["wv"], w["vone"]]
    return pl.pallas_call(
        functools.partial(_mla_in_kernel, need_q=need_q),
        grid=(B, L // tm),
        in_specs=[row, full(gamma), mod, mod] + [full(a) for a in weights] + [tab, tab],
        out_specs=out_specs,
        out_shape=out_shape,
        compiler_params=_cparams(("arbitrary", "arbitrary")),
        name="mla_in_lat" if need_q else "mla_in_ctx",
    )(x, gamma, shift, scale, *weights, cos_t, sin_t)


def _attn_kernel(q_ref, kl_ref, kc_ref, vl_ref, vc_ref, o_ref):
    tq = q_ref.shape[1]
    blocks = [(j, r) for r in range(0, tq, ATTN_QB) for j in range(2)]

    def scores(j, r):
        sl = slice(j * HEAD_PAD, (j + 1) * HEAD_PAD)
        q = q_ref[0, r:r + ATTN_QB, sl]
        return _dot_nt(q, kl_ref[0, :, sl]), _dot_nt(q, kc_ref[0, :, sl])

    def weighted(j, s1, s2):
        sl = slice(j * HEAD_PAD, (j + 1) * HEAD_PAD)
        m = jnp.maximum(jnp.max(s1, axis=-1, keepdims=True), jnp.max(s2, axis=-1, keepdims=True))
        p1 = jnp.exp2((s1 - m) * EXP2_SCALE).astype(BF16)
        p2 = jnp.exp2((s2 - m) * EXP2_SCALE).astype(BF16)
        return _dot(p1, vl_ref[0, :, sl]) + _dot(p2, vc_ref[0, :, sl])

    pending = scores(*blocks[0])
    acc = {}
    for n, (j, r) in enumerate(blocks):
        s1, s2 = pending
        if n + 1 < len(blocks):
            pending = scores(*blocks[n + 1])
        acc[(j, r)] = weighted(j, s1, s2)
    lane = lax.broadcasted_iota(I32, (ATTN_QB, 2 * V_DIM), 1)
    for r in range(0, tq, ATTN_QB):
        even = acc[(0, r)]
        odd = acc[(1, r)]
        even = even / even[:, V_DIM:V_DIM + 1]
        odd = odd / odd[:, 0:1]
        o_ref[0, r:r + ATTN_QB, :] = jnp.where(lane < V_DIM, even, odd).astype(BF16)


def _attention(q, k_lat, k_ctx, v_lat, v_ctx, tq):
    B, L, _ = q.shape
    Lc = k_ctx.shape[1]
    pair = 2 * HEAD_PAD
    return pl.pallas_call(
        _attn_kernel,
        grid=(B, MLA_HEADS // 2, L // tq),
        in_specs=[pl.BlockSpec((1, tq, pair), lambda b, h, i: (b, i, h)),
                  pl.BlockSpec((1, L, pair), lambda b, h, i: (b, 0, h)),
                  pl.BlockSpec((1, Lc, pair), lambda b, h, i: (b, 0, h)),
                  pl.BlockSpec((1, L, pair), lambda b, h, i: (b, 0, h)),
                  pl.BlockSpec((1, Lc, pair), lambda b, h, i: (b, 0, h))],
        out_specs=pl.BlockSpec((1, tq, 2 * V_DIM), lambda b, h, i: (b, i, h)),
        out_shape=jax.ShapeDtypeStruct((B, L, MLA_HEADS * V_DIM), BF16),
        compiler_params=_cparams(("arbitrary", "arbitrary", "arbitrary")),
        name="attention",
    )(q, k_lat, k_ctx, v_lat, v_ctx)


def _rope_tables(rows):
    t = np.arange(rows * GRID_W)
    pos = np.stack([t // GRID_W, t % GRID_W], axis=-1).astype(np.float32)
    freq = jnp.asarray(ROPE_BASE, F32) ** (-jnp.arange(ROPE_FREQS, dtype=F32) / ROPE_FREQS)
    ang = jnp.asarray(pos)[:, :, None] * freq
    cos, sin = jnp.cos(ang), jnp.sin(ang)
    cos_r = jnp.stack([cos, cos], axis=2).reshape(-1, QK_ROPE)
    sin_r = jnp.stack([-sin, sin], axis=2).reshape(-1, QK_ROPE)
    T = cos_r.shape[0]
    pad = HEAD_PAD - QK_DIM
    cos_t = jnp.concatenate([jnp.ones((T, QK_NOPE), F32), cos_r, jnp.zeros((T, pad), F32)], axis=1)
    sin_t = jnp.concatenate([jnp.zeros((T, QK_NOPE), F32), sin_r, jnp.zeros((T, pad), F32)], axis=1)
    return cos_t, sin_t


def _rope_partner():
    p = np.arange(QK_ROPE)
    half = (p % (2 * ROPE_FREQS)) // ROPE_FREQS
    return np.where(half == 0, p + ROPE_FREQS, p - ROPE_FREQS)


def _mla_weights(w_in, q_norm, kv_norm, w_uq, w_ukv):
    partner = _rope_partner()
    pad = HEAD_PAD - QK_DIM
    kv0 = Q_LORA + KV_LORA
    kr = w_in[:, kv0:]
    zl = jnp.zeros((D_MODEL, QK_NOPE), F32)
    zr = jnp.zeros((D_MODEL, pad), F32)
    win = jnp.concatenate([w_in[:, :kv0], zl, kr, zr, zl, kr[:, partner], zr], axis=1)
    wq = w_uq.reshape(Q_LORA, MLA_HEADS, QK_DIM)
    zq = jnp.zeros((Q_LORA, MLA_HEADS, pad), F32)
    wqa = jnp.concatenate([wq, zq], axis=2)
    wqb = jnp.concatenate([jnp.zeros((Q_LORA, MLA_HEADS, QK_NOPE), F32),
                           wq[:, :, QK_NOPE:][:, :, partner], zq], axis=2)
    wkv = w_ukv.reshape(KV_LORA, MLA_HEADS, QK_NOPE + V_DIM)
    wk = jnp.concatenate([wkv[:, :, :QK_NOPE],
                          jnp.zeros((KV_LORA, MLA_HEADS, HEAD_PAD - QK_NOPE), F32)], axis=2)
    zv = jnp.zeros((KV_LORA, MLA_HEADS // 2, HEAD_PAD - V_DIM), F32)
    wv2 = wkv[:, :, QK_NOPE:].reshape(KV_LORA, MLA_HEADS // 2, 2, V_DIM)
    wv = jnp.stack([jnp.concatenate([wv2[:, :, 0], zv], axis=2),
                    jnp.concatenate([zv, wv2[:, :, 1]], axis=2)], axis=2)
    hp = MLA_HEADS * HEAD_PAD
    vone = np.zeros((MLA_HEADS // 2, 2, HEAD_PAD), np.float32)
    vone[:, 0, V_DIM] = 1.0
    vone[:, 1, 0] = 1.0
    return {"win": win.astype(BF16),
            "qn": q_norm.reshape(1, Q_LORA), "kvn": kv_norm.reshape(1, KV_LORA),
            "wqa": wqa.reshape(Q_LORA, hp).astype(BF16),
            "wqb": wqb.reshape(Q_LORA, hp).astype(BF16),
            "wk": wk.reshape(KV_LORA, hp).astype(BF16),
            "wv": wv.reshape(KV_LORA, hp).astype(BF16),
            "vone": jnp.asarray(vone.reshape(1, hp))}


def _pad_router(r):
    return jnp.pad(r, ((0, 0), (0, LANES - N_EXPERTS))).astype(BF16)


def kernel(x, c, ctx, c_ctx, ada_w, ada_b, norm_mix, norm_ffn, norm_final, lru_w_in, lru_conv_w, lru_conv_b, lru_gate_w, lru_gate_b, lru_lambda, lru_w_out, mla_w_in, mla_q_norm, mla_kv_norm, mla_w_uq, mla_w_ukv, mla_w_o, moe_router, moe_w_up, moe_w_down):
    B, T, D = x.shape
    Lc = ctx.shape[1]

    rows = ((B + 1 + 7) // 8) * 8
    cc = jnp.concatenate([c, c_ctx[None, :], jnp.zeros((rows - B - 1, D), F32)], axis=0)
    ada = _adaln(cc, ada_w, ada_b).reshape(DEPTH, rows, 6, D)

    def mods(i):
        lat = [ada[i, :B, k].reshape(B, 1, D) for k in range(6)]
        cx = [jnp.broadcast_to(ada[i, B, k].reshape(1, 1, D), (B, 1, D)) for k in range(6)]
        return lat, cx

    vec = lambda a: a.reshape(1, -1)

    (s1, sc1, g1, s2, sc2, g2), (s1c, sc1c, g1c, s2c, sc2c, g2c) = mods(0)
    w_in = lru_w_in[0].astype(BF16)
    cw, cb = 0.5 * lru_conv_w[0], vec(0.5 * lru_conv_b[0])
    g_lat, xc_lat = _lru_in(x, vec(norm_mix[0]), s1, sc1, w_in, cw, cb, tt=512)
    g_ctx, xc_ctx = _lru_in(ctx, vec(norm_mix[0]), s1c, sc1c, w_in, cw, cb, tt=Lc)
    ys = []
    for d in range(2):
        gw = lru_gate_w[0, d]
        wg = jnp.concatenate([gw[0], gw[1]], axis=-1).astype(BF16)
        ys.append(_lru_scan(xc_ctx, xc_lat, wg, 0.5 * lru_gate_b[0, d], vec(lru_lambda[0, d]),
                            reverse=(d == 1)))
    w_out = lru_w_out[0].astype(BF16)
    router = _pad_router(moe_router[0])
    gf = vec(norm_ffn[0])
    x_a, h_lat, aff_lat = _mix_out((ys[0][1], ys[1][1], g_lat), x, w_out, g1, gf, s2, sc2, router, tm=512)
    c_a, h_ctx, aff_ctx = _mix_out((ys[0][0], ys[1][0], g_ctx), ctx, w_out, g1c, gf, s2c, sc2c, router, tm=Lc)
    x_b = _ec_moe(x_a, h_lat, aff_lat, moe_w_up, moe_w_down, 0, g2, gf, final=False)
    c_b = _ec_moe(c_a, h_ctx, aff_ctx, moe_w_up, moe_w_down, 0, g2c, gf, final=False)

    (s1, sc1, g1, s2, sc2, g2), (s1c, sc1c, _, _, _, _) = mods(1)
    w = _mla_weights(mla_w_in[0], mla_q_norm[0], mla_kv_norm[0], mla_w_uq[0], mla_w_ukv[0])
    cos_t, sin_t = _rope_tables(T // GRID_W)
    one_t = jnp.concatenate([jnp.ones((Lc, QK_DIM), F32), jnp.zeros((Lc, HEAD_PAD - QK_DIM), F32)], axis=1)
    gm = vec(norm_mix[1])
    q, k_lat, v_lat = _mla_in(x_b, gm, s1, sc1, w, cos_t, sin_t, True, tm=512)
    k_ctx, v_ctx = _mla_in(c_b, gm, s1c, sc1c, w, one_t, jnp.zeros_like(one_t), False, tm=Lc)
    attn = _attention(q, k_lat, k_ctx, v_lat, v_ctx, tq=512)
    x_c, h_lat, aff_lat = _mix_out((attn,), x_b, mla_w_o[0].astype(BF16), g1, vec(norm_ffn[1]),
                                   s2, sc2, _pad_router(moe_router[1]), tm=512)
    return _ec_moe(x_c, h_lat, aff_lat, moe_w_up, moe_w_down, 1, g2, vec(norm_final), final=True)
```

```python
import functools
import math

import numpy as np
import jax
import jax.numpy as jnp
from jax import lax
from jax.experimental import pallas as pl
from jax.experimental.pallas import tpu as pltpu

F32 = jnp.float32
BF16 = jnp.bfloat16
I32 = jnp.int32

D_MODEL = 1024
DEPTH = 2
GRID_W = 64
D_RNN = D_MODEL
CONV_W = 4
LRU_BLOCKS = 8
LRU_BLOCK = D_RNN // LRU_BLOCKS
LRU_C = 8.0
MLA_HEADS = 16
QK_NOPE = 64
QK_ROPE = 32
QK_DIM = QK_NOPE + QK_ROPE
V_DIM = 64
Q_LORA = 384
KV_LORA = 256
ROPE_FREQS = QK_ROPE // 4
ROPE_BASE = 10000.0
ATTN_SCALE = QK_DIM ** -0.5
N_EXPERTS = 16
D_EXPERT = 1024
CAPACITY_FACTOR = 2
EPS = 1e-6

LANES = 128
HEAD_PAD = LANES
VMEM_LIMIT = 56 * 1024 * 1024
NEG_BIG = -1e30
HALO = 8
MOE_ROWS = 2048
SCAN_TT = 64
COMBINE_ROWS = 256
COMBINE_WIN = 64
SLOT_ALIGN = 16
ATTN_QB = 256
EXP2_SCALE = ATTN_SCALE * math.log2(math.e)


def _cparams(sem):
    return pltpu.CompilerParams(dimension_semantics=sem, vmem_limit_bytes=VMEM_LIMIT)


def _sigmoid(x):
    return 0.5 * (jnp.tanh(0.5 * x) + 1.0)


def _silu(x):
    return x * _sigmoid(x)


def _gelu_tanh(x):
    return 0.5 * x * (1.0 + jnp.tanh(0.7978845608028654 * (x + 0.044715 * (x * x * x))))


def _rms(x, gamma):
    return x * lax.rsqrt(jnp.mean(x * x, axis=-1, keepdims=True) + EPS) * gamma


def _rms_mod(x, gamma, shift, scale):
    return _rms(x, gamma) * (1.0 + scale) + shift


def _dot(a, b):
    return jnp.dot(a, b, preferred_element_type=F32)


def _dot_nt(a, b):
    return lax.dot_general(a, b, (((1,), (1,)), ((), ())), preferred_element_type=F32)


def _adaln_kernel(c_ref, w_ref, b_ref, o_ref):
    s = _silu(c_ref[...]).astype(BF16)
    o_ref[0] = _dot(s, w_ref[0].astype(BF16)) + b_ref[0]


def _adaln(cc, ada_w, ada_b):
    rows = cc.shape[0]
    tn = 1024
    return pl.pallas_call(
        _adaln_kernel,
        grid=(DEPTH, 6 * D_MODEL // tn),
        in_specs=[pl.BlockSpec((rows, D_MODEL), lambda l, j: (0, 0)),
                  pl.BlockSpec((1, D_MODEL, tn), lambda l, j: (l, 0, j)),
                  pl.BlockSpec((1, 1, tn), lambda l, j: (l, 0, j))],
        out_specs=pl.BlockSpec((1, rows, tn), lambda l, j: (l, 0, j)),
        out_shape=jax.ShapeDtypeStruct((DEPTH, rows, 6 * D_MODEL), F32),
        compiler_params=_cparams(("arbitrary", "arbitrary")),
        name="adaln",
    )(cc, ada_w, ada_b.reshape(DEPTH, 1, 6 * D_MODEL))


def _lru_in_kernel(xp_ref, x_ref, xn_ref, gam_ref, sh_ref, sc_ref, w_ref, cw_ref, cb_ref,
                   g_ref, xc_ref, *, tt):
    i = pl.program_id(1)
    n = pl.num_programs(1)
    xe = jnp.concatenate([xp_ref[0], x_ref[0], xn_ref[0]], axis=0)
    he = _rms_mod(xe, gam_ref[...], sh_ref[0], sc_ref[0]).astype(BF16)
    z = _dot(he, w_ref[...])
    g_ref[0] = _gelu_tanh(z[HALO:HALO + tt, :D_RNN]).astype(BF16)
    z2 = z[:, D_RNN:]
    row = lax.broadcasted_iota(I32, (tt + 2 * HALO, 1), 0)
    valid = jnp.logical_and(jnp.logical_or(row >= HALO, i > 0),
                            jnp.logical_or(row < tt + HALO, i < n - 1))
    z2 = jnp.where(valid, z2, 0.0)
    cw = cw_ref[...]
    base = HALO - CONV_W // 2
    acc = z2[base:base + tt] * cw[0:1]
    for k in range(1, CONV_W):
        acc = acc + z2[base + k:base + k + tt] * cw[k:k + 1]
    xc_ref[0] = (acc + cb_ref[...]).astype(BF16)


def _lru_in(x, gamma, shift, scale, w_in, conv_w, conv_b, tt):
    B, L, D = x.shape
    nt = L // tt
    hb = tt // HALO
    last = L // HALO - 1
    vec = pl.BlockSpec((1, D), lambda b, i: (0, 0))
    mod = pl.BlockSpec((1, 1, D), lambda b, i: (b, 0, 0))
    out = pl.BlockSpec((1, tt, D_RNN), lambda b, i: (b, i, 0))
    return pl.pallas_call(
        functools.partial(_lru_in_kernel, tt=tt),
        grid=(B, nt),
        in_specs=[pl.BlockSpec((1, HALO, D), lambda b, i: (b, jnp.maximum(i * hb - 1, 0), 0)),
                  pl.BlockSpec((1, tt, D), lambda b, i: (b, i, 0)),
                  pl.BlockSpec((1, HALO, D), lambda b, i: (b, jnp.minimum((i + 1) * hb, last), 0)),
                  vec, mod, mod,
                  pl.BlockSpec((D, 2 * D_RNN), lambda b, i: (0, 0)),
                  pl.BlockSpec((CONV_W, D_RNN), lambda b, i: (0, 0)),
                  pl.BlockSpec((1, D_RNN), lambda b, i: (0, 0))],
        out_specs=[out, out],
        out_shape=[jax.ShapeDtypeStruct((B, L, D_RNN), BF16)] * 2,
        compiler_params=_cparams(("arbitrary", "arbitrary")),
        name="lru_in",
    )(x, x, x, gamma, shift, scale, w_in, conv_w, conv_b)


def _scan_kernel(xc_ctx_ref, xc_lat_ref, wg_ref, bg_ref, lam_ref, y_ctx_ref, y_lat_ref,
                 a_s, u_s, y_s, h_s, *, reverse, n_ctx):
    i = pl.program_id(0)
    nb, tt, _ = a_s.shape

    @pl.when(i == 0)
    def _():
        h_s[...] = jnp.zeros_like(h_s)

    def tile(x_ref, y_ref):
        x2 = x_ref[...].reshape(nb * tt, D_RNN)
        nlam = -lam_ref[...]
        sp = jnp.maximum(nlam, 0.0) + jnp.log1p(jnp.exp(-jnp.abs(nlam)))
        k2 = sp * (-0.5 * LRU_C * math.log2(math.e))
        for n in range(LRU_BLOCKS):
            sl = slice(n * LRU_BLOCK, (n + 1) * LRU_BLOCK)
            xb = x2[:, sl]
            g = _dot(xb, wg_ref[n])
            tr = jnp.tanh(g[:, :LRU_BLOCK] + bg_ref[0:1, sl])
            ti = jnp.tanh(g[:, LRU_BLOCK:] + bg_ref[1:2, sl])
            a = jnp.exp2(k2[:, sl] * tr + k2[:, sl])
            v = 1.0 - a * a
            root = jnp.where(v > 0.0, v * lax.rsqrt(v), 0.0)
            u = root * ((ti + 1.0) * xb.astype(F32))
            a_s[:, :, sl] = a.reshape(nb, tt, LRU_BLOCK)
            u_s[:, :, sl] = u.reshape(nb, tt, LRU_BLOCK)
        h = h_s[...]
        steps = range(tt - 1, -1, -1) if reverse else range(tt)
        for t in steps:
            h = a_s[:, t, :] * h + u_s[:, t, :]
            y_s[:, t, :] = h
        h_s[...] = h
        y_ref[...] = y_s[...].astype(BF16)

    @pl.when(i < n_ctx)
    def _():
        tile(xc_ctx_ref, y_ctx_ref)

    @pl.when(i >= n_ctx)
    def _():
        tile(xc_lat_ref, y_lat_ref)


def _lru_scan(xc_ctx, xc_lat, wg, bg, lam, reverse):
    B, Lc, _ = xc_ctx.shape
    L = xc_lat.shape[1]
    tt = SCAN_TT
    n_ctx = Lc // tt
    n_lat = L // tt

    def ctx_map(i):
        j = jnp.minimum(i, n_ctx - 1)
        return (0, (n_ctx - 1 - j) if reverse else j, 0)

    def lat_map(i):
        j = jnp.maximum(i - n_ctx, 0)
        return (0, (n_lat - 1 - j) if reverse else j, 0)

    return pl.pallas_call(
        functools.partial(_scan_kernel, reverse=reverse, n_ctx=n_ctx),
        grid=(n_ctx + n_lat,),
        in_specs=[pl.BlockSpec((B, tt, D_RNN), ctx_map),
                  pl.BlockSpec((B, tt, D_RNN), lat_map),
                  pl.BlockSpec((LRU_BLOCKS, LRU_BLOCK, 2 * LRU_BLOCK), lambda i: (0, 0, 0)),
                  pl.BlockSpec((2, D_RNN), lambda i: (0, 0)),
                  pl.BlockSpec((1, D_RNN), lambda i: (0, 0))],
        out_specs=[pl.BlockSpec((B, tt, D_RNN), ctx_map),
                   pl.BlockSpec((B, tt, D_RNN), lat_map)],
        out_shape=[jax.ShapeDtypeStruct((B, Lc, D_RNN), BF16),
                   jax.ShapeDtypeStruct((B, L, D_RNN), BF16)],
        scratch_shapes=[pltpu.VMEM((B, tt, D_RNN), F32),
                        pltpu.VMEM((B, tt, D_RNN), F32),
                        pltpu.VMEM((B, tt, D_RNN), F32),
                        pltpu.VMEM((B, D_RNN), F32)],
        compiler_params=_cparams(("arbitrary",)),
        name="lru_scan_rev" if reverse else "lru_scan_fwd",
    )(xc_ctx, xc_lat, wg, bg, lam)


def _mix_out_kernel(*refs, lru):
    if lru:
        (yf_ref, yb_ref, g_ref, x_ref, w_ref, g1_ref, gam_ref, sh_ref, sc_ref, r_ref,
         xo_ref, h_ref, aff_ref) = refs
        y = yf_ref[0].astype(F32) + yb_ref[0].astype(F32)
        lhs = (y * g_ref[0].astype(F32)).astype(BF16)
    else:
        (a_ref, x_ref, w_ref, g1_ref, gam_ref, sh_ref, sc_ref, r_ref,
         xo_ref, h_ref, aff_ref) = refs
        lhs = a_ref[0]
    xn = x_ref[0] + g1_ref[0] * _dot(lhs, w_ref[...])
    xo_ref[0] = xn
    h = _rms_mod(xn, gam_ref[...], sh_ref[0], sc_ref[0]).astype(BF16)
    h_ref[0] = h
    logits = _dot(h, r_ref[...])
    lane = lax.broadcasted_iota(I32, logits.shape, 1)
    logits = jnp.where(lane < N_EXPERTS, logits, NEG_BIG)
    p = jnp.exp(logits - jnp.max(logits, axis=-1, keepdims=True))
    aff_ref[0] = p / jnp.sum(p, axis=-1, keepdims=True)


def _mix_out(acts, x, w_out, g1, gamma, shift, scale, router_pad, tm):
    B, L, D = x.shape
    row = pl.BlockSpec((1, tm, D), lambda b, i: (b, i, 0))
    vec = pl.BlockSpec((1, D), lambda b, i: (0, 0))
    mod = pl.BlockSpec((1, 1, D), lambda b, i: (b, 0, 0))
    return pl.pallas_call(
        functools.partial(_mix_out_kernel, lru=len(acts) == 3),
        grid=(B, L // tm),
        in_specs=[row] * len(acts) + [
            row,
            pl.BlockSpec((D, D), lambda b, i: (0, 0)),
            mod, vec, mod, mod,
            pl.BlockSpec((D, LANES), lambda b, i: (0, 0))],
        out_specs=[row, row, pl.BlockSpec((1, tm, LANES), lambda b, i: (b, i, 0))],
        out_shape=[jax.ShapeDtypeStruct((B, L, D), F32),
                   jax.ShapeDtypeStruct((B, L, D), BF16),
                   jax.ShapeDtypeStruct((B, L, LANES), F32)],
        compiler_params=_cparams(("arbitrary", "arbitrary")),
        name="mix_out_lru" if len(acts) == 3 else "mix_out_attn",
    )(*acts, x, w_out, g1, gamma, shift, scale, router_pad)


def _topk_kernel(a_ref, offs_ref, slot_ref, first_ref, *, cap, rb):
    L, ncol = a_ref.shape
    a = a_ref[...]

    def as_f32(bits):
        return lax.bitcast_convert_type(bits, F32)

    def search(k, thr):
        cand = thr | lax.shift_left(jnp.int32(1), 30 - k)
        cnt = jnp.sum((a >= as_f32(cand)).astype(I32), axis=0, keepdims=True)
        return jnp.where(cnt >= cap, cand, thr)

    thr = lax.fori_loop(0, 31, search, jnp.zeros((1, ncol), I32))
    lo = as_f32(thr)
    hi = as_f32(thr + 1)
    need = (cap - jnp.sum((a >= hi).astype(I32), axis=0, keepdims=True)).astype(F32)
    eq_b = jnp.logical_and(a >= lo, a < hi).astype(F32).astype(BF16)
    col = lax.broadcasted_iota(I32, (rb, L), 1)
    rowi = lax.broadcasted_iota(I32, (rb, L), 0)

    def before(r0):
        return (col < rowi + r0).astype(F32).astype(BF16)

    sel_blocks = []
    for r0 in range(0, L, rb):
        blk = a[r0:r0 + rb]
        eq_rank = _dot(before(r0), eq_b)
        take = jnp.logical_or(blk >= hi, jnp.logical_and(blk >= lo, eq_rank < need))
        sel_blocks.append(take.astype(F32))
    sel = jnp.concatenate(sel_blocks, axis=0) if len(sel_blocks) > 1 else sel_blocks[0]
    sel_b = sel.astype(BF16)
    offs = offs_ref[...]
    for k, r0 in enumerate(range(0, L, rb)):
        pos = _dot(before(r0), sel_b).astype(I32) + offs
        slot_ref[r0:r0 + rb, :] = jnp.where(sel_blocks[k] > 0.0, pos, -1)
        first_ref[k:k + 1, :] = pos[0:1, :]


def _topk_slots(aff_cols, offs, cap):
    L, ncol = aff_cols.shape
    rb = min(L, COMBINE_ROWS)
    return pl.pallas_call(
        functools.partial(_topk_kernel, cap=cap, rb=rb),
        grid=(1,),
        in_specs=[pl.BlockSpec((L, ncol), lambda i: (0, 0)),
                  pl.BlockSpec((1, ncol), lambda i: (0, 0))],
        out_specs=[pl.BlockSpec((L, ncol), lambda i: (0, 0)),
                   pl.BlockSpec((L // rb, ncol), lambda i: (0, 0))],
        out_shape=[jax.ShapeDtypeStruct((L, ncol), I32),
                   jax.ShapeDtypeStruct((L // rb, ncol), I32)],
        compiler_params=_cparams(("arbitrary",)),
        name="topk_slots",
    )(aff_cols, offs)


def _moe_ffn_kernel(h_ref, slot_ref, gate_ref, wu_ref, wd_ref, yg_ref, wu_s, wd_s, *, n_slots):
    rows = h_ref.shape[1]

    @pl.when(pl.program_id(1) == 0)
    def _():
        wu_s[...] = wu_ref[0, 0].astype(BF16)
        wd_s[...] = wd_ref[0, 0].astype(BF16)

    hit = lax.broadcasted_iota(I32, (n_slots, rows), 0) == slot_ref[0, 0]
    onehot = hit.astype(F32).astype(BF16)
    gate_s = jnp.sum(jnp.where(hit, gate_ref[0, 0], 0.0), axis=1, keepdims=True)
    xg = _dot(onehot, h_ref[0]).astype(BF16)
    up = _dot(xg, wu_s[...])
    hid = (_silu(up[:, :D_EXPERT]) * up[:, D_EXPERT:]).astype(BF16)
    yg_ref[0, 0] = (_dot(hid, wd_s[...]) * gate_s).astype(BF16)


def _moe_ffn(h_g, slot_g, gate_g, w_up, w_down, layer, n_slots):
    ng, rows, D = h_g.shape
    route = pl.BlockSpec((1, 1, 1, rows), lambda e, g: (g, e, 0, 0))
    return pl.pallas_call(
        functools.partial(_moe_ffn_kernel, n_slots=n_slots),
        grid=(N_EXPERTS, ng),
        in_specs=[pl.BlockSpec((1, rows, D), lambda e, g: (g, 0, 0)),
                  route, route,
                  pl.BlockSpec((1, 1, D, 2 * D_EXPERT), lambda e, g: (layer, e, 0, 0)),
                  pl.BlockSpec((1, 1, D_EXPERT, D), lambda e, g: (layer, e, 0, 0))],
        out_specs=pl.BlockSpec((1, 1, n_slots, D), lambda e, g: (g, e, 0, 0)),
        out_shape=jax.ShapeDtypeStruct((ng, N_EXPERTS, n_slots, D), BF16),
        scratch_shapes=[pltpu.VMEM((D, 2 * D_EXPERT), BF16),
                        pltpu.VMEM((D_EXPERT, D), BF16)],
        compiler_params=_cparams(("arbitrary", "arbitrary")),
        name="moe_ffn",
    )(h_g, slot_g, gate_g, w_up, w_down)


def _moe_combine_kernel(win_ref, fits_ref, yg_ref, slot_ref, x_ref, g2_ref, gam_ref, o_ref, *, final):
    g = pl.program_id(0)
    c = pl.program_id(1)
    _, n_e, n_slots, D = yg_ref.shape
    chunk = x_ref.shape[1]
    slots = slot_ref[0]

    def finish(comb):
        xn = x_ref[0] + g2_ref[0] * comb
        o_ref[0] = _rms(xn, gam_ref[...]) if final else xn

    @pl.when(fits_ref[g, c] != 0)
    def _():
        lane = lax.broadcasted_iota(I32, (chunk, 2 * COMBINE_WIN), 1)
        pieces, windows = [], []
        for e in range(0, n_e, 2):
            rel = []
            for j in range(2):
                start = win_ref[g, c * n_e + e + j]
                r = slots[:, e + j:e + j + 1] - start
                inside = jnp.logical_and(r >= 0, r < COMBINE_WIN)
                rel.append(jnp.where(inside, r + j * COMBINE_WIN, -1))
                windows.append(yg_ref[0, e + j, pl.ds(pl.multiple_of(start, SLOT_ALIGN), COMBINE_WIN), :])
            hit = jnp.logical_or(lane == rel[0], lane == rel[1])
            pieces.append(hit.astype(F32).astype(BF16))
        finish(_dot(jnp.concatenate(pieces, axis=1), jnp.concatenate(windows, axis=0)))

    @pl.when(fits_ref[g, c] == 0)
    def _():
        lane = lax.broadcasted_iota(I32, (chunk, n_slots), 1)
        pieces = [(slots[:, e:e + 1] == lane).astype(F32).astype(BF16) for e in range(n_e)]
        finish(_dot(jnp.concatenate(pieces, axis=1), yg_ref[0].reshape(n_e * n_slots, D)))


def _moe_combine(win, fits, yg, slot_t, x_g, g2, gamma, L, final):
    ng, n_e, n_slots, D = yg.shape
    rows = x_g.shape[1]
    chunk = min(COMBINE_ROWS, L)
    row = pl.BlockSpec((1, chunk, D), lambda g, c, *_: (g, c, 0))
    return pl.pallas_call(
        functools.partial(_moe_combine_kernel, final=final),
        grid_spec=pltpu.PrefetchScalarGridSpec(
            num_scalar_prefetch=2,
            grid=(ng, rows // chunk),
            in_specs=[pl.BlockSpec((1, n_e, n_slots, D), lambda g, c, *_: (g, 0, 0, 0)),
                      pl.BlockSpec((1, chunk, LANES), lambda g, c, *_: (g, c, 0)),
                      row,
                      pl.BlockSpec((1, 1, D), lambda g, c, *_: ((g * rows + c * chunk) // L, 0, 0)),
                      pl.BlockSpec((1, D), lambda g, c, *_: (0, 0))],
            out_specs=row),
        out_shape=jax.ShapeDtypeStruct((ng, rows, D), F32),
        compiler_params=_cparams(("arbitrary", "arbitrary")),
        name="moe_combine_final" if final else "moe_combine",
    )(win, fits, yg, slot_t, x_g, g2, gamma)


def _combine_windows(first, ng, G, n_slots):
    nblk = first.shape[0]
    E = N_EXPERTS
    s = first.reshape(nblk, ng, G, E).transpose(1, 2, 0, 3).reshape(ng, G * nblk, E)
    s_next = jnp.concatenate([s[:, 1:], jnp.full((ng, 1, E), n_slots, I32)], axis=1)
    start = jnp.minimum((s // SLOT_ALIGN) * SLOT_ALIGN, n_slots - COMBINE_WIN)
    fits = jnp.all(s_next <= start + COMBINE_WIN, axis=2).astype(I32)
    return start.reshape(ng, G * nblk * E), fits


def _ec_moe(x, h, aff, w_up, w_down, layer, g2, gamma, final):
    B, L, D = h.shape
    E = N_EXPERTS
    cap = CAPACITY_FACTOR * L // E
    G = MOE_ROWS // L
    ng = B // G
    cols = aff[:, :, :E].transpose(1, 0, 2).reshape(L, B * E)
    offs = jnp.repeat((jnp.arange(B, dtype=I32) % G) * cap, E).reshape(1, B * E)
    slot, first = _topk_slots(cols, offs, cap)

    def by_lane(a):
        return a.reshape(L, ng, G, E).transpose(1, 3, 2, 0).reshape(ng, E, 1, G * L)

    yg = _moe_ffn(h.reshape(ng, G * L, D), by_lane(slot), by_lane(cols), w_up, w_down, layer, G * cap)
    slot_t = slot.reshape(L, ng, G, E).transpose(1, 2, 0, 3).reshape(ng, G * L, E)
    slot_t = jnp.pad(slot_t, ((0, 0), (0, 0), (0, LANES - E)), constant_values=-1)
    win, fits = _combine_windows(first, ng, G, G * cap)
    out = _moe_combine(win, fits, yg, slot_t, x.reshape(ng, G * L, D), g2, gamma, L, final)
    return out.reshape(B, L, D)


def _mla_in_kernel(*refs, need_q):
    (x_ref, gam_ref, sh_ref, sc_ref, win_ref, qn_ref, kvn_ref,
     wqa_ref, wqb_ref, wk_ref, wv_ref, vone_ref, cos_ref, sin_ref) = refs[:14]
    outs = refs[14:]
    h = _rms_mod(x_ref[0], gam_ref[...], sh_ref[0], sc_ref[0]).astype(BF16)
    z = _dot(h, win_ref[...])
    cos = cos_ref[...]
    sin = sin_ref[...]
    kv0 = Q_LORA + KV_LORA
    ckv = _rms(z[:, Q_LORA:kv0], kvn_ref[...]).astype(BF16)
    k_rope = z[:, kv0:kv0 + HEAD_PAD] * cos + z[:, kv0 + HEAD_PAD:kv0 + 2 * HEAD_PAD] * sin
    k_nope = _dot(ckv, wk_ref[...])
    if need_q:
        q_ref, k_ref, v_ref = outs
        cq = _rms(z[:, :Q_LORA], qn_ref[...]).astype(BF16)
        qa = _dot(cq, wqa_ref[...])
        qb = _dot(cq, wqb_ref[...])
    else:
        k_ref, v_ref = outs
    for hh in range(MLA_HEADS):
        sl = slice(hh * HEAD_PAD, (hh + 1) * HEAD_PAD)
        k_ref[0, :, sl] = (k_nope[:, sl] + k_rope).astype(BF16)
        if need_q:
            q_ref[0, :, sl] = (qa[:, sl] * cos + qb[:, sl] * sin).astype(BF16)
    v_ref[0] = (_dot(ckv, wv_ref[...]) + vone_ref[...]).astype(BF16)


def _mla_in(x, gamma, shift, scale, w, cos_t, sin_t, need_q, tm):
    B, L, D = x.shape
    row = pl.BlockSpec((1, tm, D), lambda b, i: (b, i, 0))
    mod = pl.BlockSpec((1, 1, D), lambda b, i: (b, 0, 0))

    def full(a):
        return pl.BlockSpec(a.shape, lambda b, i: (0,) * a.ndim)

    hp = MLA_HEADS * HEAD_PAD
    wide = pl.BlockSpec((1, tm, hp), lambda b, i: (b, i, 0))
    tab = pl.BlockSpec((tm, HEAD_PAD), lambda b, i: (i, 0))
    out_specs = [wide, wide]
    out_shape = [jax.ShapeDtypeStruct((B, L, hp), BF16)] * 2
    if need_q:
        out_specs = [wide] + out_specs
        out_shape = [jax.ShapeDtypeStruct((B, L, hp), BF16)] + out_shape
    weights = [w["win"], w["qn"], w["kvn"], w["wqa"], w["wqb"], w["wk"], w["wv"], w["vone"]]
    return pl.pallas_call(
        functools.partial(_mla_in_kernel, need_q=need_q),
        grid=(B, L // tm),
        in_specs=[row, full(gamma), mod, mod] + [full(a) for a in weights] + [tab, tab],
        out_specs=out_specs,
        out_shape=out_shape,
        compiler_params=_cparams(("arbitrary", "arbitrary")),
        name="mla_in_lat" if need_q else "mla_in_ctx",
    )(x, gamma, shift, scale, *weights, cos_t, sin_t)


def _attn_kernel(q_ref, kl_ref, kc_ref, vl_ref, vc_ref, o_ref):
    tq = q_ref.shape[1]
    blocks = [(j, r) for r in range(0, tq, ATTN_QB) for j in range(2)]

    def scores(j, r):
        sl = slice(j * HEAD_PAD, (j + 1) * HEAD_PAD)
        q = q_ref[0, r:r + ATTN_QB, sl]
        return _dot_nt(q, kl_ref[0, :, sl]), _dot_nt(q, kc_ref[0, :, sl])

    def weighted(j, s1, s2):
        sl = slice(j * HEAD_PAD, (j + 1) * HEAD_PAD)
        m = jnp.maximum(jnp.max(s1, axis=-1, keepdims=True), jnp.max(s2, axis=-1, keepdims=True))
        p1 = jnp.exp2((s1 - m) * EXP2_SCALE).astype(BF16)
        p2 = jnp.exp2((s2 - m) * EXP2_SCALE).astype(BF16)
        return _dot(p1, vl_ref[0, :, sl]) + _dot(p2, vc_ref[0, :, sl])

    pending = scores(*blocks[0])
    acc = {}
    for n, (j, r) in enumerate(blocks):
        s1, s2 = pending
        if n + 1 < len(blocks):
            pending = scores(*blocks[n + 1])
        acc[(j, r)] = weighted(j, s1, s2)
    lane = lax.broadcasted_iota(I32, (ATTN_QB, 2 * V_DIM), 1)
    for r in range(0, tq, ATTN_QB):
        even = acc[(0, r)]
        odd = acc[(1, r)]
        even = even / even[:, V_DIM:V_DIM + 1]
        odd = odd / odd[:, 0:1]
        o_ref[0, r:r + ATTN_QB, :] = jnp.where(lane < V_DIM, even, odd).astype(BF16)


def _attention(q, k_lat, k_ctx, v_lat, v_ctx, tq):
    B, L, _ = q.shape
    Lc = k_ctx.shape[1]
    pair = 2 * HEAD_PAD
    return pl.pallas_call(
        _attn_kernel,
        grid=(B, MLA_HEADS // 2, L // tq),
        in_specs=[pl.BlockSpec((1, tq, pair), lambda b, h, i: (b, i, h)),
                  pl.BlockSpec((1, L, pair), lambda b, h, i: (b, 0, h)),
                  pl.BlockSpec((1, Lc, pair), lambda b, h, i: (b, 0, h)),
                  pl.BlockSpec((1, L, pair), lambda b, h, i: (b, 0, h)),
                  pl.BlockSpec((1, Lc, pair), lambda b, h, i: (b, 0, h))],
        out_specs=pl.BlockSpec((1, tq, 2 * V_DIM), lambda b, h, i: (b, i, h)),
        out_shape=jax.ShapeDtypeStruct((B, L, MLA_HEADS * V_DIM), BF16),
        compiler_params=_cparams(("arbitrary", "arbitrary", "arbitrary")),
        name="attention",
    )(q, k_lat, k_ctx, v_lat, v_ctx)


def _rope_tables(rows):
    t = np.arange(rows * GRID_W)
    pos = np.stack([t // GRID_W, t % GRID_W], axis=-1).astype(np.float32)
    freq = jnp.asarray(ROPE_BASE, F32) ** (-jnp.arange(ROPE_FREQS, dtype=F32) / ROPE_FREQS)
    ang = jnp.asarray(pos)[:, :, None] * freq
    cos, sin = jnp.cos(ang), jnp.sin(ang)
    cos_r = jnp.stack([cos, cos], axis=2).reshape(-1, QK_ROPE)
    sin_r = jnp.stack([-sin, sin], axis=2).reshape(-1, QK_ROPE)
    T = cos_r.shape[0]
    pad = HEAD_PAD - QK_DIM
    cos_t = jnp.concatenate([jnp.ones((T, QK_NOPE), F32), cos_r, jnp.zeros((T, pad), F32)], axis=1)
    sin_t = jnp.concatenate([jnp.zeros((T, QK_NOPE), F32), sin_r, jnp.zeros((T, pad), F32)], axis=1)
    return cos_t, sin_t


def _rope_partner():
    p = np.arange(QK_ROPE)
    half = (p % (2 * ROPE_FREQS)) // ROPE_FREQS
    return np.where(half == 0, p + ROPE_FREQS, p - ROPE_FREQS)


def _mla_weights(w_in, q_norm, kv_norm, w_uq, w_ukv):
    partner = _rope_partner()
    pad = HEAD_PAD - QK_DIM
    kv0 = Q_LORA + KV_LORA
    kr = w_in[:, kv0:]
    zl = jnp.zeros((D_MODEL, QK_NOPE), F32)
    zr = jnp.zeros((D_MODEL, pad), F32)
    win = jnp.concatenate([w_in[:, :kv0], zl, kr, zr, zl, kr[:, partner], zr], axis=1)
    wq = w_uq.reshape(Q_LORA, MLA_HEADS, QK_DIM)
    zq = jnp.zeros((Q_LORA, MLA_HEADS, pad), F32)
    wqa = jnp.concatenate([wq, zq], axis=2)
    wqb = jnp.concatenate([jnp.zeros((Q_LORA, MLA_HEADS, QK_NOPE), F32),
                           wq[:, :, QK_NOPE:][:, :, partner], zq], axis=2)
    wkv = w_ukv.reshape(KV_LORA, MLA_HEADS, QK_NOPE + V_DIM)
    wk = jnp.concatenate([wkv[:, :, :QK_NOPE],
                          jnp.zeros((KV_LORA, MLA_HEADS, HEAD_PAD - QK_NOPE), F32)], axis=2)
    zv = jnp.zeros((KV_LORA, MLA_HEADS // 2, HEAD_PAD - V_DIM), F32)
    wv2 = wkv[:, :, QK_NOPE:].reshape(KV_LORA, MLA_HEADS // 2, 2, V_DIM)
    wv = jnp.stack([jnp.concatenate([wv2[:, :, 0], zv], axis=2),
                    jnp.concatenate([zv, wv2[:, :, 1]], axis=2)], axis=2)
    hp = MLA_HEADS * HEAD_PAD
    vone = np.zeros((MLA_HEADS // 2, 2, HEAD_PAD), np.float32)
    vone[:, 0, V_DIM] = 1.0
    vone[:, 1, 0] = 1.0
    return {"win": win.astype(BF16),
            "qn": q_norm.reshape(1, Q_LORA), "kvn": kv_norm.reshape(1, KV_LORA),
            "wqa": wqa.reshape(Q_LORA, hp).astype(BF16),
            "wqb": wqb.reshape(Q_LORA, hp).astype(BF16),
            "wk": wk.reshape(KV_LORA, hp).astype(BF16),
            "wv": wv.reshape(KV_LORA, hp).astype(BF16),
            "vone": jnp.asarray(vone.reshape(1, hp))}


def _pad_router(r):
    return jnp.pad(r, ((0, 0), (0, LANES - N_EXPERTS))).astype(BF16)


def kernel(x, c, ctx, c_ctx, ada_w, ada_b, norm_mix, norm_ffn, norm_final, lru_w_in, lru_conv_w, lru_conv_b, lru_gate_w, lru_gate_b, lru_lambda, lru_w_out, mla_w_in, mla_q_norm, mla_kv_norm, mla_w_uq, mla_w_ukv, mla_w_o, moe_router, moe_w_up, moe_w_down):
    B, T, D = x.shape
    Lc = ctx.shape[1]

    rows = ((B + 1 + 7) // 8) * 8
    cc = jnp.concatenate([c, c_ctx[None, :], jnp.zeros((rows - B - 1, D), F32)], axis=0)
    ada = _adaln(cc, ada_w, ada_b).reshape(DEPTH, rows, 6, D)

    def mods(i):
        lat = [ada[i, :B, k].reshape(B, 1, D) for k in range(6)]
        cx = [jnp.broadcast_to(ada[i, B, k].reshape(1, 1, D), (B, 1, D)) for k in range(6)]
        return lat, cx

    vec = lambda a: a.reshape(1, -1)

    (s1, sc1, g1, s2, sc2, g2), (s1c, sc1c, g1c, s2c, sc2c, g2c) = mods(0)
    w_in = lru_w_in[0].astype(BF16)
    cw, cb = 0.5 * lru_conv_w[0], vec(0.5 * lru_conv_b[0])
    g_lat, xc_lat = _lru_in(x, vec(norm_mix[0]), s1, sc1, w_in, cw, cb, tt=512)
    g_ctx, xc_ctx = _lru_in(ctx, vec(norm_mix[0]), s1c, sc1c, w_in, cw, cb, tt=Lc)
    ys = []
    for d in range(2):
        gw = lru_gate_w[0, d]
        wg = jnp.concatenate([gw[0], gw[1]], axis=-1).astype(BF16)
        ys.append(_lru_scan(xc_ctx, xc_lat, wg, 0.5 * lru_gate_b[0, d], vec(lru_lambda[0, d]),
                            reverse=(d == 1)))
    w_out = lru_w_out[0].astype(BF16)
    router = _pad_router(moe_router[0])
    gf = vec(norm_ffn[0])
    x_a, h_lat, aff_lat = _mix_out((ys[0][1], ys[1][1], g_lat), x, w_out, g1, gf, s2, sc2, router, tm=512)
    c_a, h_ctx, aff_ctx = _mix_out((ys[0][0], ys[1][0], g_ctx), ctx, w_out, g1c, gf, s2c, sc2c, router, tm=Lc)
    x_b = _ec_moe(x_a, h_lat, aff_lat, moe_w_up, moe_w_down, 0, g2, gf, final=False)
    c_b = _ec_moe(c_a, h_ctx, aff_ctx, moe_w_up, moe_w_down, 0, g2c, gf, final=False)

    (s1, sc1, g1, s2, sc2, g2), (s1c, sc1c, _, _, _, _) = mods(1)
    w = _mla_weights(mla_w_in[0], mla_q_norm[0], mla_kv_norm[0], mla_w_uq[0], mla_w_ukv[0])
    cos_t, sin_t = _rope_tables(T // GRID_W)
    one_t = jnp.concatenate([jnp.ones((Lc, QK_DIM), F32), jnp.zeros((Lc, HEAD_PAD - QK_DIM), F32)], axis=1)
    gm = vec(norm_mix[1])
    q, k_lat, v_lat = _mla_in(x_b, gm, s1, sc1, w, cos_t, sin_t, True, tm=512)
    k_ctx, v_ctx = _mla_in(c_b, gm, s1c, sc1c, w, one_t, jnp.zeros_like(one_t), False, tm=Lc)
    attn = _attention(q, k_lat, k_ctx, v_lat, v_ctx, tq=512)
    x_c, h_lat, aff_lat = _mix_out((attn,), x_b, mla_w_o[0].astype(BF16), g1, vec(norm_ffn[1]),
                                   s2, sc2, _pad_router(moe_router[1]), tm=512)
    return _ec_moe(x_c, h_lat, aff_lat, moe_w_up, moe_w_down, 1, g2, vec(norm_final), final=True)
```

```python
import functools
import math

import numpy as np
import jax
import jax.numpy as jnp
from jax import lax
from jax.experimental import pallas as pl
from jax.experimental.pallas import tpu as pltpu

F32 = jnp.float32
BF16 = jnp.bfloat16
I32 = jnp.int32

D_MODEL = 1024
DEPTH = 2
GRID_W = 64
D_RNN = D_MODEL
CONV_W = 4
LRU_BLOCKS = 8
LRU_BLOCK = D_RNN // LRU_BLOCKS
LRU_C = 8.0
MLA_HEADS = 16
QK_NOPE = 64
QK_ROPE = 32
QK_DIM = QK_NOPE + QK_ROPE
V_DIM = 64
Q_LORA = 384
KV_LORA = 256
ROPE_FREQS = QK_ROPE // 4
ROPE_BASE = 10000.0
ATTN_SCALE = QK_DIM ** -0.5
N_EXPERTS = 16
D_EXPERT = 1024
CAPACITY_FACTOR = 2
EPS = 1e-6

LANES = 128
HEAD_PAD = LANES
VMEM_LIMIT = 56 * 1024 * 1024
NEG_BIG = -1e30
HALO = 8
MOE_ROWS = 2048
SCAN_TT = 64
COMBINE_ROWS = 256
COMBINE_WIN = 64
SLOT_ALIGN = 16
ATTN_QB = 256
EXP2_SCALE = ATTN_SCALE * math.log2(math.e)


def _cparams(sem):
    return pltpu.CompilerParams(dimension_semantics=sem, vmem_limit_bytes=VMEM_LIMIT)


def _sigmoid(x):
    return 0.5 * (jnp.tanh(0.5 * x) + 1.0)


def _silu(x):
    return x * _sigmoid(x)


def _gelu_tanh(x):
    return 0.5 * x * (1.0 + jnp.tanh(0.7978845608028654 * (x + 0.044715 * (x * x * x))))


def _rms(x, gamma):
    return x * lax.rsqrt(jnp.mean(x * x, axis=-1, keepdims=True) + EPS) * gamma


def _rms_mod(x, gamma, shift, scale):
    return _rms(x, gamma) * (1.0 + scale) + shift


def _dot(a, b):
    return jnp.dot(a, b, preferred_element_type=F32)


def _dot_nt(a, b):
    return lax.dot_general(a, b, (((1,), (1,)), ((), ())), preferred_element_type=F32)


def _adaln_kernel(c_ref, w_ref, b_ref, o_ref):
    s = _silu(c_ref[...]).astype(BF16)
    o_ref[0] = _dot(s, w_ref[0].astype(BF16)) + b_ref[0]


def _adaln(cc, ada_w, ada_b):
    rows = cc.shape[0]
    tn = 1024
    return pl.pallas_call(
        _adaln_kernel,
        grid=(DEPTH, 6 * D_MODEL // tn),
        in_specs=[pl.BlockSpec((rows, D_MODEL), lambda l, j: (0, 0)),
                  pl.BlockSpec((1, D_MODEL, tn), lambda l, j: (l, 0, j)),
                  pl.BlockSpec((1, 1, tn), lambda l, j: (l, 0, j))],
        out_specs=pl.BlockSpec((1, rows, tn), lambda l, j: (l, 0, j)),
        out_shape=jax.ShapeDtypeStruct((DEPTH, rows, 6 * D_MODEL), F32),
        compiler_params=_cparams(("arbitrary", "arbitrary")),
        name="adaln",
    )(cc, ada_w, ada_b.reshape(DEPTH, 1, 6 * D_MODEL))


def _lru_in_kernel(xp_ref, x_ref, xn_ref, gam_ref, sh_ref, sc_ref, w_ref, cw_ref, cb_ref,
                   g_ref, xc_ref, *, tt):
    i = pl.program_id(1)
    n = pl.num_programs(1)
    xe = jnp.concatenate([xp_ref[0], x_ref[0], xn_ref[0]], axis=0)
    he = _rms_mod(xe, gam_ref[...], sh_ref[0], sc_ref[0]).astype(BF16)
    z = _dot(he, w_ref[...])
    g_ref[0] = _gelu_tanh(z[HALO:HALO + tt, :D_RNN]).astype(BF16)
    z2 = z[:, D_RNN:]
    row = lax.broadcasted_iota(I32, (tt + 2 * HALO, 1), 0)
    valid = jnp.logical_and(jnp.logical_or(row >= HALO, i > 0),
                            jnp.logical_or(row < tt + HALO, i < n - 1))
    z2 = jnp.where(valid, z2, 0.0)
    cw = cw_ref[...]
    base = HALO - CONV_W // 2
    acc = z2[base:base + tt] * cw[0:1]
    for k in range(1, CONV_W):
        acc = acc + z2[base + k:base + k + tt] * cw[k:k + 1]
    xc_ref[0] = (acc + cb_ref[...]).astype(BF16)


def _lru_in(x, gamma, shift, scale, w_in, conv_w, conv_b, tt):
    B, L, D = x.shape
    nt = L // tt
    hb = tt // HALO
    last = L // HALO - 1
    vec = pl.BlockSpec((1, D), lambda b, i: (0, 0))
    mod = pl.BlockSpec((1, 1, D), lambda b, i: (b, 0, 0))
    out = pl.BlockSpec((1, tt, D_RNN), lambda b, i: (b, i, 0))
    return pl.pallas_call(
        functools.partial(_lru_in_kernel, tt=tt),
        grid=(B, nt),
        in_specs=[pl.BlockSpec((1, HALO, D), lambda b, i: (b, jnp.maximum(i * hb - 1, 0), 0)),
                  pl.BlockSpec((1, tt, D), lambda b, i: (b, i, 0)),
                  pl.BlockSpec((1, HALO, D), lambda b, i: (b, jnp.minimum((i + 1) * hb, last), 0)),
                  vec, mod, mod,
                  pl.BlockSpec((D, 2 * D_RNN), lambda b, i: (0, 0)),
                  pl.BlockSpec((CONV_W, D_RNN), lambda b, i: (0, 0)),
                  pl.BlockSpec((1, D_RNN), lambda b, i: (0, 0))],
        out_specs=[out, out],
        out_shape=[jax.ShapeDtypeStruct((B, L, D_RNN), BF16)] * 2,
        compiler_params=_cparams(("arbitrary", "arbitrary")),
        name="lru_in",
    )(x, x, x, gamma, shift, scale, w_in, conv_w, conv_b)


def _scan_kernel(xc_ctx_ref, xc_lat_ref, wg_ref, bg_ref, lam_ref, y_ctx_ref, y_lat_ref,
                 a_s, u_s, y_s, h_s, *, reverse, n_ctx):
    i = pl.program_id(0)
    nb, tt, _ = a_s.shape

    @pl.when(i == 0)
    def _():
        h_s[...] = jnp.zeros_like(h_s)

    def tile(x_ref, y_ref):
        x2 = x_ref[...].reshape(nb * tt, D_RNN)
        nlam = -lam_ref[...]
        sp = jnp.maximum(nlam, 0.0) + jnp.log1p(jnp.exp(-jnp.abs(nlam)))
        k2 = sp * (-0.5 * LRU_C * math.log2(math.e))
        for n in range(LRU_BLOCKS):
            sl = slice(n * LRU_BLOCK, (n + 1) * LRU_BLOCK)
            xb = x2[:, sl]
            g = _dot(xb, wg_ref[n])
            tr = jnp.tanh(g[:, :LRU_BLOCK] + bg_ref[0:1, sl])
            ti = jnp.tanh(g[:, LRU_BLOCK:] + bg_ref[1:2, sl])
            a = jnp.exp2(k2[:, sl] * tr + k2[:, sl])
            v = 1.0 - a * a
            root = jnp.where(v > 0.0, v * lax.rsqrt(v), 0.0)
            u = root * ((ti + 1.0) * xb.astype(F32))
            a_s[:, :, sl] = a.reshape(nb, tt, LRU_BLOCK)
            u_s[:, :, sl] = u.reshape(nb, tt, LRU_BLOCK)
        h = h_s[...]
        steps = range(tt - 1, -1, -1) if reverse else range(tt)
        for t in steps:
            h = a_s[:, t, :] * h + u_s[:, t, :]
            y_s[:, t, :] = h
        h_s[...] = h
        y_ref[...] = y_s[...].astype(BF16)

    @pl.when(i < n_ctx)
    def _():
        tile(xc_ctx_ref, y_ctx_ref)

    @pl.when(i >= n_ctx)
    def _():
        tile(xc_lat_ref, y_lat_ref)


def _lru_scan(xc_ctx, xc_lat, wg, bg, lam, reverse):
    B, Lc, _ = xc_ctx.shape
    L = xc_lat.shape[1]
    tt = SCAN_TT
    n_ctx = Lc // tt
    n_lat = L // tt

    def ctx_map(i):
        j = jnp.minimum(i, n_ctx - 1)
        return (0, (n_ctx - 1 - j) if reverse else j, 0)

    def lat_map(i):
        j = jnp.maximum(i - n_ctx, 0)
        return (0, (n_lat - 1 - j) if reverse else j, 0)

    return pl.pallas_call(
        functools.partial(_scan_kernel, reverse=reverse, n_ctx=n_ctx),
        grid=(n_ctx + n_lat,),
        in_specs=[pl.BlockSpec((B, tt, D_RNN), ctx_map),
                  pl.BlockSpec((B, tt, D_RNN), lat_map),
                  pl.BlockSpec((LRU_BLOCKS, LRU_BLOCK, 2 * LRU_BLOCK), lambda i: (0, 0, 0)),
                  pl.BlockSpec((2, D_RNN), lambda i: (0, 0)),
                  pl.BlockSpec((1, D_RNN), lambda i: (0, 0))],
        out_specs=[pl.BlockSpec((B, tt, D_RNN), ctx_map),
                   pl.BlockSpec((B, tt, D_RNN), lat_map)],
        out_shape=[jax.ShapeDtypeStruct((B, Lc, D_RNN), BF16),
                   jax.ShapeDtypeStruct((B, L, D_RNN), BF16)],
        scratch_shapes=[pltpu.VMEM((B, tt, D_RNN), F32),
                        pltpu.VMEM((B, tt, D_RNN), F32),
                        pltpu.VMEM((B, tt, D_RNN), F32),
                        pltpu.VMEM((B, D_RNN), F32)],
        compiler_params=_cparams(("arbitrary",)),
        name="lru_scan_rev" if reverse else "lru_scan_fwd",
    )(xc_ctx, xc_lat, wg, bg, lam)


def _mix_out_kernel(*refs, lru):
    if lru:
        (yf_ref, yb_ref, g_ref, x_ref, w_ref, g1_ref, gam_ref, sh_ref, sc_ref, r_ref,
         xo_ref, h_ref, aff_ref) = refs
        y = yf_ref[0].astype(F32) + yb_ref[0].astype(F32)
        lhs = (y * g_ref[0].astype(F32)).astype(BF16)
    else:
        (a_ref, x_ref, w_ref, g1_ref, gam_ref, sh_ref, sc_ref, r_ref,
         xo_ref, h_ref, aff_ref) = refs
        lhs = a_ref[0]
    xn = x_ref[0] + g1_ref[0] * _dot(lhs, w_ref[...])
    xo_ref[0] = xn
    hf = _rms_mod(xn, gam_ref[...], sh_ref[0], sc_ref[0])
    for s in range(D_MODEL // LANES):
        h_ref[0, :, s, :] = hf[:, s * LANES:(s + 1) * LANES]
    logits = _dot(hf.astype(BF16), r_ref[...])
    lane = lax.broadcasted_iota(I32, logits.shape, 1)
    logits = jnp.where(lane < N_EXPERTS, logits, NEG_BIG)
    p = jnp.exp(logits - jnp.max(logits, axis=-1, keepdims=True))
    aff_ref[0] = p / jnp.sum(p, axis=-1, keepdims=True)


def _mix_out(acts, x, w_out, g1, gamma, shift, scale, router_pad, tm):
    B, L, D = x.shape
    row = pl.BlockSpec((1, tm, D), lambda b, i: (b, i, 0))
    vec = pl.BlockSpec((1, D), lambda b, i: (0, 0))
    mod = pl.BlockSpec((1, 1, D), lambda b, i: (b, 0, 0))
    return pl.pallas_call(
        functools.partial(_mix_out_kernel, lru=len(acts) == 3),
        grid=(B, L // tm),
        in_specs=[row] * len(acts) + [
            row,
            pl.BlockSpec((D, D), lambda b, i: (0, 0)),
            mod, vec, mod, mod,
            pl.BlockSpec((D, LANES), lambda b, i: (0, 0))],
        out_specs=[row, pl.BlockSpec((1, tm, D // LANES, LANES), lambda b, i: (b, i, 0, 0)),
                   pl.BlockSpec((1, tm, LANES), lambda b, i: (b, i, 0))],
        out_shape=[jax.ShapeDtypeStruct((B, L, D), F32),
                   jax.ShapeDtypeStruct((B, L, D // LANES, LANES), F32),
                   jax.ShapeDtypeStruct((B, L, LANES), F32)],
        compiler_params=_cparams(("arbitrary", "arbitrary")),
        name="mix_out_lru" if len(acts) == 3 else "mix_out_attn",
    )(*acts, x, w_out, g1, gamma, shift, scale, router_pad)


def _topk_kernel(a_ref, offs_ref, slot_ref, first_ref, idx_ref, cum_s, *, cap, rb):
    L, ncol = a_ref.shape
    a = a_ref[...]

    def as_f32(bits):
        return lax.bitcast_convert_type(bits, F32)

    def search(k, thr):
        cand = thr | lax.shift_left(jnp.int32(1), 30 - k)
        cnt = jnp.sum((a >= as_f32(cand)).astype(I32), axis=0, keepdims=True)
        return jnp.where(cnt >= cap, cand, thr)

    thr = lax.fori_loop(0, 31, search, jnp.zeros((1, ncol), I32))
    lo = as_f32(thr)
    hi = as_f32(thr + 1)
    need = (cap - jnp.sum((a >= hi).astype(I32), axis=0, keepdims=True)).astype(F32)
    eq_b = jnp.logical_and(a >= lo, a < hi).astype(F32).astype(BF16)
    col = lax.broadcasted_iota(I32, (rb, L), 1)
    rowi = lax.broadcasted_iota(I32, (rb, L), 0)

    def before(r0):
        return (col < rowi + r0).astype(F32).astype(BF16)

    sel_blocks = []
    for r0 in range(0, L, rb):
        blk = a[r0:r0 + rb]
        eq_rank = _dot(before(r0), eq_b)
        take = jnp.logical_or(blk >= hi, jnp.logical_and(blk >= lo, eq_rank < need))
        sel_blocks.append(take.astype(F32))
    sel = jnp.concatenate(sel_blocks, axis=0) if len(sel_blocks) > 1 else sel_blocks[0]
    sel_b = sel.astype(BF16)
    offs = offs_ref[...]
    for k, r0 in enumerate(range(0, L, rb)):
        ahead = _dot(before(r0), sel_b)
        pos = ahead.astype(I32) + offs
        slot_ref[r0:r0 + rb, :] = jnp.where(sel_blocks[k] > 0.0, pos, -1)
        first_ref[k:k + 1, :] = pos[0:1, :]
        cum_s[r0:r0 + rb, :] = ahead + sel_blocks[k]

    def rows_of_slots(i, carry):
        base = i * 8
        cum = cum_s[...]
        out = [jnp.sum((cum <= jnp.asarray(base + j, F32)).astype(I32), axis=0, keepdims=True)
               for j in range(8)]
        idx_ref[pl.ds(pl.multiple_of(base, 8), 8), :] = jnp.concatenate(out, axis=0)
        return carry

    lax.fori_loop(0, cap // 8, rows_of_slots, 0)


def _topk_slots(aff_cols, offs, cap):
    L, ncol = aff_cols.shape
    rb = min(L, COMBINE_ROWS)
    return pl.pallas_call(
        functools.partial(_topk_kernel, cap=cap, rb=rb),
        grid=(1,),
        in_specs=[pl.BlockSpec((L, ncol), lambda i: (0, 0)),
                  pl.BlockSpec((1, ncol), lambda i: (0, 0))],
        out_specs=[pl.BlockSpec((L, ncol), lambda i: (0, 0)),
                   pl.BlockSpec((L // rb, ncol), lambda i: (0, 0)),
                   pl.BlockSpec((cap, ncol), lambda i: (0, 0))],
        out_shape=[jax.ShapeDtypeStruct((L, ncol), I32),
                   jax.ShapeDtypeStruct((L // rb, ncol), I32),
                   jax.ShapeDtypeStruct((cap, ncol), I32)],
        scratch_shapes=[pltpu.VMEM((L, ncol), F32)],
        compiler_params=_cparams(("arbitrary",)),
        name="topk_slots",
    )(aff_cols, offs)


def _moe_ffn_kernel(idx_ref, idx_next_ref, h_hbm, slot_ref, gate_ref, wu_ref, wd_ref, yg_ref,
                    wu_s, wd_s, rows_a, rows_b, sems, *, n_slots):
    n_groups = pl.num_programs(1)
    step = pl.program_id(0) * n_groups + pl.program_id(1)
    last = pl.num_programs(0) * n_groups - 1
    rows = slot_ref.shape[3]

    def gather(ids_ref, buf, sem):
        for s in range(n_slots):
            pltpu.make_async_copy(h_hbm.at[ids_ref[0, 0, 0, s]], buf.at[s], sem).start()

    def gathered(buf, sem):
        pltpu.make_async_copy(h_hbm.at[pl.ds(0, n_slots)], buf, sem).wait()

    @pl.when(step == 0)
    def _():
        gather(idx_ref, rows_a, sems.at[0])

    @pl.when(pl.program_id(1) == 0)
    def _():
        wu_s[...] = wu_ref[0, 0].astype(BF16)
        wd_s[...] = wd_ref[0, 0].astype(BF16)

    def ffn(cur, cur_sem, nxt, nxt_sem):
        gather(idx_next_ref, nxt, nxt_sem)
        hit = lax.broadcasted_iota(I32, (n_slots, rows), 0) == slot_ref[0, 0]
        gate_s = jnp.sum(jnp.where(hit, gate_ref[0, 0], 0.0), axis=1, keepdims=True)
        gathered(cur, cur_sem)
        xg = jnp.concatenate([cur[:, s, :] for s in range(D_MODEL // LANES)], axis=1).astype(BF16)
        up = _dot(xg, wu_s[...])
        hid = (_silu(up[:, :D_EXPERT]) * up[:, D_EXPERT:]).astype(BF16)
        yg_ref[0, 0] = (_dot(hid, wd_s[...]) * gate_s).astype(BF16)

        @pl.when(step == last)
        def _():
            gathered(nxt, nxt_sem)

    @pl.when(step % 2 == 0)
    def _():
        ffn(rows_a, sems.at[0], rows_b, sems.at[1])

    @pl.when(step % 2 == 1)
    def _():
        ffn(rows_b, sems.at[1], rows_a, sems.at[0])


def _moe_ffn(idx_g, h_rows, slot_g, gate_g, w_up, w_down, layer, n_slots):
    ng, _, _, rows = slot_g.shape
    D = D_MODEL
    n_steps = N_EXPERTS * ng

    def next_ids(e, g):
        n = jnp.minimum(e * ng + g + 1, n_steps - 1)
        return (n % ng, n // ng, 0, 0)

    route = pl.BlockSpec((1, 1, 1, rows), lambda e, g: (g, e, 0, 0))
    slab = (D // LANES, LANES)
    return pl.pallas_call(
        functools.partial(_moe_ffn_kernel, n_slots=n_slots),
        grid=(N_EXPERTS, ng),
        in_specs=[pl.BlockSpec((1, 1, 1, n_slots), lambda e, g: (g, e, 0, 0), memory_space=pltpu.SMEM),
                  pl.BlockSpec((1, 1, 1, n_slots), next_ids, memory_space=pltpu.SMEM),
                  pl.BlockSpec(memory_space=pl.ANY),
                  route, route,
                  pl.BlockSpec((1, 1, D, 2 * D_EXPERT), lambda e, g: (layer, e, 0, 0)),
                  pl.BlockSpec((1, 1, D_EXPERT, D), lambda e, g: (layer, e, 0, 0))],
        out_specs=pl.BlockSpec((1, 1, n_slots, D), lambda e, g: (g, e, 0, 0)),
        out_shape=jax.ShapeDtypeStruct((ng, N_EXPERTS, n_slots, D), BF16),
        scratch_shapes=[pltpu.VMEM((D, 2 * D_EXPERT), BF16),
                        pltpu.VMEM((D_EXPERT, D), BF16),
                        pltpu.VMEM((n_slots,) + slab, F32),
                        pltpu.VMEM((n_slots,) + slab, F32),
                        pltpu.SemaphoreType.DMA((2,))],
        compiler_params=_cparams(("arbitrary", "arbitrary")),
        name="moe_ffn",
    )(idx_g, idx_g, h_rows, slot_g, gate_g, w_up, w_down)


def _moe_combine_kernel(win_ref, fits_ref, yg_ref, slot_ref, x_ref, g2_ref, gam_ref, o_ref, *, final):
    g = pl.program_id(0)
    c = pl.program_id(1)
    _, n_e, n_slots, D = yg_ref.shape
    chunk = x_ref.shape[1]
    slots = slot_ref[0]

    def finish(comb):
        xn = x_ref[0] + g2_ref[0] * comb
        o_ref[0] = _rms(xn, gam_ref[...]) if final else xn

    @pl.when(fits_ref[g, c] != 0)
    def _():
        lane = lax.broadcasted_iota(I32, (chunk, 2 * COMBINE_WIN), 1)
        pieces, windows = [], []
        for e in range(0, n_e, 2):
            rel = []
            for j in range(2):
                start = win_ref[g, c * n_e + e + j]
                r = slots[:, e + j:e + j + 1] - start
                inside = jnp.logical_and(r >= 0, r < COMBINE_WIN)
                rel.append(jnp.where(inside, r + j * COMBINE_WIN, -1))
                windows.append(yg_ref[0, e + j, pl.ds(pl.multiple_of(start, SLOT_ALIGN), COMBINE_WIN), :])
            hit = jnp.logical_or(lane == rel[0], lane == rel[1])
            pieces.append(hit.astype(F32).astype(BF16))
        finish(_dot(jnp.concatenate(pieces, axis=1), jnp.concatenate(windows, axis=0)))

    @pl.when(fits_ref[g, c] == 0)
    def _():
        lane = lax.broadcasted_iota(I32, (chunk, n_slots), 1)
        pieces = [(slots[:, e:e + 1] == lane).astype(F32).astype(BF16) for e in range(n_e)]
        finish(_dot(jnp.concatenate(pieces, axis=1), yg_ref[0].reshape(n_e * n_slots, D)))


def _moe_combine(win, fits, yg, slot_t, x_g, g2, gamma, L, final):
    ng, n_e, n_slots, D = yg.shape
    rows = x_g.shape[1]
    chunk = min(COMBINE_ROWS, L)
    row = pl.BlockSpec((1, chunk, D), lambda g, c, *_: (g, c, 0))
    return pl.pallas_call(
        functools.partial(_moe_combine_kernel, final=final),
        grid_spec=pltpu.PrefetchScalarGridSpec(
            num_scalar_prefetch=2,
            grid=(ng, rows // chunk),
            in_specs=[pl.BlockSpec((1, n_e, n_slots, D), lambda g, c, *_: (g, 0, 0, 0)),
                      pl.BlockSpec((1, chunk, LANES), lambda g, c, *_: (g, c, 0)),
                      row,
                      pl.BlockSpec((1, 1, D), lambda g, c, *_: ((g * rows + c * chunk) // L, 0, 0)),
                      pl.BlockSpec((1, D), lambda g, c, *_: (0, 0))],
            out_specs=row),
        out_shape=jax.ShapeDtypeStruct((ng, rows, D), F32),
        compiler_params=_cparams(("arbitrary", "arbitrary")),
        name="moe_combine_final" if final else "moe_combine",
    )(win, fits, yg, slot_t, x_g, g2, gamma)


def _combine_windows(first, ng, G, n_slots):
    nblk = first.shape[0]
    E = N_EXPERTS
    s = first.reshape(nblk, ng, G, E).transpose(1, 2, 0, 3).reshape(ng, G * nblk, E)
    s_next = jnp.concatenate([s[:, 1:], jnp.full((ng, 1, E), n_slots, I32)], axis=1)
    start = jnp.minimum((s // SLOT_ALIGN) * SLOT_ALIGN, n_slots - COMBINE_WIN)
    fits = jnp.all(s_next <= start + COMBINE_WIN, axis=2).astype(I32)
    return start.reshape(ng, G * nblk * E), fits


def _ec_moe(x, h, aff, w_up, w_down, layer, g2, gamma, final):
    B, L, D = x.shape
    E = N_EXPERTS
    cap = CAPACITY_FACTOR * L // E
    G = MOE_ROWS // L
    ng = B // G
    cols = aff[:, :, :E].transpose(1, 0, 2).reshape(L, B * E)
    offs = jnp.repeat((jnp.arange(B, dtype=I32) % G) * cap, E).reshape(1, B * E)
    slot, first, idx = _topk_slots(cols, offs, cap)

    def by_lane(a):
        return a.reshape(L, ng, G, E).transpose(1, 3, 2, 0).reshape(ng, E, 1, G * L)

    idx_g = idx.reshape(cap, ng, G, E).transpose(1, 3, 2, 0) + (jnp.arange(B, dtype=I32) * L).reshape(ng, 1, G, 1)
    yg = _moe_ffn(idx_g.reshape(ng, E, 1, G * cap), h.reshape(B * L, D // LANES, LANES),
                  by_lane(slot), by_lane(cols), w_up, w_down, layer, G * cap)
    slot_t = slot.reshape(L, ng, G, E).transpose(1, 2, 0, 3).reshape(ng, G * L, E)
    slot_t = jnp.pad(slot_t, ((0, 0), (0, 0), (0, LANES - E)), constant_values=-1)
    win, fits = _combine_windows(first, ng, G, G * cap)
    out = _moe_combine(win, fits, yg, slot_t, x.reshape(ng, G * L, D), g2, gamma, L, final)
    return out.reshape(B, L, D)


def _mla_in_kernel(*refs, need_q):
    (x_ref, gam_ref, sh_ref, sc_ref, win_ref, qn_ref, kvn_ref,
     wqa_ref, wqb_ref, wk_ref, wv_ref, vone_ref, cos_ref, sin_ref) = refs[:14]
    outs = refs[14:]
    h = _rms_mod(x_ref[0], gam_ref[...], sh_ref[0], sc_ref[0]).astype(BF16)
    z = _dot(h, win_ref[...])
    cos = cos_ref[...]
    sin = sin_ref[...]
    kv0 = Q_LORA + KV_LORA
    ckv = _rms(z[:, Q_LORA:kv0], kvn_ref[...]).astype(BF16)
    k_rope = z[:, kv0:kv0 + HEAD_PAD] * cos + z[:, kv0 + HEAD_PAD:kv0 + 2 * HEAD_PAD] * sin
    k_nope = _dot(ckv, wk_ref[...])
    if need_q:
        q_ref, k_ref, v_ref = outs
        cq = _rms(z[:, :Q_LORA], qn_ref[...]).astype(BF16)
        qa = _dot(cq, wqa_ref[...])
        qb = _dot(cq, wqb_ref[...])
    else:
        k_ref, v_ref = outs
    for hh in range(MLA_HEADS):
        sl = slice(hh * HEAD_PAD, (hh + 1) * HEAD_PAD)
        k_ref[0, :, sl] = (k_nope[:, sl] + k_rope).astype(BF16)
        if need_q:
            q_ref[0, :, sl] = (qa[:, sl] * cos + qb[:, sl] * sin).astype(BF16)
    v_ref[0] = (_dot(ckv, wv_ref[...]) + vone_ref[...]).astype(BF16)


def _mla_in(x, gamma, shift, scale, w, cos_t, sin_t, need_q, tm):
    B, L, D = x.shape
    row = pl.BlockSpec((1, tm, D), lambda b, i: (b, i, 0))
    mod = pl.BlockSpec((1, 1, D), lambda b, i: (b, 0, 0))

    def full(a):
        return pl.BlockSpec(a.shape, lambda b, i: (0,) * a.ndim)

    hp = MLA_HEADS * HEAD_PAD
    wide = pl.BlockSpec((1, tm, hp), lambda b, i: (b, i, 0))
    tab = pl.BlockSpec((tm, HEAD_PAD), lambda b, i: (i, 0))
    out_specs = [wide, wide]
    out_shape = [jax.ShapeDtypeStruct((B, L, hp), BF16)] * 2
    if need_q:
        out_specs = [wide] + out_specs
        out_shape = [jax.ShapeDtypeStruct((B, L, hp), BF16)] + out_shape
    weights = [w["win"], w["qn"], w["kvn"], w["wqa"], w["wqb"], w["wk"], w["wv"], w["vone"]]
    return pl.pallas_call(
        functools.partial(_mla_in_kernel, need_q=need_q),
        grid=(B, L // tm),
        in_specs=[row, full(gamma), mod, mod] + [full(a) for a in weights] + [tab, tab],
        out_specs=out_specs,
        out_shape=out_shape,
        compiler_params=_cparams(("arbitrary", "arbitrary")),
        name="mla_in_lat" if need_q else "mla_in_ctx",
    )(x, gamma, shift, scale, *weights, cos_t, sin_t)


def _attn_kernel(q_ref, kl_ref, kc_ref, vl_ref, vc_ref, o_ref):
    tq = q_ref.shape[1]
    blocks = [(j, r) for r in range(0, tq, ATTN_QB) for j in range(2)]

    def scores(j, r):
        sl = slice(j * HEAD_PAD, (j + 1) * HEAD_PAD)
        q = q_ref[0, r:r + ATTN_QB, sl]
        return _dot_nt(q, kl_ref[0, :, sl]), _dot_nt(q, kc_ref[0, :, sl])

    def weighted(j, s1, s2):
        sl = slice(j * HEAD_PAD, (j + 1) * HEAD_PAD)
        m = jnp.maximum(jnp.max(s1, axis=-1, keepdims=True), jnp.max(s2, axis=-1, keepdims=True))
        p1 = jnp.exp2((s1 - m) * EXP2_SCALE).astype(BF16)
        p2 = jnp.exp2((s2 - m) * EXP2_SCALE).astype(BF16)
        return _dot(p1, vl_ref[0, :, sl]) + _dot(p2, vc_ref[0, :, sl])

    pending = scores(*blocks[0])
    acc = {}
    for n, (j, r) in enumerate(blocks):
        s1, s2 = pending
        if n + 1 < len(blocks):
            pending = scores(*blocks[n + 1])
        acc[(j, r)] = weighted(j, s1, s2)
    lane = lax.broadcasted_iota(I32, (ATTN_QB, 2 * V_DIM), 1)
    for r in range(0, tq, ATTN_QB):
        even = acc[(0, r)]
        odd = acc[(1, r)]
        even = even / even[:, V_DIM:V_DIM + 1]
        odd = odd / odd[:, 0:1]
        o_ref[0, r:r + ATTN_QB, :] = jnp.where(lane < V_DIM, even, odd).astype(BF16)


def _attention(q, k_lat, k_ctx, v_lat, v_ctx, tq):
    B, L, _ = q.shape
    Lc = k_ctx.shape[1]
    pair = 2 * HEAD_PAD
    return pl.pallas_call(
        _attn_kernel,
        grid=(B, MLA_HEADS // 2, L // tq),
        in_specs=[pl.BlockSpec((1, tq, pair), lambda b, h, i: (b, i, h)),
                  pl.BlockSpec((1, L, pair), lambda b, h, i: (b, 0, h)),
                  pl.BlockSpec((1, Lc, pair), lambda b, h, i: (b, 0, h)),
                  pl.BlockSpec((1, L, pair), lambda b, h, i: (b, 0, h)),
                  pl.BlockSpec((1, Lc, pair), lambda b, h, i: (b, 0, h))],
        out_specs=pl.BlockSpec((1, tq, 2 * V_DIM), lambda b, h, i: (b, i, h)),
        out_shape=jax.ShapeDtypeStruct((B, L, MLA_HEADS * V_DIM), BF16),
        compiler_params=_cparams(("arbitrary", "arbitrary", "arbitrary")),
        name="attention",
    )(q, k_lat, k_ctx, v_lat, v_ctx)


def _rope_tables(rows):
    t = np.arange(rows * GRID_W)
    pos = np.stack([t // GRID_W, t % GRID_W], axis=-1).astype(np.float32)
    freq = jnp.asarray(ROPE_BASE, F32) ** (-jnp.arange(ROPE_FREQS, dtype=F32) / ROPE_FREQS)
    ang = jnp.asarray(pos)[:, :, None] * freq
    cos, sin = jnp.cos(ang), jnp.sin(ang)
    cos_r = jnp.stack([cos, cos], axis=2).reshape(-1, QK_ROPE)
    sin_r = jnp.stack([-sin, sin], axis=2).reshape(-1, QK_ROPE)
    T = cos_r.shape[0]
    pad = HEAD_PAD - QK_DIM
    cos_t = jnp.concatenate([jnp.ones((T, QK_NOPE), F32), cos_r, jnp.zeros((T, pad), F32)], axis=1)
    sin_t = jnp.concatenate([jnp.zeros((T, QK_NOPE), F32), sin_r, jnp.zeros((T, pad), F32)], axis=1)
    return cos_t, sin_t


def _rope_partner():
    p = np.arange(QK_ROPE)
    half = (p % (2 * ROPE_FREQS)) // ROPE_FREQS
    return np.where(half == 0, p + ROPE_FREQS, p - ROPE_FREQS)


def _mla_weights(w_in, q_norm, kv_norm, w_uq, w_ukv):
    partner = _rope_partner()
    pad = HEAD_PAD - QK_DIM
    kv0 = Q_LORA + KV_LORA
    kr = w_in[:, kv0:]
    zl = jnp.zeros((D_MODEL, QK_NOPE), F32)
    zr = jnp.zeros((D_MODEL, pad), F32)
    win = jnp.concatenate([w_in[:, :kv0], zl, kr, zr, zl, kr[:, partner], zr], axis=1)
    wq = w_uq.reshape(Q_LORA, MLA_HEADS, QK_DIM)
    zq = jnp.zeros((Q_LORA, MLA_HEADS, pad), F32)
    wqa = jnp.concatenate([wq, zq], axis=2)
    wqb = jnp.concatenate([jnp.zeros((Q_LORA, MLA_HEADS, QK_NOPE), F32),
                           wq[:, :, QK_NOPE:][:, :, partner], zq], axis=2)
    wkv = w_ukv.reshape(KV_LORA, MLA_HEADS, QK_NOPE + V_DIM)
    wk = jnp.concatenate([wkv[:, :, :QK_NOPE],
                          jnp.zeros((KV_LORA, MLA_HEADS, HEAD_PAD - QK_NOPE), F32)], axis=2)
    zv = jnp.zeros((KV_LORA, MLA_HEADS // 2, HEAD_PAD - V_DIM), F32)
    wv2 = wkv[:, :, QK_NOPE:].reshape(KV_LORA, MLA_HEADS // 2, 2, V_DIM)
    wv = jnp.stack([jnp.concatenate([wv2[:, :, 0], zv], axis=2),
                    jnp.concatenate([zv, wv2[:, :, 1]], axis=2)], axis=2)
    hp = MLA_HEADS * HEAD_PAD
    vone = np.zeros((MLA_HEADS // 2, 2, HEAD_PAD), np.float32)
    vone[:, 0, V_DIM] = 1.0
    vone[:, 1, 0] = 1.0
    return {"win": win.astype(BF16),
            "qn": q_norm.reshape(1, Q_LORA), "kvn": kv_norm.reshape(1, KV_LORA),
            "wqa": wqa.reshape(Q_LORA, hp).astype(BF16),
            "wqb": wqb.reshape(Q_LORA, hp).astype(BF16),
            "wk": wk.reshape(KV_LORA, hp).astype(BF16),
            "wv": wv.reshape(KV_LORA, hp).astype(BF16),
            "vone": jnp.asarray(vone.reshape(1, hp))}


def _pad_router(r):
    return jnp.pad(r, ((0, 0), (0, LANES - N_EXPERTS))).astype(BF16)


def kernel(x, c, ctx, c_ctx, ada_w, ada_b, norm_mix, norm_ffn, norm_final, lru_w_in, lru_conv_w, lru_conv_b, lru_gate_w, lru_gate_b, lru_lambda, lru_w_out, mla_w_in, mla_q_norm, mla_kv_norm, mla_w_uq, mla_w_ukv, mla_w_o, moe_router, moe_w_up, moe_w_down):
    B, T, D = x.shape
    Lc = ctx.shape[1]

    rows = ((B + 1 + 7) // 8) * 8
    cc = jnp.concatenate([c, c_ctx[None, :], jnp.zeros((rows - B - 1, D), F32)], axis=0)
    ada = _adaln(cc, ada_w, ada_b).reshape(DEPTH, rows, 6, D)

    def mods(i):
        lat = [ada[i, :B, k].reshape(B, 1, D) for k in range(6)]
        cx = [jnp.broadcast_to(ada[i, B, k].reshape(1, 1, D), (B, 1, D)) for k in range(6)]
        return lat, cx

    vec = lambda a: a.reshape(1, -1)

    (s1, sc1, g1, s2, sc2, g2), (s1c, sc1c, g1c, s2c, sc2c, g2c) = mods(0)
    w_in = lru_w_in[0].astype(BF16)
    cw, cb = 0.5 * lru_conv_w[0], vec(0.5 * lru_conv_b[0])
    g_lat, xc_lat = _lru_in(x, vec(norm_mix[0]), s1, sc1, w_in, cw, cb, tt=512)
    g_ctx, xc_ctx = _lru_in(ctx, vec(norm_mix[0]), s1c, sc1c, w_in, cw, cb, tt=Lc)
    ys = []
    for d in range(2):
        gw = lru_gate_w[0, d]
        wg = jnp.concatenate([gw[0], gw[1]], axis=-1).astype(BF16)
        ys.append(_lru_scan(xc_ctx, xc_lat, wg, 0.5 * lru_gate_b[0, d], vec(lru_lambda[0, d]),
                            reverse=(d == 1)))
    w_out = lru_w_out[0].astype(BF16)
    router = _pad_router(moe_router[0])
    gf = vec(norm_ffn[0])
    x_a, h_lat, aff_lat = _mix_out((ys[0][1], ys[1][1], g_lat), x, w_out, g1, gf, s2, sc2, router, tm=512)
    c_a, h_ctx, aff_ctx = _mix_out((ys[0][0], ys[1][0], g_ctx), ctx, w_out, g1c, gf, s2c, sc2c, router, tm=Lc)
    x_b = _ec_moe(x_a, h_lat, aff_lat, moe_w_up, moe_w_down, 0, g2, gf, final=False)
    c_b = _ec_moe(c_a, h_ctx, aff_ctx, moe_w_up, moe_w_down, 0, g2c, gf, final=False)

    (s1, sc1, g1, s2, sc2, g2), (s1c, sc1c, _, _, _, _) = mods(1)
    w = _mla_weights(mla_w_in[0], mla_q_norm[0], mla_kv_norm[0], mla_w_uq[0], mla_w_ukv[0])
    cos_t, sin_t = _rope_tables(T // GRID_W)
    one_t = jnp.concatenate([jnp.ones((Lc, QK_DIM), F32), jnp.zeros((Lc, HEAD_PAD - QK_DIM), F32)], axis=1)
    gm = vec(norm_mix[1])
    q, k_lat, v_lat = _mla_in(x_b, gm, s1, sc1, w, cos_t, sin_t, True, tm=512)
    k_ctx, v_ctx = _mla_in(c_b, gm, s1c, sc1c, w, one_t, jnp.zeros_like(one_t), False, tm=Lc)
    attn = _attention(q, k_lat, k_ctx, v_lat, v_ctx, tq=1024)
    x_c, h_lat, aff_lat = _mix_out((attn,), x_b, mla_w_o[0].astype(BF16), g1, vec(norm_ffn[1]),
                                   s2, sc2, _pad_router(moe_router[1]), tm=512)
    return _ec_moe(x_c, h_lat, aff_lat, moe_w_up, moe_w_down, 1, g2, vec(norm_final), final=True)
```

```python
import functools
import math

import numpy as np
import jax
import jax.numpy as jnp
from jax import lax
from jax.experimental import pallas as pl
from jax.experimental.pallas import tpu as pltpu

F32 = jnp.float32
BF16 = jnp.bfloat16
I32 = jnp.int32

D_MODEL = 1024
DEPTH = 2
GRID_W = 64
D_RNN = D_MODEL
CONV_W = 4
LRU_BLOCKS = 8
LRU_BLOCK = D_RNN // LRU_BLOCKS
LRU_C = 8.0
MLA_HEADS = 16
QK_NOPE = 64
QK_ROPE = 32
QK_DIM = QK_NOPE + QK_ROPE
V_DIM = 64
Q_LORA = 384
KV_LORA = 256
ROPE_FREQS = QK_ROPE // 4
ROPE_BASE = 10000.0
ATTN_SCALE = QK_DIM ** -0.5
N_EXPERTS = 16
D_EXPERT = 1024
CAPACITY_FACTOR = 2
EPS = 1e-6

LANES = 128
HEAD_PAD = LANES
VMEM_LIMIT = 56 * 1024 * 1024
NEG_BIG = -1e30
HALO = 8
MOE_ROWS = 2048
SCAN_TT = 64
COMBINE_ROWS = 256
COMBINE_WIN = 64
SLOT_ALIGN = 16
ATTN_QB = 256
EXP2_SCALE = ATTN_SCALE * math.log2(math.e)


def _cparams(sem):
    return pltpu.CompilerParams(dimension_semantics=sem, vmem_limit_bytes=VMEM_LIMIT)


def _sigmoid(x):
    return 0.5 * (jnp.tanh(0.5 * x) + 1.0)


def _silu(x):
    return x * _sigmoid(x)


def _gelu_tanh(x):
    return 0.5 * x * (1.0 + jnp.tanh(0.7978845608028654 * (x + 0.044715 * (x * x * x))))


def _rms(x, gamma):
    return x * lax.rsqrt(jnp.mean(x * x, axis=-1, keepdims=True) + EPS) * gamma


def _rms_mod(x, gamma, shift, scale):
    return _rms(x, gamma) * (1.0 + scale) + shift


def _dot(a, b):
    return jnp.dot(a, b, preferred_element_type=F32)


def _dot_nt(a, b):
    return lax.dot_general(a, b, (((1,), (1,)), ((), ())), preferred_element_type=F32)


def _adaln_kernel(c_ref, w_ref, b_ref, o_ref):
    s = _silu(c_ref[...]).astype(BF16)
    o_ref[0] = _dot(s, w_ref[0].astype(BF16)) + b_ref[0]


def _adaln(cc, ada_w, ada_b):
    rows = cc.shape[0]
    tn = 1024
    return pl.pallas_call(
        _adaln_kernel,
        grid=(DEPTH, 6 * D_MODEL // tn),
        in_specs=[pl.BlockSpec((rows, D_MODEL), lambda l, j: (0, 0)),
                  pl.BlockSpec((1, D_MODEL, tn), lambda l, j: (l, 0, j)),
                  pl.BlockSpec((1, 1, tn), lambda l, j: (l, 0, j))],
        out_specs=pl.BlockSpec((1, rows, tn), lambda l, j: (l, 0, j)),
        out_shape=jax.ShapeDtypeStruct((DEPTH, rows, 6 * D_MODEL), F32),
        compiler_params=_cparams(("arbitrary", "arbitrary")),
        name="adaln",
    )(cc, ada_w, ada_b.reshape(DEPTH, 1, 6 * D_MODEL))


def _lru_in_kernel(xp_ref, x_ref, xn_ref, gam_ref, sh_ref, sc_ref, w_ref, cw_ref, cb_ref,
                   g_ref, xc_ref, *, tt):
    i = pl.program_id(1)
    n = pl.num_programs(1)
    xe = jnp.concatenate([xp_ref[0], x_ref[0], xn_ref[0]], axis=0)
    he = _rms_mod(xe, gam_ref[...], sh_ref[0], sc_ref[0]).astype(BF16)
    z = _dot(he, w_ref[...])
    g_ref[0] = _gelu_tanh(z[HALO:HALO + tt, :D_RNN]).astype(BF16)
    z2 = z[:, D_RNN:]
    row = lax.broadcasted_iota(I32, (tt + 2 * HALO, 1), 0)
    valid = jnp.logical_and(jnp.logical_or(row >= HALO, i > 0),
                            jnp.logical_or(row < tt + HALO, i < n - 1))
    z2 = jnp.where(valid, z2, 0.0)
    cw = cw_ref[...]
    base = HALO - CONV_W // 2
    acc = z2[base:base + tt] * cw[0:1]
    for k in range(1, CONV_W):
        acc = acc + z2[base + k:base + k + tt] * cw[k:k + 1]
    xc_ref[0] = (acc + cb_ref[...]).astype(BF16)


def _lru_in(x, gamma, shift, scale, w_in, conv_w, conv_b, tt):
    B, L, D = x.shape
    nt = L // tt
    hb = tt // HALO
    last = L // HALO - 1
    vec = pl.BlockSpec((1, D), lambda b, i: (0, 0))
    mod = pl.BlockSpec((1, 1, D), lambda b, i: (b, 0, 0))
    out = pl.BlockSpec((1, tt, D_RNN), lambda b, i: (b, i, 0))
    return pl.pallas_call(
        functools.partial(_lru_in_kernel, tt=tt),
        grid=(B, nt),
        in_specs=[pl.BlockSpec((1, HALO, D), lambda b, i: (b, jnp.maximum(i * hb - 1, 0), 0)),
                  pl.BlockSpec((1, tt, D), lambda b, i: (b, i, 0)),
                  pl.BlockSpec((1, HALO, D), lambda b, i: (b, jnp.minimum((i + 1) * hb, last), 0)),
                  vec, mod, mod,
                  pl.BlockSpec((D, 2 * D_RNN), lambda b, i: (0, 0)),
                  pl.BlockSpec((CONV_W, D_RNN), lambda b, i: (0, 0)),
                  pl.BlockSpec((1, D_RNN), lambda b, i: (0, 0))],
        out_specs=[out, out],
        out_shape=[jax.ShapeDtypeStruct((B, L, D_RNN), BF16)] * 2,
        compiler_params=_cparams(("arbitrary", "arbitrary")),
        name="lru_in",
    )(x, x, x, gamma, shift, scale, w_in, conv_w, conv_b)


def _scan_kernel(xc_ctx_ref, xc_lat_ref, wg_ref, bg_ref, lam_ref, y_ctx_ref, y_lat_ref,
                 a_s, u_s, y_s, h_s, *, reverse, n_ctx):
    i = pl.program_id(0)
    nb, tt, _ = a_s.shape

    @pl.when(i == 0)
    def _():
        h_s[...] = jnp.zeros_like(h_s)

    def tile(x_ref, y_ref):
        x2 = x_ref[...].reshape(nb * tt, D_RNN)
        nlam = -lam_ref[...]
        sp = jnp.maximum(nlam, 0.0) + jnp.log1p(jnp.exp(-jnp.abs(nlam)))
        k2 = sp * (-0.5 * LRU_C * math.log2(math.e))
        for n in range(LRU_BLOCKS):
            sl = slice(n * LRU_BLOCK, (n + 1) * LRU_BLOCK)
            xb = x2[:, sl]
            g = _dot(xb, wg_ref[n])
            tr = jnp.tanh(g[:, :LRU_BLOCK] + bg_ref[0:1, sl])
            ti = jnp.tanh(g[:, LRU_BLOCK:] + bg_ref[1:2, sl])
            a = jnp.exp2(k2[:, sl] * tr + k2[:, sl])
            v = 1.0 - a * a
            root = jnp.where(v > 0.0, v * lax.rsqrt(v), 0.0)
            u = root * ((ti + 1.0) * xb.astype(F32))
            a_s[:, :, sl] = a.reshape(nb, tt, LRU_BLOCK)
            u_s[:, :, sl] = u.reshape(nb, tt, LRU_BLOCK)
        h = h_s[...]
        steps = range(tt - 1, -1, -1) if reverse else range(tt)
        for t in steps:
            h = a_s[:, t, :] * h + u_s[:, t, :]
            y_s[:, t, :] = h
        h_s[...] = h
        y_ref[...] = y_s[...].astype(BF16)

    @pl.when(i < n_ctx)
    def _():
        tile(xc_ctx_ref, y_ctx_ref)

    @pl.when(i >= n_ctx)
    def _():
        tile(xc_lat_ref, y_lat_ref)


def _lru_scan(xc_ctx, xc_lat, wg, bg, lam, reverse):
    B, Lc, _ = xc_ctx.shape
    L = xc_lat.shape[1]
    tt = SCAN_TT
    n_ctx = Lc // tt
    n_lat = L // tt

    def ctx_map(i):
        j = jnp.minimum(i, n_ctx - 1)
        return (0, (n_ctx - 1 - j) if reverse else j, 0)

    def lat_map(i):
        j = jnp.maximum(i - n_ctx, 0)
        return (0, (n_lat - 1 - j) if reverse else j, 0)

    return pl.pallas_call(
        functools.partial(_scan_kernel, reverse=reverse, n_ctx=n_ctx),
        grid=(n_ctx + n_lat,),
        in_specs=[pl.BlockSpec((B, tt, D_RNN), ctx_map),
                  pl.BlockSpec((B, tt, D_RNN), lat_map),
                  pl.BlockSpec((LRU_BLOCKS, LRU_BLOCK, 2 * LRU_BLOCK), lambda i: (0, 0, 0)),
                  pl.BlockSpec((2, D_RNN), lambda i: (0, 0)),
                  pl.BlockSpec((1, D_RNN), lambda i: (0, 0))],
        out_specs=[pl.BlockSpec((B, tt, D_RNN), ctx_map),
                   pl.BlockSpec((B, tt, D_RNN), lat_map)],
        out_shape=[jax.ShapeDtypeStruct((B, Lc, D_RNN), BF16),
                   jax.ShapeDtypeStruct((B, L, D_RNN), BF16)],
        scratch_shapes=[pltpu.VMEM((B, tt, D_RNN), F32),
                        pltpu.VMEM((B, tt, D_RNN), F32),
                        pltpu.VMEM((B, tt, D_RNN), F32),
                        pltpu.VMEM((B, D_RNN), F32)],
        compiler_params=_cparams(("arbitrary",)),
        name="lru_scan_rev" if reverse else "lru_scan_fwd",
    )(xc_ctx, xc_lat, wg, bg, lam)


def _mix_out_kernel(*refs, lru):
    if lru:
        (yf_ref, yb_ref, g_ref, x_ref, w_ref, g1_ref, gam_ref, sh_ref, sc_ref, r_ref,
         xo_ref, h_ref, aff_ref) = refs
        y = yf_ref[0].astype(F32) + yb_ref[0].astype(F32)
        lhs = (y * g_ref[0].astype(F32)).astype(BF16)
    else:
        (a_ref, x_ref, w_ref, g1_ref, gam_ref, sh_ref, sc_ref, r_ref,
         xo_ref, h_ref, aff_ref) = refs
        lhs = a_ref[0]
    xn = x_ref[0] + g1_ref[0] * _dot(lhs, w_ref[...])
    xo_ref[0] = xn
    hf = _rms_mod(xn, gam_ref[...], sh_ref[0], sc_ref[0])
    h_ref[0] = hf
    logits = _dot(hf.astype(BF16), r_ref[...])
    lane = lax.broadcasted_iota(I32, logits.shape, 1)
    logits = jnp.where(lane < N_EXPERTS, logits, NEG_BIG)
    p = jnp.exp(logits - jnp.max(logits, axis=-1, keepdims=True))
    aff_ref[0] = p / jnp.sum(p, axis=-1, keepdims=True)


def _mix_out(acts, x, w_out, g1, gamma, shift, scale, router_pad, tm):
    B, L, D = x.shape
    row = pl.BlockSpec((1, tm, D), lambda b, i: (b, i, 0))
    vec = pl.BlockSpec((1, D), lambda b, i: (0, 0))
    mod = pl.BlockSpec((1, 1, D), lambda b, i: (b, 0, 0))
    return pl.pallas_call(
        functools.partial(_mix_out_kernel, lru=len(acts) == 3),
        grid=(B, L // tm),
        in_specs=[row] * len(acts) + [
            row,
            pl.BlockSpec((D, D), lambda b, i: (0, 0)),
            mod, vec, mod, mod,
            pl.BlockSpec((D, LANES), lambda b, i: (0, 0))],
        out_specs=[row, row,
                   pl.BlockSpec((1, tm, LANES), lambda b, i: (b, i, 0))],
        out_shape=[jax.ShapeDtypeStruct((B, L, D), F32),
                   jax.ShapeDtypeStruct((B, L, D), F32),
                   jax.ShapeDtypeStruct((B, L, LANES), F32)],
        compiler_params=_cparams(("arbitrary", "arbitrary")),
        name="mix_out_lru" if len(acts) == 3 else "mix_out_attn",
    )(*acts, x, w_out, g1, gamma, shift, scale, router_pad)


def _topk_kernel(a_ref, offs_ref, slot_ref, first_ref, idx_ref, cum_s, *, cap, rb):
    L, ncol = a_ref.shape
    a = a_ref[...]

    def as_f32(bits):
        return lax.bitcast_convert_type(bits, F32)

    def search(k, thr):
        cand = thr | lax.shift_left(jnp.int32(1), 30 - k)
        cnt = jnp.sum((a >= as_f32(cand)).astype(I32), axis=0, keepdims=True)
        return jnp.where(cnt >= cap, cand, thr)

    thr = lax.fori_loop(0, 31, search, jnp.zeros((1, ncol), I32))
    lo = as_f32(thr)
    hi = as_f32(thr + 1)
    need = (cap - jnp.sum((a >= hi).astype(I32), axis=0, keepdims=True)).astype(F32)
    eq_b = jnp.logical_and(a >= lo, a < hi).astype(F32).astype(BF16)
    col = lax.broadcasted_iota(I32, (rb, L), 1)
    rowi = lax.broadcasted_iota(I32, (rb, L), 0)

    def before(r0):
        return (col < rowi + r0).astype(F32).astype(BF16)

    sel_blocks = []
    for r0 in range(0, L, rb):
        blk = a[r0:r0 + rb]
        eq_rank = _dot(before(r0), eq_b)
        take = jnp.logical_or(blk >= hi, jnp.logical_and(blk >= lo, eq_rank < need))
        sel_blocks.append(take.astype(F32))
    sel = jnp.concatenate(sel_blocks, axis=0) if len(sel_blocks) > 1 else sel_blocks[0]
    sel_b = sel.astype(BF16)
    offs = offs_ref[...]
    for k, r0 in enumerate(range(0, L, rb)):
        ahead = _dot(before(r0), sel_b)
        pos = ahead.astype(I32) + offs
        slot_ref[r0:r0 + rb, :] = jnp.where(sel_blocks[k] > 0.0, pos, -1)
        first_ref[k:k + 1, :] = pos[0:1, :]
        cum_s[r0:r0 + rb, :] = ahead + sel_blocks[k]

    def rows_of_slots(i, carry):
        base = i * 8
        cum = cum_s[...]
        out = [jnp.sum((cum <= jnp.asarray(base + j, F32)).astype(I32), axis=0, keepdims=True)
               for j in range(8)]
        idx_ref[pl.ds(pl.multiple_of(base, 8), 8), :] = jnp.concatenate(out, axis=0)
        return carry

    lax.fori_loop(0, cap // 8, rows_of_slots, 0)


def _topk_slots(aff_cols, offs, cap):
    L, ncol = aff_cols.shape
    rb = min(L, COMBINE_ROWS)
    return pl.pallas_call(
        functools.partial(_topk_kernel, cap=cap, rb=rb),
        grid=(1,),
        in_specs=[pl.BlockSpec((L, ncol), lambda i: (0, 0)),
                  pl.BlockSpec((1, ncol), lambda i: (0, 0))],
        out_specs=[pl.BlockSpec((L, ncol), lambda i: (0, 0)),
                   pl.BlockSpec((L // rb, ncol), lambda i: (0, 0)),
                   pl.BlockSpec((cap, ncol), lambda i: (0, 0))],
        out_shape=[jax.ShapeDtypeStruct((L, ncol), I32),
                   jax.ShapeDtypeStruct((L // rb, ncol), I32),
                   jax.ShapeDtypeStruct((cap, ncol), I32)],
        scratch_shapes=[pltpu.VMEM((L, ncol), F32)],
        compiler_params=_cparams(("arbitrary",)),
        name="topk_slots",
    )(aff_cols, offs)


def _moe_ffn_kernel(idx_ref, idx_next_ref, h_hbm, slot_ref, gate_ref, wu_ref, wd_ref, yg_ref,
                    wu_s, wd_s, rows_a, rows_b, sems, *, n_slots):
    n_groups = pl.num_programs(1)
    step = pl.program_id(0) * n_groups + pl.program_id(1)
    last = pl.num_programs(0) * n_groups - 1
    rows = slot_ref.shape[3]

    def gather(ids_ref, buf, sem):
        for s in range(n_slots):
            pltpu.make_async_copy(h_hbm.at[pl.ds(ids_ref[0, 0, 0, s], 1)], buf.at[pl.ds(s, 1)], sem).start()

    def gathered(buf, sem):
        pltpu.make_async_copy(h_hbm.at[pl.ds(0, n_slots)], buf, sem).wait()

    @pl.when(step == 0)
    def _():
        gather(idx_ref, rows_a, sems.at[0])

    @pl.when(pl.program_id(1) == 0)
    def _():
        wu_s[...] = wu_ref[0, 0].astype(BF16)
        wd_s[...] = wd_ref[0, 0].astype(BF16)

    def ffn(cur, cur_sem, nxt, nxt_sem):
        gather(idx_next_ref, nxt, nxt_sem)
        hit = lax.broadcasted_iota(I32, (n_slots, rows), 0) == slot_ref[0, 0]
        gate_s = jnp.sum(jnp.where(hit, gate_ref[0, 0], 0.0), axis=1, keepdims=True)
        gathered(cur, cur_sem)
        xg = cur[...].astype(BF16)
        up = _dot(xg, wu_s[...])
        hid = (_silu(up[:, :D_EXPERT]) * up[:, D_EXPERT:]).astype(BF16)
        yg_ref[0, 0] = (_dot(hid, wd_s[...]) * gate_s).astype(BF16)

        @pl.when(step == last)
        def _():
            gathered(nxt, nxt_sem)

    @pl.when(step % 2 == 0)
    def _():
        ffn(rows_a, sems.at[0], rows_b, sems.at[1])

    @pl.when(step % 2 == 1)
    def _():
        ffn(rows_b, sems.at[1], rows_a, sems.at[0])


def _moe_ffn(idx_g, h_rows, slot_g, gate_g, w_up, w_down, layer, n_slots):
    ng, _, _, rows = slot_g.shape
    D = D_MODEL
    n_steps = N_EXPERTS * ng

    def next_ids(e, g):
        n = jnp.minimum(e * ng + g + 1, n_steps - 1)
        return (n % ng, n // ng, 0, 0)

    route = pl.BlockSpec((1, 1, 1, rows), lambda e, g: (g, e, 0, 0))
    return pl.pallas_call(
        functools.partial(_moe_ffn_kernel, n_slots=n_slots),
        grid=(N_EXPERTS, ng),
        in_specs=[pl.BlockSpec((1, 1, 1, n_slots), lambda e, g: (g, e, 0, 0), memory_space=pltpu.SMEM),
                  pl.BlockSpec((1, 1, 1, n_slots), next_ids, memory_space=pltpu.SMEM),
                  pl.BlockSpec(memory_space=pl.ANY),
                  route, route,
                  pl.BlockSpec((1, 1, D, 2 * D_EXPERT), lambda e, g: (layer, e, 0, 0)),
                  pl.BlockSpec((1, 1, D_EXPERT, D), lambda e, g: (layer, e, 0, 0))],
        out_specs=pl.BlockSpec((1, 1, n_slots, D), lambda e, g: (g, e, 0, 0)),
        out_shape=jax.ShapeDtypeStruct((ng, N_EXPERTS, n_slots, D), BF16),
        scratch_shapes=[pltpu.VMEM((D, 2 * D_EXPERT), BF16),
                        pltpu.VMEM((D_EXPERT, D), BF16),
                        pltpu.VMEM((n_slots, D), F32),
                        pltpu.VMEM((n_slots, D), F32),
                        pltpu.SemaphoreType.DMA((2,))],
        compiler_params=_cparams(("arbitrary", "arbitrary")),
        name="moe_ffn",
    )(idx_g, idx_g, h_rows, slot_g, gate_g, w_up, w_down)


def _moe_combine_kernel(win_ref, fits_ref, yg_ref, slot_ref, x_ref, g2_ref, gam_ref, o_ref, *, final):
    g = pl.program_id(0)
    c = pl.program_id(1)
    _, n_e, n_slots, D = yg_ref.shape
    chunk = x_ref.shape[1]
    slots = slot_ref[0]

    def finish(comb):
        xn = x_ref[0] + g2_ref[0] * comb
        o_ref[0] = _rms(xn, gam_ref[...]) if final else xn

    @pl.when(fits_ref[g, c] != 0)
    def _():
        lane = lax.broadcasted_iota(I32, (chunk, 2 * COMBINE_WIN), 1)
        pieces, windows = [], []
        for e in range(0, n_e, 2):
            rel = []
            for j in range(2):
                start = win_ref[g, c * n_e + e + j]
                r = slots[:, e + j:e + j + 1] - start
                inside = jnp.logical_and(r >= 0, r < COMBINE_WIN)
                rel.append(jnp.where(inside, r + j * COMBINE_WIN, -1))
                windows.append(yg_ref[0, e + j, pl.ds(pl.multiple_of(start, SLOT_ALIGN), COMBINE_WIN), :])
            hit = jnp.logical_or(lane == rel[0], lane == rel[1])
            pieces.append(hit.astype(F32).astype(BF16))
        finish(_dot(jnp.concatenate(pieces, axis=1), jnp.concatenate(windows, axis=0)))

    @pl.when(fits_ref[g, c] == 0)
    def _():
        lane = lax.broadcasted_iota(I32, (chunk, n_slots), 1)
        pieces = [(slots[:, e:e + 1] == lane).astype(F32).astype(BF16) for e in range(n_e)]
        finish(_dot(jnp.concatenate(pieces, axis=1), yg_ref[0].reshape(n_e * n_slots, D)))


def _moe_combine(win, fits, yg, slot_t, x_g, g2, gamma, L, final):
    ng, n_e, n_slots, D = yg.shape
    rows = x_g.shape[1]
    chunk = min(COMBINE_ROWS, L)
    row = pl.BlockSpec((1, chunk, D), lambda g, c, *_: (g, c, 0))
    return pl.pallas_call(
        functools.partial(_moe_combine_kernel, final=final),
        grid_spec=pltpu.PrefetchScalarGridSpec(
            num_scalar_prefetch=2,
            grid=(ng, rows // chunk),
            in_specs=[pl.BlockSpec((1, n_e, n_slots, D), lambda g, c, *_: (g, 0, 0, 0)),
                      pl.BlockSpec((1, chunk, LANES), lambda g, c, *_: (g, c, 0)),
                      row,
                      pl.BlockSpec((1, 1, D), lambda g, c, *_: ((g * rows + c * chunk) // L, 0, 0)),
                      pl.BlockSpec((1, D), lambda g, c, *_: (0, 0))],
            out_specs=row),
        out_shape=jax.ShapeDtypeStruct((ng, rows, D), F32),
        compiler_params=_cparams(("arbitrary", "arbitrary")),
        name="moe_combine_final" if final else "moe_combine",
    )(win, fits, yg, slot_t, x_g, g2, gamma)


def _combine_windows(first, ng, G, n_slots):
    nblk = first.shape[0]
    E = N_EXPERTS
    s = first.reshape(nblk, ng, G, E).transpose(1, 2, 0, 3).reshape(ng, G * nblk, E)
    s_next = jnp.concatenate([s[:, 1:], jnp.full((ng, 1, E), n_slots, I32)], axis=1)
    start = jnp.minimum((s // SLOT_ALIGN) * SLOT_ALIGN, n_slots - COMBINE_WIN)
    fits = jnp.all(s_next <= start + COMBINE_WIN, axis=2).astype(I32)
    return start.reshape(ng, G * nblk * E), fits


def _ec_moe(x, h, aff, w_up, w_down, layer, g2, gamma, final):
    B, L, D = x.shape
    E = N_EXPERTS
    cap = CAPACITY_FACTOR * L // E
    G = MOE_ROWS // L
    ng = B // G
    cols = aff[:, :, :E].transpose(1, 0, 2).reshape(L, B * E)
    offs = jnp.repeat((jnp.arange(B, dtype=I32) % G) * cap, E).reshape(1, B * E)
    slot, first, idx = _topk_slots(cols, offs, cap)

    def by_lane(a):
        return a.reshape(L, ng, G, E).transpose(1, 3, 2, 0).reshape(ng, E, 1, G * L)

    idx_g = idx.reshape(cap, ng, G, E).transpose(1, 3, 2, 0) + (jnp.arange(B, dtype=I32) * L).reshape(ng, 1, G, 1)
    yg = _moe_ffn(idx_g.reshape(ng, E, 1, G * cap), h.reshape(B * L, D),
                  by_lane(slot), by_lane(cols), w_up, w_down, layer, G * cap)
    slot_t = slot.reshape(L, ng, G, E).transpose(1, 2, 0, 3).reshape(ng, G * L, E)
    slot_t = jnp.pad(slot_t, ((0, 0), (0, 0), (0, LANES - E)), constant_values=-1)
    win, fits = _combine_windows(first, ng, G, G * cap)
    out = _moe_combine(win, fits, yg, slot_t, x.reshape(ng, G * L, D), g2, gamma, L, final)
    return out.reshape(B, L, D)


def _mla_in_kernel(*refs, need_q):
    (x_ref, gam_ref, sh_ref, sc_ref, win_ref, qn_ref, kvn_ref,
     wqa_ref, wqb_ref, wk_ref, wv_ref, vone_ref, cos_ref, sin_ref) = refs[:14]
    outs = refs[14:]
    h = _rms_mod(x_ref[0], gam_ref[...], sh_ref[0], sc_ref[0]).astype(BF16)
    z = _dot(h, win_ref[...])
    cos = cos_ref[...]
    sin = sin_ref[...]
    kv0 = Q_LORA + KV_LORA
    ckv = _rms(z[:, Q_LORA:kv0], kvn_ref[...]).astype(BF16)
    k_rope = z[:, kv0:kv0 + HEAD_PAD] * cos + z[:, kv0 + HEAD_PAD:kv0 + 2 * HEAD_PAD] * sin
    k_nope = _dot(ckv, wk_ref[...])
    if need_q:
        q_ref, k_ref, v_ref = outs
        cq = _rms(z[:, :Q_LORA], qn_ref[...]).astype(BF16)
        qa = _dot(cq, wqa_ref[...])
        qb = _dot(cq, wqb_ref[...])
    else:
        k_ref, v_ref = outs
    for hh in range(MLA_HEADS):
        sl = slice(hh * HEAD_PAD, (hh + 1) * HEAD_PAD)
        k_ref[0, :, sl] = (k_nope[:, sl] + k_rope).astype(BF16)
        if need_q:
            q_ref[0, :, sl] = (qa[:, sl] * cos + qb[:, sl] * sin).astype(BF16)
    v_ref[0] = (_dot(ckv, wv_ref[...]) + vone_ref[...]).astype(BF16)


def _mla_in(x, gamma, shift, scale, w, cos_t, sin_t, need_q, tm):
    B, L, D = x.shape
    row = pl.BlockSpec((1, tm, D), lambda b, i: (b, i, 0))
    mod = pl.BlockSpec((1, 1, D), lambda b, i: (b, 0, 0))

    def full(a):
        return pl.BlockSpec(a.shape, lambda b, i: (0,) * a.ndim)

    hp = MLA_HEADS * HEAD_PAD
    wide = pl.BlockSpec((1, tm, hp), lambda b, i: (b, i, 0))
    tab = pl.BlockSpec((tm, HEAD_PAD), lambda b, i: (i, 0))
    out_specs = [wide, wide]
    out_shape = [jax.ShapeDtypeStruct((B, L, hp), BF16)] * 2
    if need_q:
        out_specs = [wide] + out_specs
        out_shape = [jax.ShapeDtypeStruct((B, L, hp), BF16)] + out_shape
    weights = [w["win"], w["qn"], w["kvn"], w["wqa"], w["wqb"], w["wk"], w["wv"], w["vone"]]
    return pl.pallas_call(
        functools.partial(_mla_in_kernel, need_q=need_q),
        grid=(B, L // tm),
        in_specs=[row, full(gamma), mod, mod] + [full(a) for a in weights] + [tab, tab],
        out_specs=out_specs,
        out_shape=out_shape,
        compiler_params=_cparams(("arbitrary", "arbitrary")),
        name="mla_in_lat" if need_q else "mla_in_ctx",
    )(x, gamma, shift, scale, *weights, cos_t, sin_t)


def _attn_kernel(q_ref, kl_ref, kc_ref, vl_ref, vc_ref, o_ref):
    tq = q_ref.shape[1]
    blocks = [(j, r) for r in range(0, tq, ATTN_QB) for j in range(2)]

    def scores(j, r):
        sl = slice(j * HEAD_PAD, (j + 1) * HEAD_PAD)
        q = q_ref[0, r:r + ATTN_QB, sl]
        return _dot_nt(q, kl_ref[0, :, sl]), _dot_nt(q, kc_ref[0, :, sl])

    def weighted(j, s1, s2):
        sl = slice(j * HEAD_PAD, (j + 1) * HEAD_PAD)
        m = jnp.maximum(jnp.max(s1, axis=-1, keepdims=True), jnp.max(s2, axis=-1, keepdims=True))
        p1 = jnp.exp2((s1 - m) * EXP2_SCALE).astype(BF16)
        p2 = jnp.exp2((s2 - m) * EXP2_SCALE).astype(BF16)
        return _dot(p1, vl_ref[0, :, sl]) + _dot(p2, vc_ref[0, :, sl])

    pending = scores(*blocks[0])
    acc = {}
    for n, (j, r) in enumerate(blocks):
        s1, s2 = pending
        if n + 1 < len(blocks):
            pending = scores(*blocks[n + 1])
        acc[(j, r)] = weighted(j, s1, s2)
    lane = lax.broadcasted_iota(I32, (ATTN_QB, 2 * V_DIM), 1)
    for r in range(0, tq, ATTN_QB):
        even = acc[(0, r)]
        odd = acc[(1, r)]
        even = even / even[:, V_DIM:V_DIM + 1]
        odd = odd / odd[:, 0:1]
        o_ref[0, r:r + ATTN_QB, :] = jnp.where(lane < V_DIM, even, odd).astype(BF16)


def _attention(q, k_lat, k_ctx, v_lat, v_ctx, tq):
    B, L, _ = q.shape
    Lc = k_ctx.shape[1]
    pair = 2 * HEAD_PAD
    return pl.pallas_call(
        _attn_kernel,
        grid=(B, MLA_HEADS // 2, L // tq),
        in_specs=[pl.BlockSpec((1, tq, pair), lambda b, h, i: (b, i, h)),
                  pl.BlockSpec((1, L, pair), lambda b, h, i: (b, 0, h)),
                  pl.BlockSpec((1, Lc, pair), lambda b, h, i: (b, 0, h)),
                  pl.BlockSpec((1, L, pair), lambda b, h, i: (b, 0, h)),
                  pl.BlockSpec((1, Lc, pair), lambda b, h, i: (b, 0, h))],
        out_specs=pl.BlockSpec((1, tq, 2 * V_DIM), lambda b, h, i: (b, i, h)),
        out_shape=jax.ShapeDtypeStruct((B, L, MLA_HEADS * V_DIM), BF16),
        compiler_params=_cparams(("arbitrary", "arbitrary", "arbitrary")),
        name="attention",
    )(q, k_lat, k_ctx, v_lat, v_ctx)


def _rope_tables(rows):
    t = np.arange(rows * GRID_W)
    pos = np.stack([t // GRID_W, t % GRID_W], axis=-1).astype(np.float32)
    freq = jnp.asarray(ROPE_BASE, F32) ** (-jnp.arange(ROPE_FREQS, dtype=F32) / ROPE_FREQS)
    ang = jnp.asarray(pos)[:, :, None] * freq
    cos, sin = jnp.cos(ang), jnp.sin(ang)
    cos_r = jnp.stack([cos, cos], axis=2).reshape(-1, QK_ROPE)
    sin_r = jnp.stack([-sin, sin], axis=2).reshape(-1, QK_ROPE)
    T = cos_r.shape[0]
    pad = HEAD_PAD - QK_DIM
    cos_t = jnp.concatenate([jnp.ones((T, QK_NOPE), F32), cos_r, jnp.zeros((T, pad), F32)], axis=1)
    sin_t = jnp.concatenate([jnp.zeros((T, QK_NOPE), F32), sin_r, jnp.zeros((T, pad), F32)], axis=1)
    return cos_t, sin_t


def _rope_partner():
    p = np.arange(QK_ROPE)
    half = (p % (2 * ROPE_FREQS)) // ROPE_FREQS
    return np.where(half == 0, p + ROPE_FREQS, p - ROPE_FREQS)


def _mla_weights(w_in, q_norm, kv_norm, w_uq, w_ukv):
    partner = _rope_partner()
    pad = HEAD_PAD - QK_DIM
    kv0 = Q_LORA + KV_LORA
    kr = w_in[:, kv0:]
    zl = jnp.zeros((D_MODEL, QK_NOPE), F32)
    zr = jnp.zeros((D_MODEL, pad), F32)
    win = jnp.concatenate([w_in[:, :kv0], zl, kr, zr, zl, kr[:, partner], zr], axis=1)
    wq = w_uq.reshape(Q_LORA, MLA_HEADS, QK_DIM)
    zq = jnp.zeros((Q_LORA, MLA_HEADS, pad), F32)
    wqa = jnp.concatenate([wq, zq], axis=2)
    wqb = jnp.concatenate([jnp.zeros((Q_LORA, MLA_HEADS, QK_NOPE), F32),
                           wq[:, :, QK_NOPE:][:, :, partner], zq], axis=2)
    wkv = w_ukv.reshape(KV_LORA, MLA_HEADS, QK_NOPE + V_DIM)
    wk = jnp.concatenate([wkv[:, :, :QK_NOPE],
                          jnp.zeros((KV_LORA, MLA_HEADS, HEAD_PAD - QK_NOPE), F32)], axis=2)
    zv = jnp.zeros((KV_LORA, MLA_HEADS // 2, HEAD_PAD - V_DIM), F32)
    wv2 = wkv[:, :, QK_NOPE:].reshape(KV_LORA, MLA_HEADS // 2, 2, V_DIM)
    wv = jnp.stack([jnp.concatenate([wv2[:, :, 0], zv], axis=2),
                    jnp.concatenate([zv, wv2[:, :, 1]], axis=2)], axis=2)
    hp = MLA_HEADS * HEAD_PAD
    vone = np.zeros((MLA_HEADS // 2, 2, HEAD_PAD), np.float32)
    vone[:, 0, V_DIM] = 1.0
    vone[:, 1, 0] = 1.0
    return {"win": win.astype(BF16),
            "qn": q_norm.reshape(1, Q_LORA), "kvn": kv_norm.reshape(1, KV_LORA),
            "wqa": wqa.reshape(Q_LORA, hp).astype(BF16),
            "wqb": wqb.reshape(Q_LORA, hp).astype(BF16),
            "wk": wk.reshape(KV_LORA, hp).astype(BF16),
            "wv": wv.reshape(KV_LORA, hp).astype(BF16),
            "vone": jnp.asarray(vone.reshape(1, hp))}


def _pad_router(r):
    return jnp.pad(r, ((0, 0), (0, LANES - N_EXPERTS))).astype(BF16)


def kernel(x, c, ctx, c_ctx, ada_w, ada_b, norm_mix, norm_ffn, norm_final, lru_w_in, lru_conv_w, lru_conv_b, lru_gate_w, lru_gate_b, lru_lambda, lru_w_out, mla_w_in, mla_q_norm, mla_kv_norm, mla_w_uq, mla_w_ukv, mla_w_o, moe_router, moe_w_up, moe_w_down):
    B, T, D = x.shape
    Lc = ctx.shape[1]

    rows = ((B + 1 + 7) // 8) * 8
    cc = jnp.concatenate([c, c_ctx[None, :], jnp.zeros((rows - B - 1, D), F32)], axis=0)
    ada = _adaln(cc, ada_w, ada_b).reshape(DEPTH, rows, 6, D)

    def mods(i):
        lat = [ada[i, :B, k].reshape(B, 1, D) for k in range(6)]
        cx = [jnp.broadcast_to(ada[i, B, k].reshape(1, 1, D), (B, 1, D)) for k in range(6)]
        return lat, cx

    vec = lambda a: a.reshape(1, -1)

    (s1, sc1, g1, s2, sc2, g2), (s1c, sc1c, g1c, s2c, sc2c, g2c) = mods(0)
    w_in = lru_w_in[0].astype(BF16)
    cw, cb = 0.5 * lru_conv_w[0], vec(0.5 * lru_conv_b[0])
    g_lat, xc_lat = _lru_in(x, vec(norm_mix[0]), s1, sc1, w_in, cw, cb, tt=512)
    g_ctx, xc_ctx = _lru_in(ctx, vec(norm_mix[0]), s1c, sc1c, w_in, cw, cb, tt=Lc)
    ys = []
    for d in range(2):
        gw = lru_gate_w[0, d]
        wg = jnp.concatenate([gw[0], gw[1]], axis=-1).astype(BF16)
        ys.append(_lru_scan(xc_ctx, xc_lat, wg, 0.5 * lru_gate_b[0, d], vec(lru_lambda[0, d]),
                            reverse=(d == 1)))
    w_out = lru_w_out[0].astype(BF16)
    router = _pad_router(moe_router[0])
    gf = vec(norm_ffn[0])
    x_a, h_lat, aff_lat = _mix_out((ys[0][1], ys[1][1], g_lat), x, w_out, g1, gf, s2, sc2, router, tm=512)
    c_a, h_ctx, aff_ctx = _mix_out((ys[0][0], ys[1][0], g_ctx), ctx, w_out, g1c, gf, s2c, sc2c, router, tm=Lc)
    x_b = _ec_moe(x_a, h_lat, aff_lat, moe_w_up, moe_w_down, 0, g2, gf, final=False)
    c_b = _ec_moe(c_a, h_ctx, aff_ctx, moe_w_up, moe_w_down, 0, g2c, gf, final=False)

    (s1, sc1, g1, s2, sc2, g2), (s1c, sc1c, _, _, _, _) = mods(1)
    w = _mla_weights(mla_w_in[0], mla_q_norm[0], mla_kv_norm[0], mla_w_uq[0], mla_w_ukv[0])
    cos_t, sin_t = _rope_tables(T // GRID_W)
    one_t = jnp.concatenate([jnp.ones((Lc, QK_DIM), F32), jnp.zeros((Lc, HEAD_PAD - QK_DIM), F32)], axis=1)
    gm = vec(norm_mix[1])
    q, k_lat, v_lat = _mla_in(x_b, gm, s1, sc1, w, cos_t, sin_t, True, tm=512)
    k_ctx, v_ctx = _mla_in(c_b, gm, s1c, sc1c, w, one_t, jnp.zeros_like(one_t), False, tm=Lc)
    attn = _attention(q, k_lat, k_ctx, v_lat, v_ctx, tq=1024)
    x_c, h_lat, aff_lat = _mix_out((attn,), x_b, mla_w_o[0].astype(BF16), g1, vec(norm_ffn[1]),
                                   s2, sc2, _pad_router(moe_router[1]), tm=512)
    return _ec_moe(x_c, h_lat, aff_lat, moe_w_up, moe_w_down, 1, g2, vec(norm_final), final=True)
```

```python
import functools
import math

import numpy as np
import jax
import jax.numpy as jnp
from jax import lax
from jax.experimental import pallas as pl
from jax.experimental.pallas import tpu as pltpu

F32 = jnp.float32
BF16 = jnp.bfloat16
I32 = jnp.int32

D_MODEL = 1024
DEPTH = 2
GRID_W = 64
D_RNN = D_MODEL
CONV_W = 4
LRU_BLOCKS = 8
LRU_BLOCK = D_RNN // LRU_BLOCKS
LRU_C = 8.0
MLA_HEADS = 16
QK_NOPE = 64
QK_ROPE = 32
QK_DIM = QK_NOPE + QK_ROPE
V_DIM = 64
Q_LORA = 384
KV_LORA = 256
ROPE_FREQS = QK_ROPE // 4
ROPE_BASE = 10000.0
ATTN_SCALE = QK_DIM ** -0.5
N_EXPERTS = 16
D_EXPERT = 1024
CAPACITY_FACTOR = 2
EPS = 1e-6

LANES = 128
HEAD_PAD = LANES
VMEM_LIMIT = 56 * 1024 * 1024
NEG_BIG = -1e30
HALO = 8
MOE_ROWS = 2048
SCAN_TT = 64
COMBINE_ROWS = 256
COMBINE_WIN = 64
SLOT_ALIGN = 16
ATTN_QB = 256
EXP2_SCALE = ATTN_SCALE * math.log2(math.e)


def _cparams(sem):
    return pltpu.CompilerParams(dimension_semantics=sem, vmem_limit_bytes=VMEM_LIMIT)


def _sigmoid(x):
    return 0.5 * (jnp.tanh(0.5 * x) + 1.0)


def _silu(x):
    return x * _sigmoid(x)


def _gelu_tanh(x):
    return 0.5 * x * (1.0 + jnp.tanh(0.7978845608028654 * (x + 0.044715 * (x * x * x))))


def _rms(x, gamma):
    return x * lax.rsqrt(jnp.mean(x * x, axis=-1, keepdims=True) + EPS) * gamma


def _rms_mod(x, gamma, shift, scale):
    return _rms(x, gamma) * (1.0 + scale) + shift


def _dot(a, b):
    return jnp.dot(a, b, preferred_element_type=F32)


def _dot_nt(a, b):
    return lax.dot_general(a, b, (((1,), (1,)), ((), ())), preferred_element_type=F32)


def _adaln_kernel(c_ref, w_ref, b_ref, o_ref):
    s = _silu(c_ref[...]).astype(BF16)
    o_ref[0] = _dot(s, w_ref[0].astype(BF16)) + b_ref[0]


def _adaln(cc, ada_w, ada_b):
    rows = cc.shape[0]
    tn = 1024
    return pl.pallas_call(
        _adaln_kernel,
        grid=(DEPTH, 6 * D_MODEL // tn),
        in_specs=[pl.BlockSpec((rows, D_MODEL), lambda l, j: (0, 0)),
                  pl.BlockSpec((1, D_MODEL, tn), lambda l, j: (l, 0, j)),
                  pl.BlockSpec((1, 1, tn), lambda l, j: (l, 0, j))],
        out_specs=pl.BlockSpec((1, rows, tn), lambda l, j: (l, 0, j)),
        out_shape=jax.ShapeDtypeStruct((DEPTH, rows, 6 * D_MODEL), F32),
        compiler_params=_cparams(("arbitrary", "arbitrary")),
        name="adaln",
    )(cc, ada_w, ada_b.reshape(DEPTH, 1, 6 * D_MODEL))


def _lru_in_kernel(xp_ref, x_ref, xn_ref, gam_ref, sh_ref, sc_ref, w_ref, cw_ref, cb_ref,
                   g_ref, xc_ref, *, tt):
    i = pl.program_id(1)
    n = pl.num_programs(1)
    xe = jnp.concatenate([xp_ref[0], x_ref[0], xn_ref[0]], axis=0)
    he = _rms_mod(xe, gam_ref[...], sh_ref[0], sc_ref[0]).astype(BF16)
    z = _dot(he, w_ref[...])
    g_ref[0] = _gelu_tanh(z[HALO:HALO + tt, :D_RNN]).astype(BF16)
    z2 = z[:, D_RNN:]
    row = lax.broadcasted_iota(I32, (tt + 2 * HALO, 1), 0)
    valid = jnp.logical_and(jnp.logical_or(row >= HALO, i > 0),
                            jnp.logical_or(row < tt + HALO, i < n - 1))
    z2 = jnp.where(valid, z2, 0.0)
    cw = cw_ref[...]
    base = HALO - CONV_W // 2
    acc = z2[base:base + tt] * cw[0:1]
    for k in range(1, CONV_W):
        acc = acc + z2[base + k:base + k + tt] * cw[k:k + 1]
    xc_ref[0] = (acc + cb_ref[...]).astype(BF16)


def _lru_in(x, gamma, shift, scale, w_in, conv_w, conv_b, tt):
    B, L, D = x.shape
    nt = L // tt
    hb = tt // HALO
    last = L // HALO - 1
    vec = pl.BlockSpec((1, D), lambda b, i: (0, 0))
    mod = pl.BlockSpec((1, 1, D), lambda b, i: (b, 0, 0))
    out = pl.BlockSpec((1, tt, D_RNN), lambda b, i: (b, i, 0))
    return pl.pallas_call(
        functools.partial(_lru_in_kernel, tt=tt),
        grid=(B, nt),
        in_specs=[pl.BlockSpec((1, HALO, D), lambda b, i: (b, jnp.maximum(i * hb - 1, 0), 0)),
                  pl.BlockSpec((1, tt, D), lambda b, i: (b, i, 0)),
                  pl.BlockSpec((1, HALO, D), lambda b, i: (b, jnp.minimum((i + 1) * hb, last), 0)),
                  vec, mod, mod,
                  pl.BlockSpec((D, 2 * D_RNN), lambda b, i: (0, 0)),
                  pl.BlockSpec((CONV_W, D_RNN), lambda b, i: (0, 0)),
                  pl.BlockSpec((1, D_RNN), lambda b, i: (0, 0))],
        out_specs=[out, out],
        out_shape=[jax.ShapeDtypeStruct((B, L, D_RNN), BF16)] * 2,
        compiler_params=_cparams(("arbitrary", "arbitrary")),
        name="lru_in",
    )(x, x, x, gamma, shift, scale, w_in, conv_w, conv_b)


def _scan_kernel(xc_ctx_ref, xc_lat_ref, wg_ref, bg_ref, lam_ref, y_ctx_ref, y_lat_ref,
                 a_s, u_s, y_s, h_s, *, reverse, n_ctx):
    i = pl.program_id(0)
    nb, tt, _ = a_s.shape

    @pl.when(i == 0)
    def _():
        h_s[...] = jnp.zeros_like(h_s)

    def tile(x_ref, y_ref):
        x2 = x_ref[...].reshape(nb * tt, D_RNN)
        nlam = -lam_ref[...]
        sp = jnp.maximum(nlam, 0.0) + jnp.log1p(jnp.exp(-jnp.abs(nlam)))
        k2 = sp * (-0.5 * LRU_C * math.log2(math.e))
        for n in range(LRU_BLOCKS):
            sl = slice(n * LRU_BLOCK, (n + 1) * LRU_BLOCK)
            xb = x2[:, sl]
            g = _dot(xb, wg_ref[n])
            tr = jnp.tanh(g[:, :LRU_BLOCK] + bg_ref[0:1, sl])
            ti = jnp.tanh(g[:, LRU_BLOCK:] + bg_ref[1:2, sl])
            a = jnp.exp2(k2[:, sl] * tr + k2[:, sl])
            v = 1.0 - a * a
            root = jnp.where(v > 0.0, v * lax.rsqrt(v), 0.0)
            u = root * ((ti + 1.0) * xb.astype(F32))
            a_s[:, :, sl] = a.reshape(nb, tt, LRU_BLOCK)
            u_s[:, :, sl] = u.reshape(nb, tt, LRU_BLOCK)
        h = h_s[...]
        steps = range(tt - 1, -1, -1) if reverse else range(tt)
        for t in steps:
            h = a_s[:, t, :] * h + u_s[:, t, :]
            y_s[:, t, :] = h
        h_s[...] = h
        y_ref[...] = y_s[...].astype(BF16)

    @pl.when(i < n_ctx)
    def _():
        tile(xc_ctx_ref, y_ctx_ref)

    @pl.when(i >= n_ctx)
    def _():
        tile(xc_lat_ref, y_lat_ref)


def _lru_scan(xc_ctx, xc_lat, wg, bg, lam, reverse):
    B, Lc, _ = xc_ctx.shape
    L = xc_lat.shape[1]
    tt = SCAN_TT
    n_ctx = Lc // tt
    n_lat = L // tt

    def ctx_map(i):
        j = jnp.minimum(i, n_ctx - 1)
        return (0, (n_ctx - 1 - j) if reverse else j, 0)

    def lat_map(i):
        j = jnp.maximum(i - n_ctx, 0)
        return (0, (n_lat - 1 - j) if reverse else j, 0)

    return pl.pallas_call(
        functools.partial(_scan_kernel, reverse=reverse, n_ctx=n_ctx),
        grid=(n_ctx + n_lat,),
        in_specs=[pl.BlockSpec((B, tt, D_RNN), ctx_map),
                  pl.BlockSpec((B, tt, D_RNN), lat_map),
                  pl.BlockSpec((LRU_BLOCKS, LRU_BLOCK, 2 * LRU_BLOCK), lambda i: (0, 0, 0)),
                  pl.BlockSpec((2, D_RNN), lambda i: (0, 0)),
                  pl.BlockSpec((1, D_RNN), lambda i: (0, 0))],
        out_specs=[pl.BlockSpec((B, tt, D_RNN), ctx_map),
                   pl.BlockSpec((B, tt, D_RNN), lat_map)],
        out_shape=[jax.ShapeDtypeStruct((B, Lc, D_RNN), BF16),
                   jax.ShapeDtypeStruct((B, L, D_RNN), BF16)],
        scratch_shapes=[pltpu.VMEM((B, tt, D_RNN), F32),
                        pltpu.VMEM((B, tt, D_RNN), F32),
                        pltpu.VMEM((B, tt, D_RNN), F32),
                        pltpu.VMEM((B, D_RNN), F32)],
        compiler_params=_cparams(("arbitrary",)),
        name="lru_scan_rev" if reverse else "lru_scan_fwd",
    )(xc_ctx, xc_lat, wg, bg, lam)


def _mix_out_kernel(*refs, lru):
    if lru:
        (yf_ref, yb_ref, g_ref, x_ref, w_ref, g1_ref, gam_ref, sh_ref, sc_ref, r_ref,
         xo_ref, h_ref, aff_ref) = refs
        y = yf_ref[0].astype(F32) + yb_ref[0].astype(F32)
        lhs = (y * g_ref[0].astype(F32)).astype(BF16)
    else:
        (a_ref, x_ref, w_ref, g1_ref, gam_ref, sh_ref, sc_ref, r_ref,
         xo_ref, h_ref, aff_ref) = refs
        lhs = a_ref[0]
    xn = x_ref[0] + g1_ref[0] * _dot(lhs, w_ref[...])
    xo_ref[0] = xn
    hf = _rms_mod(xn, gam_ref[...], sh_ref[0], sc_ref[0])
    h_ref[0] = hf
    logits = _dot(hf.astype(BF16), r_ref[...])
    lane = lax.broadcasted_iota(I32, logits.shape, 1)
    logits = jnp.where(lane < N_EXPERTS, logits, NEG_BIG)
    p = jnp.exp(logits - jnp.max(logits, axis=-1, keepdims=True))
    aff_ref[0] = p / jnp.sum(p, axis=-1, keepdims=True)


def _mix_out(acts, x, w_out, g1, gamma, shift, scale, router_pad, tm):
    B, L, D = x.shape
    row = pl.BlockSpec((1, tm, D), lambda b, i: (b, i, 0))
    vec = pl.BlockSpec((1, D), lambda b, i: (0, 0))
    mod = pl.BlockSpec((1, 1, D), lambda b, i: (b, 0, 0))
    return pl.pallas_call(
        functools.partial(_mix_out_kernel, lru=len(acts) == 3),
        grid=(B, L // tm),
        in_specs=[row] * len(acts) + [
            row,
            pl.BlockSpec((D, D), lambda b, i: (0, 0)),
            mod, vec, mod, mod,
            pl.BlockSpec((D, LANES), lambda b, i: (0, 0))],
        out_specs=[row, row,
                   pl.BlockSpec((1, tm, LANES), lambda b, i: (b, i, 0))],
        out_shape=[jax.ShapeDtypeStruct((B, L, D), F32),
                   jax.ShapeDtypeStruct((B, L, D), F32),
                   jax.ShapeDtypeStruct((B, L, LANES), F32)],
        compiler_params=_cparams(("arbitrary", "arbitrary")),
        name="mix_out_lru" if len(acts) == 3 else "mix_out_attn",
    )(*acts, x, w_out, g1, gamma, shift, scale, router_pad)


def _topk_kernel(a_ref, offs_ref, slot_ref, first_ref, idx_ref, cum_s, *, cap, rb):
    L, ncol = a_ref.shape
    a = a_ref[...]

    def as_f32(bits):
        return lax.bitcast_convert_type(bits, F32)

    def search(k, thr):
        cand = thr | lax.shift_left(jnp.int32(1), 30 - k)
        cnt = jnp.sum((a >= as_f32(cand)).astype(I32), axis=0, keepdims=True)
        return jnp.where(cnt >= cap, cand, thr)

    thr = lax.fori_loop(0, 31, search, jnp.zeros((1, ncol), I32))
    lo = as_f32(thr)
    hi = as_f32(thr + 1)
    need = (cap - jnp.sum((a >= hi).astype(I32), axis=0, keepdims=True)).astype(F32)
    eq_b = jnp.logical_and(a >= lo, a < hi).astype(F32).astype(BF16)
    col = lax.broadcasted_iota(I32, (rb, L), 1)
    rowi = lax.broadcasted_iota(I32, (rb, L), 0)

    def before(r0):
        return (col < rowi + r0).astype(F32).astype(BF16)

    sel_blocks = []
    for r0 in range(0, L, rb):
        blk = a[r0:r0 + rb]
        eq_rank = _dot(before(r0), eq_b)
        take = jnp.logical_or(blk >= hi, jnp.logical_and(blk >= lo, eq_rank < need))
        sel_blocks.append(take.astype(F32))
    sel = jnp.concatenate(sel_blocks, axis=0) if len(sel_blocks) > 1 else sel_blocks[0]
    sel_b = sel.astype(BF16)
    offs = offs_ref[...]
    for k, r0 in enumerate(range(0, L, rb)):
        ahead = _dot(before(r0), sel_b)
        pos = ahead.astype(I32) + offs
        slot_ref[r0:r0 + rb, :] = jnp.where(sel_blocks[k] > 0.0, pos, -1)
        first_ref[k:k + 1, :] = pos[0:1, :]
        cum_s[r0:r0 + rb, :] = ahead + sel_blocks[k]

    def rows_of_slots(i, carry):
        base = i * 8
        cum = cum_s[...]
        out = [jnp.sum((cum <= jnp.asarray(base + j, F32)).astype(I32), axis=0, keepdims=True)
               for j in range(8)]
        idx_ref[pl.ds(pl.multiple_of(base, 8), 8), :] = jnp.concatenate(out, axis=0)
        return carry

    lax.fori_loop(0, cap // 8, rows_of_slots, 0)


def _topk_slots(aff_cols, offs, cap):
    L, ncol = aff_cols.shape
    rb = min(L, COMBINE_ROWS)
    return pl.pallas_call(
        functools.partial(_topk_kernel, cap=cap, rb=rb),
        grid=(1,),
        in_specs=[pl.BlockSpec((L, ncol), lambda i: (0, 0)),
                  pl.BlockSpec((1, ncol), lambda i: (0, 0))],
        out_specs=[pl.BlockSpec((L, ncol), lambda i: (0, 0)),
                   pl.BlockSpec((L // rb, ncol), lambda i: (0, 0)),
                   pl.BlockSpec((cap, ncol), lambda i: (0, 0))],
        out_shape=[jax.ShapeDtypeStruct((L, ncol), I32),
                   jax.ShapeDtypeStruct((L // rb, ncol), I32),
                   jax.ShapeDtypeStruct((cap, ncol), I32)],
        scratch_shapes=[pltpu.VMEM((L, ncol), F32)],
        compiler_params=_cparams(("arbitrary",)),
        name="topk_slots",
    )(aff_cols, offs)


def _moe_ffn_kernel(idx_ref, idx_next_ref, h_hbm, slot_ref, gate_ref, wu_ref, wd_ref, yg_ref,
                    wu_s, wd_s, rows_a, rows_b, sems, *, n_slots):
    n_groups = pl.num_programs(1)
    step = pl.program_id(0) * n_groups + pl.program_id(1)
    last = pl.num_programs(0) * n_groups - 1
    rows = slot_ref.shape[3]

    def gather(ids_ref, buf, sem):
        for s in range(n_slots):
            pltpu.make_async_copy(h_hbm.at[pl.ds(ids_ref[0, 0, 0, s], 1)], buf.at[pl.ds(s, 1)], sem).start()

    def gathered(buf, sem):
        pltpu.make_async_copy(h_hbm.at[pl.ds(0, n_slots)], buf, sem).wait()

    @pl.when(step == 0)
    def _():
        gather(idx_ref, rows_a, sems.at[0])

    @pl.when(pl.program_id(1) == 0)
    def _():
        wu_s[...] = wu_ref[0, 0].astype(BF16)
        wd_s[...] = wd_ref[0, 0].astype(BF16)

    def ffn(cur, cur_sem, nxt, nxt_sem):
        gathered(cur, cur_sem)
        xg = cur[...].astype(BF16)
        gather(idx_next_ref, nxt, nxt_sem)
        hit = lax.broadcasted_iota(I32, (n_slots, rows), 0) == slot_ref[0, 0]
        gate_s = jnp.sum(jnp.where(hit, gate_ref[0, 0], 0.0), axis=1, keepdims=True)
        up = _dot(xg, wu_s[...])
        hid = (_silu(up[:, :D_EXPERT]) * up[:, D_EXPERT:]).astype(BF16)
        yg_ref[0, 0] = (_dot(hid, wd_s[...]) * gate_s).astype(BF16)

        @pl.when(step == last)
        def _():
            gathered(nxt, nxt_sem)

    @pl.when(step % 2 == 0)
    def _():
        ffn(rows_a, sems.at[0], rows_b, sems.at[1])

    @pl.when(step % 2 == 1)
    def _():
        ffn(rows_b, sems.at[1], rows_a, sems.at[0])


def _moe_ffn(idx_g, h_rows, slot_g, gate_g, w_up, w_down, layer, n_slots):
    ng, _, _, rows = slot_g.shape
    D = D_MODEL
    n_steps = N_EXPERTS * ng

    def next_ids(e, g):
        n = jnp.minimum(e * ng + g + 1, n_steps - 1)
        return (n % ng, n // ng, 0, 0)

    route = pl.BlockSpec((1, 1, 1, rows), lambda e, g: (g, e, 0, 0))
    return pl.pallas_call(
        functools.partial(_moe_ffn_kernel, n_slots=n_slots),
        grid=(N_EXPERTS, ng),
        in_specs=[pl.BlockSpec((1, 1, 1, n_slots), lambda e, g: (g, e, 0, 0), memory_space=pltpu.SMEM),
                  pl.BlockSpec((1, 1, 1, n_slots), next_ids, memory_space=pltpu.SMEM),
                  pl.BlockSpec(memory_space=pl.ANY),
                  route, route,
                  pl.BlockSpec((1, 1, D, 2 * D_EXPERT), lambda e, g: (layer, e, 0, 0)),
                  pl.BlockSpec((1, 1, D_EXPERT, D), lambda e, g: (layer, e, 0, 0))],
        out_specs=pl.BlockSpec((1, 1, n_slots, D), lambda e, g: (g, e, 0, 0)),
        out_shape=jax.ShapeDtypeStruct((ng, N_EXPERTS, n_slots, D), BF16),
        scratch_shapes=[pltpu.VMEM((D, 2 * D_EXPERT), BF16),
                        pltpu.VMEM((D_EXPERT, D), BF16),
                        pltpu.VMEM((n_slots, D), F32),
                        pltpu.VMEM((n_slots, D), F32),
                        pltpu.SemaphoreType.DMA((2,))],
        compiler_params=_cparams(("arbitrary", "arbitrary")),
        name="moe_ffn",
    )(idx_g, idx_g, h_rows, slot_g, gate_g, w_up, w_down)


def _moe_combine_kernel(win_ref, fits_ref, yg_ref, slot_ref, x_ref, g2_ref, gam_ref, o_ref, *, final):
    g = pl.program_id(0)
    c = pl.program_id(1)
    _, n_e, n_slots, D = yg_ref.shape
    chunk = x_ref.shape[1]
    slots = slot_ref[0]

    def finish(comb):
        xn = x_ref[0] + g2_ref[0] * comb
        o_ref[0] = _rms(xn, gam_ref[...]) if final else xn

    @pl.when(fits_ref[g, c] != 0)
    def _():
        lane = lax.broadcasted_iota(I32, (chunk, 2 * COMBINE_WIN), 1)
        pieces, windows = [], []
        for e in range(0, n_e, 2):
            rel = []
            for j in range(2):
                start = win_ref[g, c * n_e + e + j]
                r = slots[:, e + j:e + j + 1] - start
                inside = jnp.logical_and(r >= 0, r < COMBINE_WIN)
                rel.append(jnp.where(inside, r + j * COMBINE_WIN, -1))
                windows.append(yg_ref[0, e + j, pl.ds(pl.multiple_of(start, SLOT_ALIGN), COMBINE_WIN), :])
            hit = jnp.logical_or(lane == rel[0], lane == rel[1])
            pieces.append(hit.astype(F32).astype(BF16))
        finish(_dot(jnp.concatenate(pieces, axis=1), jnp.concatenate(windows, axis=0)))

    @pl.when(fits_ref[g, c] == 0)
    def _():
        lane = lax.broadcasted_iota(I32, (chunk, n_slots), 1)
        pieces = [(slots[:, e:e + 1] == lane).astype(F32).astype(BF16) for e in range(n_e)]
        finish(_dot(jnp.concatenate(pieces, axis=1), yg_ref[0].reshape(n_e * n_slots, D)))


def _moe_combine(win, fits, yg, slot_t, x_g, g2, gamma, L, final):
    ng, n_e, n_slots, D = yg.shape
    rows = x_g.shape[1]
    chunk = min(COMBINE_ROWS, L)
    row = pl.BlockSpec((1, chunk, D), lambda g, c, *_: (g, c, 0))
    return pl.pallas_call(
        functools.partial(_moe_combine_kernel, final=final),
        grid_spec=pltpu.PrefetchScalarGridSpec(
            num_scalar_prefetch=2,
            grid=(ng, rows // chunk),
            in_specs=[pl.BlockSpec((1, n_e, n_slots, D), lambda g, c, *_: (g, 0, 0, 0)),
                      pl.BlockSpec((1, chunk, LANES), lambda g, c, *_: (g, c, 0)),
                      row,
                      pl.BlockSpec((1, 1, D), lambda g, c, *_: ((g * rows + c * chunk) // L, 0, 0)),
                      pl.BlockSpec((1, D), lambda g, c, *_: (0, 0))],
            out_specs=row),
        out_shape=jax.ShapeDtypeStruct((ng, rows, D), F32),
        compiler_params=_cparams(("arbitrary", "arbitrary")),
        name="moe_combine_final" if final else "moe_combine",
    )(win, fits, yg, slot_t, x_g, g2, gamma)


def _combine_windows(first, ng, G, n_slots):
    nblk = first.shape[0]
    E = N_EXPERTS
    s = first.reshape(nblk, ng, G, E).transpose(1, 2, 0, 3).reshape(ng, G * nblk, E)
    s_next = jnp.concatenate([s[:, 1:], jnp.full((ng, 1, E), n_slots, I32)], axis=1)
    start = jnp.minimum((s // SLOT_ALIGN) * SLOT_ALIGN, n_slots - COMBINE_WIN)
    fits = jnp.all(s_next <= start + COMBINE_WIN, axis=2).astype(I32)
    return start.reshape(ng, G * nblk * E), fits


def _ec_moe(x, h, aff, w_up, w_down, layer, g2, gamma, final):
    B, L, D = x.shape
    E = N_EXPERTS
    cap = CAPACITY_FACTOR * L // E
    G = MOE_ROWS // L
    ng = B // G
    cols = aff[:, :, :E].transpose(1, 0, 2).reshape(L, B * E)
    offs = jnp.repeat((jnp.arange(B, dtype=I32) % G) * cap, E).reshape(1, B * E)
    slot, first, idx = _topk_slots(cols, offs, cap)

    def by_lane(a):
        return a.reshape(L, ng, G, E).transpose(1, 3, 2, 0).reshape(ng, E, 1, G * L)

    idx_g = idx.reshape(cap, ng, G, E).transpose(1, 3, 2, 0) + (jnp.arange(B, dtype=I32) * L).reshape(ng, 1, G, 1)
    yg = _moe_ffn(idx_g.reshape(ng, E, 1, G * cap), h.reshape(B * L, D),
                  by_lane(slot), by_lane(cols), w_up, w_down, layer, G * cap)
    slot_t = slot.reshape(L, ng, G, E).transpose(1, 2, 0, 3).reshape(ng, G * L, E)
    slot_t = jnp.pad(slot_t, ((0, 0), (0, 0), (0, LANES - E)), constant_values=-1)
    win, fits = _combine_windows(first, ng, G, G * cap)
    out = _moe_combine(win, fits, yg, slot_t, x.reshape(ng, G * L, D), g2, gamma, L, final)
    return out.reshape(B, L, D)


def _mla_in_kernel(*refs, need_q):
    (x_ref, gam_ref, sh_ref, sc_ref, win_ref, qn_ref, kvn_ref,
     wqa_ref, wqb_ref, wk_ref, wv_ref, vone_ref, cos_ref, sin_ref) = refs[:14]
    outs = refs[14:]
    h = _rms_mod(x_ref[0], gam_ref[...], sh_ref[0], sc_ref[0]).astype(BF16)
    z = _dot(h, win_ref[...])
    cos = cos_ref[...]
    sin = sin_ref[...]
    kv0 = Q_LORA + KV_LORA
    ckv = _rms(z[:, Q_LORA:kv0], kvn_ref[...]).astype(BF16)
    k_rope = z[:, kv0:kv0 + HEAD_PAD] * cos + z[:, kv0 + HEAD_PAD:kv0 + 2 * HEAD_PAD] * sin
    k_nope = _dot(ckv, wk_ref[...])
    if need_q:
        q_ref, k_ref, v_ref = outs
        cq = _rms(z[:, :Q_LORA], qn_ref[...]).astype(BF16)
        qa = _dot(cq, wqa_ref[...])
        qb = _dot(cq, wqb_ref[...])
    else:
        k_ref, v_ref = outs
    for hh in range(MLA_HEADS):
        sl = slice(hh * HEAD_PAD, (hh + 1) * HEAD_PAD)
        k_ref[0, :, sl] = (k_nope[:, sl] + k_rope).astype(BF16)
        if need_q:
            q_ref[0, :, sl] = (qa[:, sl] * cos + qb[:, sl] * sin).astype(BF16)
    v_ref[0] = (_dot(ckv, wv_ref[...]) + vone_ref[...]).astype(BF16)


def _mla_in(x, gamma, shift, scale, w, cos_t, sin_t, need_q, tm):
    B, L, D = x.shape
    row = pl.BlockSpec((1, tm, D), lambda b, i: (b, i, 0))
    mod = pl.BlockSpec((1, 1, D), lambda b, i: (b, 0, 0))

    def full(a):
        return pl.BlockSpec(a.shape, lambda b, i: (0,) * a.ndim)

    hp = MLA_HEADS * HEAD_PAD
    wide = pl.BlockSpec((1, tm, hp), lambda b, i: (b, i, 0))
    tab = pl.BlockSpec((tm, HEAD_PAD), lambda b, i: (i, 0))
    out_specs = [wide, wide]
    out_shape = [jax.ShapeDtypeStruct((B, L, hp), BF16)] * 2
    if need_q:
        out_specs = [wide] + out_specs
        out_shape = [jax.ShapeDtypeStruct((B, L, hp), BF16)] + out_shape
    weights = [w["win"], w["qn"], w["kvn"], w["wqa"], w["wqb"], w["wk"], w["wv"], w["vone"]]
    return pl.pallas_call(
        functools.partial(_mla_in_kernel, need_q=need_q),
        grid=(B, L // tm),
        in_specs=[row, full(gamma), mod, mod] + [full(a) for a in weights] + [tab, tab],
        out_specs=out_specs,
        out_shape=out_shape,
        compiler_params=_cparams(("arbitrary", "arbitrary")),
        name="mla_in_lat" if need_q else "mla_in_ctx",
    )(x, gamma, shift, scale, *weights, cos_t, sin_t)


def _attn_kernel(q_ref, kl_ref, kc_ref, vl_ref, vc_ref, o_ref):
    tq = q_ref.shape[1]
    blocks = [(j, r) for r in range(0, tq, ATTN_QB) for j in range(2)]

    def scores(j, r):
        sl = slice(j * HEAD_PAD, (j + 1) * HEAD_PAD)
        q = q_ref[0, r:r + ATTN_QB, sl]
        return _dot_nt(q, kl_ref[0, :, sl]), _dot_nt(q, kc_ref[0, :, sl])

    def weighted(j, s1, s2):
        sl = slice(j * HEAD_PAD, (j + 1) * HEAD_PAD)
        m = jnp.maximum(jnp.max(s1, axis=-1, keepdims=True), jnp.max(s2, axis=-1, keepdims=True))
        p1 = jnp.exp2((s1 - m) * EXP2_SCALE).astype(BF16)
        p2 = jnp.exp2((s2 - m) * EXP2_SCALE).astype(BF16)
        return _dot(p1, vl_ref[0, :, sl]) + _dot(p2, vc_ref[0, :, sl])

    pending = scores(*blocks[0])
    acc = {}
    for n, (j, r) in enumerate(blocks):
        s1, s2 = pending
        if n + 1 < len(blocks):
            pending = scores(*blocks[n + 1])
        acc[(j, r)] = weighted(j, s1, s2)
    lane = lax.broadcasted_iota(I32, (ATTN_QB, 2 * V_DIM), 1)
    for r in range(0, tq, ATTN_QB):
        even = acc[(0, r)]
        odd = acc[(1, r)]
        even = even / even[:, V_DIM:V_DIM + 1]
        odd = odd / odd[:, 0:1]
        o_ref[0, r:r + ATTN_QB, :] = jnp.where(lane < V_DIM, even, odd).astype(BF16)


def _attention(q, k_lat, k_ctx, v_lat, v_ctx, tq):
    B, L, _ = q.shape
    Lc = k_ctx.shape[1]
    pair = 2 * HEAD_PAD
    return pl.pallas_call(
        _attn_kernel,
        grid=(B, MLA_HEADS // 2, L // tq),
        in_specs=[pl.BlockSpec((1, tq, pair), lambda b, h, i: (b, i, h)),
                  pl.BlockSpec((1, L, pair), lambda b, h, i: (b, 0, h)),
                  pl.BlockSpec((1, Lc, pair), lambda b, h, i: (b, 0, h)),
                  pl.BlockSpec((1, L, pair), lambda b, h, i: (b, 0, h)),
                  pl.BlockSpec((1, Lc, pair), lambda b, h, i: (b, 0, h))],
        out_specs=pl.BlockSpec((1, tq, 2 * V_DIM), lambda b, h, i: (b, i, h)),
        out_shape=jax.ShapeDtypeStruct((B, L, MLA_HEADS * V_DIM), BF16),
        compiler_params=_cparams(("arbitrary", "arbitrary", "arbitrary")),
        name="attention",
    )(q, k_lat, k_ctx, v_lat, v_ctx)


def _rope_tables(rows):
    t = np.arange(rows * GRID_W)
    pos = np.stack([t // GRID_W, t % GRID_W], axis=-1).astype(np.float32)
    freq = jnp.asarray(ROPE_BASE, F32) ** (-jnp.arange(ROPE_FREQS, dtype=F32) / ROPE_FREQS)
    ang = jnp.asarray(pos)[:, :, None] * freq
    cos, sin = jnp.cos(ang), jnp.sin(ang)
    cos_r = jnp.stack([cos, cos], axis=2).reshape(-1, QK_ROPE)
    sin_r = jnp.stack([-sin, sin], axis=2).reshape(-1, QK_ROPE)
    T = cos_r.shape[0]
    pad = HEAD_PAD - QK_DIM
    cos_t = jnp.concatenate([jnp.ones((T, QK_NOPE), F32), cos_r, jnp.zeros((T, pad), F32)], axis=1)
    sin_t = jnp.concatenate([jnp.zeros((T, QK_NOPE), F32), sin_r, jnp.zeros((T, pad), F32)], axis=1)
    return cos_t, sin_t


def _rope_partner():
    p = np.arange(QK_ROPE)
    half = (p % (2 * ROPE_FREQS)) // ROPE_FREQS
    return np.where(half == 0, p + ROPE_FREQS, p - ROPE_FREQS)


def _mla_weights(w_in, q_norm, kv_norm, w_uq, w_ukv):
    partner = _rope_partner()
    pad = HEAD_PAD - QK_DIM
    kv0 = Q_LORA + KV_LORA
    kr = w_in[:, kv0:]
    zl = jnp.zeros((D_MODEL, QK_NOPE), F32)
    zr = jnp.zeros((D_MODEL, pad), F32)
    win = jnp.concatenate([w_in[:, :kv0], zl, kr, zr, zl, kr[:, partner], zr], axis=1)
    wq = w_uq.reshape(Q_LORA, MLA_HEADS, QK_DIM)
    zq = jnp.zeros((Q_LORA, MLA_HEADS, pad), F32)
    wqa = jnp.concatenate([wq, zq], axis=2)
    wqb = jnp.concatenate([jnp.zeros((Q_LORA, MLA_HEADS, QK_NOPE), F32),
                           wq[:, :, QK_NOPE:][:, :, partner], zq], axis=2)
    wkv = w_ukv.reshape(KV_LORA, MLA_HEADS, QK_NOPE + V_DIM)
    wk = jnp.concatenate([wkv[:, :, :QK_NOPE],
                          jnp.zeros((KV_LORA, MLA_HEADS, HEAD_PAD - QK_NOPE), F32)], axis=2)
    zv = jnp.zeros((KV_LORA, MLA_HEADS // 2, HEAD_PAD - V_DIM), F32)
    wv2 = wkv[:, :, QK_NOPE:].reshape(KV_LORA, MLA_HEADS // 2, 2, V_DIM)
    wv = jnp.stack([jnp.concatenate([wv2[:, :, 0], zv], axis=2),
                    jnp.concatenate([zv, wv2[:, :, 1]], axis=2)], axis=2)
    hp = MLA_HEADS * HEAD_PAD
    vone = np.zeros((MLA_HEADS // 2, 2, HEAD_PAD), np.float32)
    vone[:, 0, V_DIM] = 1.0
    vone[:, 1, 0] = 1.0
    return {"win": win.astype(BF16),
            "qn": q_norm.reshape(1, Q_LORA), "kvn": kv_norm.reshape(1, KV_LORA),
            "wqa": wqa.reshape(Q_LORA, hp).astype(BF16),
            "wqb": wqb.reshape(Q_LORA, hp).astype(BF16),
            "wk": wk.reshape(KV_LORA, hp).astype(BF16),
            "wv": wv.reshape(KV_LORA, hp).astype(BF16),
            "vone": jnp.asarray(vone.reshape(1, hp))}


def _pad_router(r):
    return jnp.pad(r, ((0, 0), (0, LANES - N_EXPERTS))).astype(BF16)


def kernel(x, c, ctx, c_ctx, ada_w, ada_b, norm_mix, norm_ffn, norm_final, lru_w_in, lru_conv_w, lru_conv_b, lru_gate_w, lru_gate_b, lru_lambda, lru_w_out, mla_w_in, mla_q_norm, mla_kv_norm, mla_w_uq, mla_w_ukv, mla_w_o, moe_router, moe_w_up, moe_w_down):
    B, T, D = x.shape
    Lc = ctx.shape[1]

    rows = ((B + 1 + 7) // 8) * 8
    cc = jnp.concatenate([c, c_ctx[None, :], jnp.zeros((rows - B - 1, D), F32)], axis=0)
    ada = _adaln(cc, ada_w, ada_b).reshape(DEPTH, rows, 6, D)

    def mods(i):
        lat = [ada[i, :B, k].reshape(B, 1, D) for k in range(6)]
        cx = [jnp.broadcast_to(ada[i, B, k].reshape(1, 1, D), (B, 1, D)) for k in range(6)]
        return lat, cx

    vec = lambda a: a.reshape(1, -1)

    (s1, sc1, g1, s2, sc2, g2), (s1c, sc1c, g1c, s2c, sc2c, g2c) = mods(0)
    w_in = lru_w_in[0].astype(BF16)
    cw, cb = 0.5 * lru_conv_w[0], vec(0.5 * lru_conv_b[0])
    g_lat, xc_lat = _lru_in(x, vec(norm_mix[0]), s1, sc1, w_in, cw, cb, tt=512)
    g_ctx, xc_ctx = _lru_in(ctx, vec(norm_mix[0]), s1c, sc1c, w_in, cw, cb, tt=Lc)
    ys = []
    for d in range(2):
        gw = lru_gate_w[0, d]
        wg = jnp.concatenate([gw[0], gw[1]], axis=-1).astype(BF16)
        ys.append(_lru_scan(xc_ctx, xc_lat, wg, 0.5 * lru_gate_b[0, d], vec(lru_lambda[0, d]),
                            reverse=(d == 1)))
    w_out = lru_w_out[0].astype(BF16)
    router = _pad_router(moe_router[0])
    gf = vec(norm_ffn[0])
    x_a, h_lat, aff_lat = _mix_out((ys[0][1], ys[1][1], g_lat), x, w_out, g1, gf, s2, sc2, router, tm=512)
    c_a, h_ctx, aff_ctx = _mix_out((ys[0][0], ys[1][0], g_ctx), ctx, w_out, g1c, gf, s2c, sc2c, router, tm=Lc)
    x_b = _ec_moe(x_a, h_lat, aff_lat, moe_w_up, moe_w_down, 0, g2, gf, final=False)
    c_b = _ec_moe(c_a, h_ctx, aff_ctx, moe_w_up, moe_w_down, 0, g2c, gf, final=False)

    (s1, sc1, g1, s2, sc2, g2), (s1c, sc1c, _, _, _, _) = mods(1)
    w = _mla_weights(mla_w_in[0], mla_q_norm[0], mla_kv_norm[0], mla_w_uq[0], mla_w_ukv[0])
    cos_t, sin_t = _rope_tables(T // GRID_W)
    one_t = jnp.concatenate([jnp.ones((Lc, QK_DIM), F32), jnp.zeros((Lc, HEAD_PAD - QK_DIM), F32)], axis=1)
    gm = vec(norm_mix[1])
    q, k_lat, v_lat = _mla_in(x_b, gm, s1, sc1, w, cos_t, sin_t, True, tm=512)
    k_ctx, v_ctx = _mla_in(c_b, gm, s1c, sc1c, w, one_t, jnp.zeros_like(one_t), False, tm=Lc)
    attn = _attention(q, k_lat, k_ctx, v_lat, v_ctx, tq=1024)
    x_c, h_lat, aff_lat = _mix_out((attn,), x_b, mla_w_o[0].astype(BF16), g1, vec(norm_ffn[1]),
                                   s2, sc2, _pad_router(moe_router[1]), tm=512)
    return _ec_moe(x_c, h_lat, aff_lat, moe_w_up, moe_w_down, 1, g2, vec(norm_final), final=True)
```

```python
import functools
import math

import numpy as np
import jax
import jax.numpy as jnp
from jax import lax
from jax.experimental import pallas as pl
from jax.experimental.pallas import tpu as pltpu

F32 = jnp.float32
BF16 = jnp.bfloat16
I32 = jnp.int32

D_MODEL = 1024
DEPTH = 2
GRID_W = 64
D_RNN = D_MODEL
CONV_W = 4
LRU_BLOCKS = 8
LRU_BLOCK = D_RNN // LRU_BLOCKS
LRU_C = 8.0
MLA_HEADS = 16
QK_NOPE = 64
QK_ROPE = 32
QK_DIM = QK_NOPE + QK_ROPE
V_DIM = 64
Q_LORA = 384
KV_LORA = 256
ROPE_FREQS = QK_ROPE // 4
ROPE_BASE = 10000.0
ATTN_SCALE = QK_DIM ** -0.5
N_EXPERTS = 16
D_EXPERT = 1024
CAPACITY_FACTOR = 2
EPS = 1e-6

LANES = 128
HEAD_PAD = LANES
VMEM_LIMIT = 56 * 1024 * 1024
NEG_BIG = -1e30
HALO = 8
MOE_ROWS = 2048
SCAN_TT = 64
COMBINE_ROWS = 256
COMBINE_WIN = 64
SLOT_ALIGN = 16
ATTN_QB = 256
EXP2_SCALE = ATTN_SCALE * math.log2(math.e)


def _cparams(sem):
    return pltpu.CompilerParams(dimension_semantics=sem, vmem_limit_bytes=VMEM_LIMIT)


def _sigmoid(x):
    return 0.5 * (jnp.tanh(0.5 * x) + 1.0)


def _silu(x):
    return x * _sigmoid(x)


def _gelu_tanh(x):
    return 0.5 * x * (1.0 + jnp.tanh(0.7978845608028654 * (x + 0.044715 * (x * x * x))))


def _rms(x, gamma):
    return x * lax.rsqrt(jnp.mean(x * x, axis=-1, keepdims=True) + EPS) * gamma


def _rms_mod(x, gamma, shift, scale):
    return _rms(x, gamma) * (1.0 + scale) + shift


def _dot(a, b):
    return jnp.dot(a, b, preferred_element_type=F32)


def _dot_nt(a, b):
    return lax.dot_general(a, b, (((1,), (1,)), ((), ())), preferred_element_type=F32)


def _adaln_kernel(c_ref, w_ref, b_ref, o_ref):
    s = _silu(c_ref[...]).astype(BF16)
    o_ref[0] = _dot(s, w_ref[0].astype(BF16)) + b_ref[0]


def _adaln(cc, ada_w, ada_b):
    rows = cc.shape[0]
    tn = 1024
    return pl.pallas_call(
        _adaln_kernel,
        grid=(DEPTH, 6 * D_MODEL // tn),
        in_specs=[pl.BlockSpec((rows, D_MODEL), lambda l, j: (0, 0)),
                  pl.BlockSpec((1, D_MODEL, tn), lambda l, j: (l, 0, j)),
                  pl.BlockSpec((1, 1, tn), lambda l, j: (l, 0, j))],
        out_specs=pl.BlockSpec((1, rows, tn), lambda l, j: (l, 0, j)),
        out_shape=jax.ShapeDtypeStruct((DEPTH, rows, 6 * D_MODEL), F32),
        compiler_params=_cparams(("arbitrary", "arbitrary")),
        name="adaln",
    )(cc, ada_w, ada_b.reshape(DEPTH, 1, 6 * D_MODEL))


def _lru_in_kernel(xp_ref, x_ref, xn_ref, gam_ref, sh_ref, sc_ref, w_ref, cw_ref, cb_ref,
                   g_ref, xc_ref, *, tt):
    i = pl.program_id(1)
    n = pl.num_programs(1)
    xe = jnp.concatenate([xp_ref[0], x_ref[0], xn_ref[0]], axis=0)
    he = _rms_mod(xe, gam_ref[...], sh_ref[0], sc_ref[0]).astype(BF16)
    z = _dot(he, w_ref[...])
    g_ref[0] = _gelu_tanh(z[HALO:HALO + tt, :D_RNN]).astype(BF16)
    z2 = z[:, D_RNN:]
    row = lax.broadcasted_iota(I32, (tt + 2 * HALO, 1), 0)
    valid = jnp.logical_and(jnp.logical_or(row >= HALO, i > 0),
                            jnp.logical_or(row < tt + HALO, i < n - 1))
    z2 = jnp.where(valid, z2, 0.0)
    cw = cw_ref[...]
    base = HALO - CONV_W // 2
    acc = z2[base:base + tt] * cw[0:1]
    for k in range(1, CONV_W):
        acc = acc + z2[base + k:base + k + tt] * cw[k:k + 1]
    xc_ref[0] = (acc + cb_ref[...]).astype(BF16)


def _lru_in(x, gamma, shift, scale, w_in, conv_w, conv_b, tt):
    B, L, D = x.shape
    nt = L // tt
    hb = tt // HALO
    last = L // HALO - 1
    vec = pl.BlockSpec((1, D), lambda b, i: (0, 0))
    mod = pl.BlockSpec((1, 1, D), lambda b, i: (b, 0, 0))
    out = pl.BlockSpec((1, tt, D_RNN), lambda b, i: (b, i, 0))
    return pl.pallas_call(
        functools.partial(_lru_in_kernel, tt=tt),
        grid=(B, nt),
        in_specs=[pl.BlockSpec((1, HALO, D), lambda b, i: (b, jnp.maximum(i * hb - 1, 0), 0)),
                  pl.BlockSpec((1, tt, D), lambda b, i: (b, i, 0)),
                  pl.BlockSpec((1, HALO, D), lambda b, i: (b, jnp.minimum((i + 1) * hb, last), 0)),
                  vec, mod, mod,
                  pl.BlockSpec((D, 2 * D_RNN), lambda b, i: (0, 0)),
                  pl.BlockSpec((CONV_W, D_RNN), lambda b, i: (0, 0)),
                  pl.BlockSpec((1, D_RNN), lambda b, i: (0, 0))],
        out_specs=[out, out],
        out_shape=[jax.ShapeDtypeStruct((B, L, D_RNN), BF16)] * 2,
        compiler_params=_cparams(("arbitrary", "arbitrary")),
        name="lru_in",
    )(x, x, x, gamma, shift, scale, w_in, conv_w, conv_b)


def _scan_kernel(xc_ctx_ref, xc_lat_ref, wg_ref, bg_ref, lam_ref, y_ctx_ref, y_lat_ref,
                 a_s, u_s, y_s, h_s, *, reverse, n_ctx):
    i = pl.program_id(0)
    nb, tt, _ = a_s.shape

    @pl.when(i == 0)
    def _():
        h_s[...] = jnp.zeros_like(h_s)

    def tile(x_ref, y_ref):
        x2 = x_ref[...].reshape(nb * tt, D_RNN)
        nlam = -lam_ref[...]
        sp = jnp.maximum(nlam, 0.0) + jnp.log1p(jnp.exp(-jnp.abs(nlam)))
        k2 = sp * (-0.5 * LRU_C * math.log2(math.e))
        for n in range(LRU_BLOCKS):
            sl = slice(n * LRU_BLOCK, (n + 1) * LRU_BLOCK)
            xb = x2[:, sl]
            g = _dot(xb, wg_ref[n])
            tr = jnp.tanh(g[:, :LRU_BLOCK] + bg_ref[0:1, sl])
            ti = jnp.tanh(g[:, LRU_BLOCK:] + bg_ref[1:2, sl])
            a = jnp.exp2(k2[:, sl] * tr + k2[:, sl])
            v = 1.0 - a * a
            root = jnp.where(v > 0.0, v * lax.rsqrt(v), 0.0)
            u = root * ((ti + 1.0) * xb.astype(F32))
            a_s[:, :, sl] = a.reshape(nb, tt, LRU_BLOCK)
            u_s[:, :, sl] = u.reshape(nb, tt, LRU_BLOCK)
        h = h_s[...]
        steps = range(tt - 1, -1, -1) if reverse else range(tt)
        for t in steps:
            h = a_s[:, t, :] * h + u_s[:, t, :]
            y_s[:, t, :] = h
        h_s[...] = h
        y_ref[...] = y_s[...].astype(BF16)

    @pl.when(i < n_ctx)
    def _():
        tile(xc_ctx_ref, y_ctx_ref)

    @pl.when(i >= n_ctx)
    def _():
        tile(xc_lat_ref, y_lat_ref)


def _lru_scan(xc_ctx, xc_lat, wg, bg, lam, reverse):
    B, Lc, _ = xc_ctx.shape
    L = xc_lat.shape[1]
    tt = SCAN_TT
    n_ctx = Lc // tt
    n_lat = L // tt

    def ctx_map(i):
        j = jnp.minimum(i, n_ctx - 1)
        return (0, (n_ctx - 1 - j) if reverse else j, 0)

    def lat_map(i):
        j = jnp.maximum(i - n_ctx, 0)
        return (0, (n_lat - 1 - j) if reverse else j, 0)

    return pl.pallas_call(
        functools.partial(_scan_kernel, reverse=reverse, n_ctx=n_ctx),
        grid=(n_ctx + n_lat,),
        in_specs=[pl.BlockSpec((B, tt, D_RNN), ctx_map),
                  pl.BlockSpec((B, tt, D_RNN), lat_map),
                  pl.BlockSpec((LRU_BLOCKS, LRU_BLOCK, 2 * LRU_BLOCK), lambda i: (0, 0, 0)),
                  pl.BlockSpec((2, D_RNN), lambda i: (0, 0)),
                  pl.BlockSpec((1, D_RNN), lambda i: (0, 0))],
        out_specs=[pl.BlockSpec((B, tt, D_RNN), ctx_map),
                   pl.BlockSpec((B, tt, D_RNN), lat_map)],
        out_shape=[jax.ShapeDtypeStruct((B, Lc, D_RNN), BF16),
                   jax.ShapeDtypeStruct((B, L, D_RNN), BF16)],
        scratch_shapes=[pltpu.VMEM((B, tt, D_RNN), F32),
                        pltpu.VMEM((B, tt, D_RNN), F32),
                        pltpu.VMEM((B, tt, D_RNN), F32),
                        pltpu.VMEM((B, D_RNN), F32)],
        compiler_params=_cparams(("arbitrary",)),
        name="lru_scan_rev" if reverse else "lru_scan_fwd",
    )(xc_ctx, xc_lat, wg, bg, lam)


def _mix_out_kernel(*refs, lru):
    if lru:
        (yf_ref, yb_ref, g_ref, x_ref, w_ref, g1_ref, gam_ref, sh_ref, sc_ref, r_ref,
         xo_ref, h_ref, aff_ref) = refs
        y = yf_ref[0].astype(F32) + yb_ref[0].astype(F32)
        lhs = (y * g_ref[0].astype(F32)).astype(BF16)
    else:
        (a_ref, x_ref, w_ref, g1_ref, gam_ref, sh_ref, sc_ref, r_ref,
         xo_ref, h_ref, aff_ref) = refs
        lhs = a_ref[0]
    xn = x_ref[0] + g1_ref[0] * _dot(lhs, w_ref[...])
    xo_ref[0] = xn
    hf = _rms_mod(xn, gam_ref[...], sh_ref[0], sc_ref[0])
    h_ref[0] = hf
    logits = _dot(hf.astype(BF16), r_ref[...])
    lane = lax.broadcasted_iota(I32, logits.shape, 1)
    logits = jnp.where(lane < N_EXPERTS, logits, NEG_BIG)
    p = jnp.exp(logits - jnp.max(logits, axis=-1, keepdims=True))
    aff_ref[0] = p / jnp.sum(p, axis=-1, keepdims=True)


def _mix_out(acts, x, w_out, g1, gamma, shift, scale, router_pad, tm):
    B, L, D = x.shape
    row = pl.BlockSpec((1, tm, D), lambda b, i: (b, i, 0))
    vec = pl.BlockSpec((1, D), lambda b, i: (0, 0))
    mod = pl.BlockSpec((1, 1, D), lambda b, i: (b, 0, 0))
    return pl.pallas_call(
        functools.partial(_mix_out_kernel, lru=len(acts) == 3),
        grid=(B, L // tm),
        in_specs=[row] * len(acts) + [
            row,
            pl.BlockSpec((D, D), lambda b, i: (0, 0)),
            mod, vec, mod, mod,
            pl.BlockSpec((D, LANES), lambda b, i: (0, 0))],
        out_specs=[row, row,
                   pl.BlockSpec((1, tm, LANES), lambda b, i: (b, i, 0))],
        out_shape=[jax.ShapeDtypeStruct((B, L, D), F32),
                   jax.ShapeDtypeStruct((B, L, D), F32),
                   jax.ShapeDtypeStruct((B, L, LANES), F32)],
        compiler_params=_cparams(("arbitrary", "arbitrary")),
        name="mix_out_lru" if len(acts) == 3 else "mix_out_attn",
    )(*acts, x, w_out, g1, gamma, shift, scale, router_pad)


def _topk_kernel(a_ref, offs_ref, slot_ref, first_ref, idx_ref, cum_s, *, cap, rb):
    L, ncol = a_ref.shape
    a = a_ref[...]

    def as_f32(bits):
        return lax.bitcast_convert_type(bits, F32)

    def search(k, thr):
        cand = thr | lax.shift_left(jnp.int32(1), 30 - k)
        cnt = jnp.sum((a >= as_f32(cand)).astype(I32), axis=0, keepdims=True)
        return jnp.where(cnt >= cap, cand, thr)

    thr = lax.fori_loop(0, 31, search, jnp.zeros((1, ncol), I32))
    lo = as_f32(thr)
    hi = as_f32(thr + 1)
    need = (cap - jnp.sum((a >= hi).astype(I32), axis=0, keepdims=True)).astype(F32)
    eq_b = jnp.logical_and(a >= lo, a < hi).astype(F32).astype(BF16)
    col = lax.broadcasted_iota(I32, (rb, L), 1)
    rowi = lax.broadcasted_iota(I32, (rb, L), 0)

    def before(r0):
        return (col < rowi + r0).astype(F32).astype(BF16)

    sel_blocks = []
    for r0 in range(0, L, rb):
        blk = a[r0:r0 + rb]
        eq_rank = _dot(before(r0), eq_b)
        take = jnp.logical_or(blk >= hi, jnp.logical_and(blk >= lo, eq_rank < need))
        sel_blocks.append(take.astype(F32))
    sel = jnp.concatenate(sel_blocks, axis=0) if len(sel_blocks) > 1 else sel_blocks[0]
    sel_b = sel.astype(BF16)
    offs = offs_ref[...]
    for k, r0 in enumerate(range(0, L, rb)):
        ahead = _dot(before(r0), sel_b)
        pos = ahead.astype(I32) + offs
        slot_ref[r0:r0 + rb, :] = jnp.where(sel_blocks[k] > 0.0, pos, -1)
        first_ref[k:k + 1, :] = pos[0:1, :]
        cum_s[r0:r0 + rb, :] = ahead + sel_blocks[k]

    def rows_of_slots(i, carry):
        base = i * 8
        cum = cum_s[...]
        out = [jnp.sum((cum <= jnp.asarray(base + j, F32)).astype(I32), axis=0, keepdims=True)
               for j in range(8)]
        idx_ref[pl.ds(pl.multiple_of(base, 8), 8), :] = jnp.concatenate(out, axis=0)
        return carry

    lax.fori_loop(0, cap // 8, rows_of_slots, 0)


def _topk_slots(aff_cols, offs, cap):
    L, ncol = aff_cols.shape
    rb = min(L, COMBINE_ROWS)
    return pl.pallas_call(
        functools.partial(_topk_kernel, cap=cap, rb=rb),
        grid=(1,),
        in_specs=[pl.BlockSpec((L, ncol), lambda i: (0, 0)),
                  pl.BlockSpec((1, ncol), lambda i: (0, 0))],
        out_specs=[pl.BlockSpec((L, ncol), lambda i: (0, 0)),
                   pl.BlockSpec((L // rb, ncol), lambda i: (0, 0)),
                   pl.BlockSpec((cap, ncol), lambda i: (0, 0))],
        out_shape=[jax.ShapeDtypeStruct((L, ncol), I32),
                   jax.ShapeDtypeStruct((L // rb, ncol), I32),
                   jax.ShapeDtypeStruct((cap, ncol), I32)],
        scratch_shapes=[pltpu.VMEM((L, ncol), F32)],
        compiler_params=_cparams(("arbitrary",)),
        name="topk_slots",
    )(aff_cols, offs)


def _moe_ffn_kernel(idx_ref, idx_next_ref, h_hbm, slot_ref, gate_ref, wu_ref, wd_ref, yg_ref,
                    wu_s, wd_s, rows_a, rows_b, sems, *, n_slots):
    n_groups = pl.num_programs(1)
    step = pl.program_id(0) * n_groups + pl.program_id(1)
    last = pl.num_programs(0) * n_groups - 1
    rows = slot_ref.shape[3]

    def gather(ids_ref, buf, sem):
        for s in range(n_slots):
            pltpu.make_async_copy(h_hbm.at[pl.ds(ids_ref[0, 0, 0, s], 1)], buf.at[pl.ds(s, 1)],
                                  sem).start(priority=s % 2)

    def gathered(buf, sem):
        pltpu.make_async_copy(h_hbm.at[pl.ds(0, n_slots)], buf, sem).wait()

    @pl.when(step == 0)
    def _():
        gather(idx_ref, rows_a, sems.at[0])

    @pl.when(pl.program_id(1) == 0)
    def _():
        wu_s[...] = wu_ref[0, 0].astype(BF16)
        wd_s[...] = wd_ref[0, 0].astype(BF16)

    def ffn(cur, cur_sem, nxt, nxt_sem):
        gathered(cur, cur_sem)
        xg = cur[...].astype(BF16)
        gather(idx_next_ref, nxt, nxt_sem)
        hit = lax.broadcasted_iota(I32, (n_slots, rows), 0) == slot_ref[0, 0]
        gate_s = jnp.sum(jnp.where(hit, gate_ref[0, 0], 0.0), axis=1, keepdims=True)
        up = _dot(xg, wu_s[...])
        hid = (_silu(up[:, :D_EXPERT]) * up[:, D_EXPERT:]).astype(BF16)
        yg_ref[0, 0] = (_dot(hid, wd_s[...]) * gate_s).astype(BF16)

        @pl.when(step == last)
        def _():
            gathered(nxt, nxt_sem)

    @pl.when(step % 2 == 0)
    def _():
        ffn(rows_a, sems.at[0], rows_b, sems.at[1])

    @pl.when(step % 2 == 1)
    def _():
        ffn(rows_b, sems.at[1], rows_a, sems.at[0])


def _moe_ffn(idx_g, h_rows, slot_g, gate_g, w_up, w_down, layer, n_slots):
    ng, _, _, rows = slot_g.shape
    D = D_MODEL
    n_steps = N_EXPERTS * ng

    def next_ids(e, g):
        n = jnp.minimum(e * ng + g + 1, n_steps - 1)
        return (n % ng, n // ng, 0, 0)

    route = pl.BlockSpec((1, 1, 1, rows), lambda e, g: (g, e, 0, 0))
    return pl.pallas_call(
        functools.partial(_moe_ffn_kernel, n_slots=n_slots),
        grid=(N_EXPERTS, ng),
        in_specs=[pl.BlockSpec((1, 1, 1, n_slots), lambda e, g: (g, e, 0, 0), memory_space=pltpu.SMEM),
                  pl.BlockSpec((1, 1, 1, n_slots), next_ids, memory_space=pltpu.SMEM),
                  pl.BlockSpec(memory_space=pl.ANY),
                  route, route,
                  pl.BlockSpec((1, 1, D, 2 * D_EXPERT), lambda e, g: (layer, e, 0, 0)),
                  pl.BlockSpec((1, 1, D_EXPERT, D), lambda e, g: (layer, e, 0, 0))],
        out_specs=pl.BlockSpec((1, 1, n_slots, D), lambda e, g: (g, e, 0, 0)),
        out_shape=jax.ShapeDtypeStruct((ng, N_EXPERTS, n_slots, D), BF16),
        scratch_shapes=[pltpu.VMEM((D, 2 * D_EXPERT), BF16),
                        pltpu.VMEM((D_EXPERT, D), BF16),
                        pltpu.VMEM((n_slots, D), F32),
                        pltpu.VMEM((n_slots, D), F32),
                        pltpu.SemaphoreType.DMA((2,))],
        compiler_params=_cparams(("arbitrary", "arbitrary")),
        name="moe_ffn",
    )(idx_g, idx_g, h_rows, slot_g, gate_g, w_up, w_down)


def _moe_combine_kernel(win_ref, fits_ref, yg_ref, slot_ref, x_ref, g2_ref, gam_ref, o_ref, *, final):
    g = pl.program_id(0)
    c = pl.program_id(1)
    _, n_e, n_slots, D = yg_ref.shape
    chunk = x_ref.shape[1]
    slots = slot_ref[0]

    def finish(comb):
        xn = x_ref[0] + g2_ref[0] * comb
        o_ref[0] = _rms(xn, gam_ref[...]) if final else xn

    @pl.when(fits_ref[g, c] != 0)
    def _():
        lane = lax.broadcasted_iota(I32, (chunk, 2 * COMBINE_WIN), 1)
        pieces, windows = [], []
        for e in range(0, n_e, 2):
            rel = []
            for j in range(2):
                start = win_ref[g, c * n_e + e + j]
                r = slots[:, e + j:e + j + 1] - start
                inside = jnp.logical_and(r >= 0, r < COMBINE_WIN)
                rel.append(jnp.where(inside, r + j * COMBINE_WIN, -1))
                windows.append(yg_ref[0, e + j, pl.ds(pl.multiple_of(start, SLOT_ALIGN), COMBINE_WIN), :])
            hit = jnp.logical_or(lane == rel[0], lane == rel[1])
            pieces.append(hit.astype(F32).astype(BF16))
        finish(_dot(jnp.concatenate(pieces, axis=1), jnp.concatenate(windows, axis=0)))

    @pl.when(fits_ref[g, c] == 0)
    def _():
        lane = lax.broadcasted_iota(I32, (chunk, n_slots), 1)
        pieces = [(slots[:, e:e + 1] == lane).astype(F32).astype(BF16) for e in range(n_e)]
        finish(_dot(jnp.concatenate(pieces, axis=1), yg_ref[0].reshape(n_e * n_slots, D)))


def _moe_combine(win, fits, yg, slot_t, x_g, g2, gamma, L, final):
    ng, n_e, n_slots, D = yg.shape
    rows = x_g.shape[1]
    chunk = min(COMBINE_ROWS, L)
    row = pl.BlockSpec((1, chunk, D), lambda g, c, *_: (g, c, 0))
    return pl.pallas_call(
        functools.partial(_moe_combine_kernel, final=final),
        grid_spec=pltpu.PrefetchScalarGridSpec(
            num_scalar_prefetch=2,
            grid=(ng, rows // chunk),
            in_specs=[pl.BlockSpec((1, n_e, n_slots, D), lambda g, c, *_: (g, 0, 0, 0)),
                      pl.BlockSpec((1, chunk, LANES), lambda g, c, *_: (g, c, 0)),
                      row,
                      pl.BlockSpec((1, 1, D), lambda g, c, *_: ((g * rows + c * chunk) // L, 0, 0)),
                      pl.BlockSpec((1, D), lambda g, c, *_: (0, 0))],
            out_specs=row),
        out_shape=jax.ShapeDtypeStruct((ng, rows, D), F32),
        compiler_params=_cparams(("arbitrary", "arbitrary")),
        name="moe_combine_final" if final else "moe_combine",
    )(win, fits, yg, slot_t, x_g, g2, gamma)


def _combine_windows(first, ng, G, n_slots):
    nblk = first.shape[0]
    E = N_EXPERTS
    s = first.reshape(nblk, ng, G, E).transpose(1, 2, 0, 3).reshape(ng, G * nblk, E)
    s_next = jnp.concatenate([s[:, 1:], jnp.full((ng, 1, E), n_slots, I32)], axis=1)
    start = jnp.minimum((s // SLOT_ALIGN) * SLOT_ALIGN, n_slots - COMBINE_WIN)
    fits = jnp.all(s_next <= start + COMBINE_WIN, axis=2).astype(I32)
    return start.reshape(ng, G * nblk * E), fits


def _ec_moe(x, h, aff, w_up, w_down, layer, g2, gamma, final):
    B, L, D = x.shape
    E = N_EXPERTS
    cap = CAPACITY_FACTOR * L // E
    G = MOE_ROWS // L
    ng = B // G
    cols = aff[:, :, :E].transpose(1, 0, 2).reshape(L, B * E)
    offs = jnp.repeat((jnp.arange(B, dtype=I32) % G) * cap, E).reshape(1, B * E)
    slot, first, idx = _topk_slots(cols, offs, cap)

    def by_lane(a):
        return a.reshape(L, ng, G, E).transpose(1, 3, 2, 0).reshape(ng, E, 1, G * L)

    idx_g = idx.reshape(cap, ng, G, E).transpose(1, 3, 2, 0) + (jnp.arange(B, dtype=I32) * L).reshape(ng, 1, G, 1)
    yg = _moe_ffn(idx_g.reshape(ng, E, 1, G * cap), h.reshape(B * L, D),
                  by_lane(slot), by_lane(cols), w_up, w_down, layer, G * cap)
    slot_t = slot.reshape(L, ng, G, E).transpose(1, 2, 0, 3).reshape(ng, G * L, E)
    slot_t = jnp.pad(slot_t, ((0, 0), (0, 0), (0, LANES - E)), constant_values=-1)
    win, fits = _combine_windows(first, ng, G, G * cap)
    out = _moe_combine(win, fits, yg, slot_t, x.reshape(ng, G * L, D), g2, gamma, L, final)
    return out.reshape(B, L, D)


def _mla_in_kernel(*refs, need_q):
    (x_ref, gam_ref, sh_ref, sc_ref, win_ref, qn_ref, kvn_ref,
     wqa_ref, wqb_ref, wk_ref, wv_ref, vone_ref, cos_ref, sin_ref) = refs[:14]
    outs = refs[14:]
    h = _rms_mod(x_ref[0], gam_ref[...], sh_ref[0], sc_ref[0]).astype(BF16)
    z = _dot(h, win_ref[...])
    cos = cos_ref[...]
    sin = sin_ref[...]
    kv0 = Q_LORA + KV_LORA
    ckv = _rms(z[:, Q_LORA:kv0], kvn_ref[...]).astype(BF16)
    k_rope = z[:, kv0:kv0 + HEAD_PAD] * cos + z[:, kv0 + HEAD_PAD:kv0 + 2 * HEAD_PAD] * sin
    k_nope = _dot(ckv, wk_ref[...])
    if need_q:
        q_ref, k_ref, v_ref = outs
        cq = _rms(z[:, :Q_LORA], qn_ref[...]).astype(BF16)
        qa = _dot(cq, wqa_ref[...])
        qb = _dot(cq, wqb_ref[...])
    else:
        k_ref, v_ref = outs
    for hh in range(MLA_HEADS):
        sl = slice(hh * HEAD_PAD, (hh + 1) * HEAD_PAD)
        k_ref[0, :, sl] = (k_nope[:, sl] + k_rope).astype(BF16)
        if need_q:
            q_ref[0, :, sl] = (qa[:, sl] * cos + qb[:, sl] * sin).astype(BF16)
    v_ref[0] = (_dot(ckv, wv_ref[...]) + vone_ref[...]).astype(BF16)


def _mla_in(x, gamma, shift, scale, w, cos_t, sin_t, need_q, tm):
    B, L, D = x.shape
    row = pl.BlockSpec((1, tm, D), lambda b, i: (b, i, 0))
    mod = pl.BlockSpec((1, 1, D), lambda b, i: (b, 0, 0))

    def full(a):
        return pl.BlockSpec(a.shape, lambda b, i: (0,) * a.ndim)

    hp = MLA_HEADS * HEAD_PAD
    wide = pl.BlockSpec((1, tm, hp), lambda b, i: (b, i, 0))
    tab = pl.BlockSpec((tm, HEAD_PAD), lambda b, i: (i, 0))
    out_specs = [wide, wide]
    out_shape = [jax.ShapeDtypeStruct((B, L, hp), BF16)] * 2
    if need_q:
        out_specs = [wide] + out_specs
        out_shape = [jax.ShapeDtypeStruct((B, L, hp), BF16)] + out_shape
    weights = [w["win"], w["qn"], w["kvn"], w["wqa"], w["wqb"], w["wk"], w["wv"], w["vone"]]
    return pl.pallas_call(
        functools.partial(_mla_in_kernel, need_q=need_q),
        grid=(B, L // tm),
        in_specs=[row, full(gamma), mod, mod] + [full(a) for a in weights] + [tab, tab],
        out_specs=out_specs,
        out_shape=out_shape,
        compiler_params=_cparams(("arbitrary", "arbitrary")),
        name="mla_in_lat" if need_q else "mla_in_ctx",
    )(x, gamma, shift, scale, *weights, cos_t, sin_t)


def _attn_kernel(q_ref, kl_ref, kc_ref, vl_ref, vc_ref, o_ref):
    tq = q_ref.shape[1]
    blocks = [(j, r) for r in range(0, tq, ATTN_QB) for j in range(2)]

    def scores(j, r):
        sl = slice(j * HEAD_PAD, (j + 1) * HEAD_PAD)
        q = q_ref[0, r:r + ATTN_QB, sl]
        return _dot_nt(q, kl_ref[0, :, sl]), _dot_nt(q, kc_ref[0, :, sl])

    def weighted(j, s1, s2):
        sl = slice(j * HEAD_PAD, (j + 1) * HEAD_PAD)
        m = jnp.maximum(jnp.max(s1, axis=-1, keepdims=True), jnp.max(s2, axis=-1, keepdims=True))
        p1 = jnp.exp2((s1 - m) * EXP2_SCALE).astype(BF16)
        p2 = jnp.exp2((s2 - m) * EXP2_SCALE).astype(BF16)
        return _dot(p1, vl_ref[0, :, sl]) + _dot(p2, vc_ref[0, :, sl])

    pending = scores(*blocks[0])
    acc = {}
    for n, (j, r) in enumerate(blocks):
        s1, s2 = pending
        if n + 1 < len(blocks):
            pending = scores(*blocks[n + 1])
        acc[(j, r)] = weighted(j, s1, s2)
    lane = lax.broadcasted_iota(I32, (ATTN_QB, 2 * V_DIM), 1)
    for r in range(0, tq, ATTN_QB):
        even = acc[(0, r)]
        odd = acc[(1, r)]
        even = even / even[:, V_DIM:V_DIM + 1]
        odd = odd / odd[:, 0:1]
        o_ref[0, r:r + ATTN_QB, :] = jnp.where(lane < V_DIM, even, odd).astype(BF16)


def _attention(q, k_lat, k_ctx, v_lat, v_ctx, tq):
    B, L, _ = q.shape
    Lc = k_ctx.shape[1]
    pair = 2 * HEAD_PAD
    return pl.pallas_call(
        _attn_kernel,
        grid=(B, MLA_HEADS // 2, L // tq),
        in_specs=[pl.BlockSpec((1, tq, pair), lambda b, h, i: (b, i, h)),
                  pl.BlockSpec((1, L, pair), lambda b, h, i: (b, 0, h)),
                  pl.BlockSpec((1, Lc, pair), lambda b, h, i: (b, 0, h)),
                  pl.BlockSpec((1, L, pair), lambda b, h, i: (b, 0, h)),
                  pl.BlockSpec((1, Lc, pair), lambda b, h, i: (b, 0, h))],
        out_specs=pl.BlockSpec((1, tq, 2 * V_DIM), lambda b, h, i: (b, i, h)),
        out_shape=jax.ShapeDtypeStruct((B, L, MLA_HEADS * V_DIM), BF16),
        compiler_params=_cparams(("arbitrary", "arbitrary", "arbitrary")),
        name="attention",
    )(q, k_lat, k_ctx, v_lat, v_ctx)


def _rope_tables(rows):
    t = np.arange(rows * GRID_W)
    pos = np.stack([t // GRID_W, t % GRID_W], axis=-1).astype(np.float32)
    freq = jnp.asarray(ROPE_BASE, F32) ** (-jnp.arange(ROPE_FREQS, dtype=F32) / ROPE_FREQS)
    ang = jnp.asarray(pos)[:, :, None] * freq
    cos, sin = jnp.cos(ang), jnp.sin(ang)
    cos_r = jnp.stack([cos, cos], axis=2).reshape(-1, QK_ROPE)
    sin_r = jnp.stack([-sin, sin], axis=2).reshape(-1, QK_ROPE)
    T = cos_r.shape[0]
    pad = HEAD_PAD - QK_DIM
    cos_t = jnp.concatenate([jnp.ones((T, QK_NOPE), F32), cos_r, jnp.zeros((T, pad), F32)], axis=1)
    sin_t = jnp.concatenate([jnp.zeros((T, QK_NOPE), F32), sin_r, jnp.zeros((T, pad), F32)], axis=1)
    return cos_t, sin_t


def _rope_partner():
    p = np.arange(QK_ROPE)
    half = (p % (2 * ROPE_FREQS)) // ROPE_FREQS
    return np.where(half == 0, p + ROPE_FREQS, p - ROPE_FREQS)


def _mla_weights(w_in, q_norm, kv_norm, w_uq, w_ukv):
    partner = _rope_partner()
    pad = HEAD_PAD - QK_DIM
    kv0 = Q_LORA + KV_LORA
    kr = w_in[:, kv0:]
    zl = jnp.zeros((D_MODEL, QK_NOPE), F32)
    zr = jnp.zeros((D_MODEL, pad), F32)
    win = jnp.concatenate([w_in[:, :kv0], zl, kr, zr, zl, kr[:, partner], zr], axis=1)
    wq = w_uq.reshape(Q_LORA, MLA_HEADS, QK_DIM)
    zq = jnp.zeros((Q_LORA, MLA_HEADS, pad), F32)
    wqa = jnp.concatenate([wq, zq], axis=2)
    wqb = jnp.concatenate([jnp.zeros((Q_LORA, MLA_HEADS, QK_NOPE), F32),
                           wq[:, :, QK_NOPE:][:, :, partner], zq], axis=2)
    wkv = w_ukv.reshape(KV_LORA, MLA_HEADS, QK_NOPE + V_DIM)
    wk = jnp.concatenate([wkv[:, :, :QK_NOPE],
                          jnp.zeros((KV_LORA, MLA_HEADS, HEAD_PAD - QK_NOPE), F32)], axis=2)
    zv = jnp.zeros((KV_LORA, MLA_HEADS // 2, HEAD_PAD - V_DIM), F32)
    wv2 = wkv[:, :, QK_NOPE:].reshape(KV_LORA, MLA_HEADS // 2, 2, V_DIM)
    wv = jnp.stack([jnp.concatenate([wv2[:, :, 0], zv], axis=2),
                    jnp.concatenate([zv, wv2[:, :, 1]], axis=2)], axis=2)
    hp = MLA_HEADS * HEAD_PAD
    vone = np.zeros((MLA_HEADS // 2, 2, HEAD_PAD), np.float32)
    vone[:, 0, V_DIM] = 1.0
    vone[:, 1, 0] = 1.0
    return {"win": win.astype(BF16),
            "qn": q_norm.reshape(1, Q_LORA), "kvn": kv_norm.reshape(1, KV_LORA),
            "wqa": wqa.reshape(Q_LORA, hp).astype(BF16),
            "wqb": wqb.reshape(Q_LORA, hp).astype(BF16),
            "wk": wk.reshape(KV_LORA, hp).astype(BF16),
            "wv": wv.reshape(KV_LORA, hp).astype(BF16),
            "vone": jnp.asarray(vone.reshape(1, hp))}


def _pad_router(r):
    return jnp.pad(r, ((0, 0), (0, LANES - N_EXPERTS))).astype(BF16)


def kernel(x, c, ctx, c_ctx, ada_w, ada_b, norm_mix, norm_ffn, norm_final, lru_w_in, lru_conv_w, lru_conv_b, lru_gate_w, lru_gate_b, lru_lambda, lru_w_out, mla_w_in, mla_q_norm, mla_kv_norm, mla_w_uq, mla_w_ukv, mla_w_o, moe_router, moe_w_up, moe_w_down):
    B, T, D = x.shape
    Lc = ctx.shape[1]

    rows = ((B + 1 + 7) // 8) * 8
    cc = jnp.concatenate([c, c_ctx[None, :], jnp.zeros((rows - B - 1, D), F32)], axis=0)
    ada = _adaln(cc, ada_w, ada_b).reshape(DEPTH, rows, 6, D)

    def mods(i):
        lat = [ada[i, :B, k].reshape(B, 1, D) for k in range(6)]
        cx = [jnp.broadcast_to(ada[i, B, k].reshape(1, 1, D), (B, 1, D)) for k in range(6)]
        return lat, cx

    vec = lambda a: a.reshape(1, -1)

    (s1, sc1, g1, s2, sc2, g2), (s1c, sc1c, g1c, s2c, sc2c, g2c) = mods(0)
    w_in = lru_w_in[0].astype(BF16)
    cw, cb = 0.5 * lru_conv_w[0], vec(0.5 * lru_conv_b[0])
    g_lat, xc_lat = _lru_in(x, vec(norm_mix[0]), s1, sc1, w_in, cw, cb, tt=512)
    g_ctx, xc_ctx = _lru_in(ctx, vec(norm_mix[0]), s1c, sc1c, w_in, cw, cb, tt=Lc)
    ys = []
    for d in range(2):
        gw = lru_gate_w[0, d]
        wg = jnp.concatenate([gw[0], gw[1]], axis=-1).astype(BF16)
        ys.append(_lru_scan(xc_ctx, xc_lat, wg, 0.5 * lru_gate_b[0, d], vec(lru_lambda[0, d]),
                            reverse=(d == 1)))
    w_out = lru_w_out[0].astype(BF16)
    router = _pad_router(moe_router[0])
    gf = vec(norm_ffn[0])
    x_a, h_lat, aff_lat = _mix_out((ys[0][1], ys[1][1], g_lat), x, w_out, g1, gf, s2, sc2, router, tm=512)
    c_a, h_ctx, aff_ctx = _mix_out((ys[0][0], ys[1][0], g_ctx), ctx, w_out, g1c, gf, s2c, sc2c, router, tm=Lc)
    x_b = _ec_moe(x_a, h_lat, aff_lat, moe_w_up, moe_w_down, 0, g2, gf, final=False)
    c_b = _ec_moe(c_a, h_ctx, aff_ctx, moe_w_up, moe_w_down, 0, g2c, gf, final=False)

    (s1, sc1, g1, s2, sc2, g2), (s1c, sc1c, _, _, _, _) = mods(1)
    w = _mla_weights(mla_w_in[0], mla_q_norm[0], mla_kv_norm[0], mla_w_uq[0], mla_w_ukv[0])
    cos_t, sin_t = _rope_tables(T // GRID_W)
    one_t = jnp.concatenate([jnp.ones((Lc, QK_DIM), F32), jnp.zeros((Lc, HEAD_PAD - QK_DIM), F32)], axis=1)
    gm = vec(norm_mix[1])
    q, k_lat, v_lat = _mla_in(x_b, gm, s1, sc1, w, cos_t, sin_t, True, tm=512)
    k_ctx, v_ctx = _mla_in(c_b, gm, s1c, sc1c, w, one_t, jnp.zeros_like(one_t), False, tm=Lc)
    attn = _attention(q, k_lat, k_ctx, v_lat, v_ctx, tq=1024)
    x_c, h_lat, aff_lat = _mix_out((attn,), x_b, mla_w_o[0].astype(BF16), g1, vec(norm_ffn[1]),
                                   s2, sc2, _pad_router(moe_router[1]), tm=512)
    return _ec_moe(x_c, h_lat, aff_lat, moe_w_up, moe_w_down, 1, g2, vec(norm_final), final=True)
```

```python
import functools
import math

import numpy as np
import jax
import jax.numpy as jnp
from jax import lax
from jax.experimental import pallas as pl
from jax.experimental.pallas import tpu as pltpu

F32 = jnp.float32
BF16 = jnp.bfloat16
I32 = jnp.int32

D_MODEL = 1024
DEPTH = 2
GRID_W = 64
D_RNN = D_MODEL
CONV_W = 4
LRU_BLOCKS = 8
LRU_BLOCK = D_RNN // LRU_BLOCKS
LRU_C = 8.0
MLA_HEADS = 16
QK_NOPE = 64
QK_ROPE = 32
QK_DIM = QK_NOPE + QK_ROPE
V_DIM = 64
Q_LORA = 384
KV_LORA = 256
ROPE_FREQS = QK_ROPE // 4
ROPE_BASE = 10000.0
ATTN_SCALE = QK_DIM ** -0.5
N_EXPERTS = 16
D_EXPERT = 1024
CAPACITY_FACTOR = 2
EPS = 1e-6

LANES = 128
HEAD_PAD = LANES
VMEM_LIMIT = 56 * 1024 * 1024
NEG_BIG = -1e30
HALO = 8
MOE_ROWS = 2048
SCAN_TT = 64
COMBINE_ROWS = 256
COMBINE_WIN = 64
SLOT_ALIGN = 16
ATTN_QB = 256
EXP2_SCALE = ATTN_SCALE * math.log2(math.e)


def _cparams(sem):
    return pltpu.CompilerParams(dimension_semantics=sem, vmem_limit_bytes=VMEM_LIMIT)


def _sigmoid(x):
    return 0.5 * (jnp.tanh(0.5 * x) + 1.0)


def _silu(x):
    return x * _sigmoid(x)


def _gelu_tanh(x):
    return 0.5 * x * (1.0 + jnp.tanh(0.7978845608028654 * (x + 0.044715 * (x * x * x))))


def _rms(x, gamma):
    return x * lax.rsqrt(jnp.mean(x * x, axis=-1, keepdims=True) + EPS) * gamma


def _rms_mod(x, gamma, shift, scale):
    return _rms(x, gamma) * (1.0 + scale) + shift


def _dot(a, b):
    return jnp.dot(a, b, preferred_element_type=F32)


def _dot_nt(a, b):
    return lax.dot_general(a, b, (((1,), (1,)), ((), ())), preferred_element_type=F32)


def _adaln_kernel(c_ref, w_ref, b_ref, o_ref):
    s = _silu(c_ref[...]).astype(BF16)
    o_ref[0] = _dot(s, w_ref[0].astype(BF16)) + b_ref[0]


def _adaln(cc, ada_w, ada_b):
    rows = cc.shape[0]
    tn = 1024
    return pl.pallas_call(
        _adaln_kernel,
        grid=(DEPTH, 6 * D_MODEL // tn),
        in_specs=[pl.BlockSpec((rows, D_MODEL), lambda l, j: (0, 0)),
                  pl.BlockSpec((1, D_MODEL, tn), lambda l, j: (l, 0, j)),
                  pl.BlockSpec((1, 1, tn), lambda l, j: (l, 0, j))],
        out_specs=pl.BlockSpec((1, rows, tn), lambda l, j: (l, 0, j)),
        out_shape=jax.ShapeDtypeStruct((DEPTH, rows, 6 * D_MODEL), F32),
        compiler_params=_cparams(("arbitrary", "arbitrary")),
        name="adaln",
    )(cc, ada_w, ada_b.reshape(DEPTH, 1, 6 * D_MODEL))


def _lru_in_kernel(xp_ref, x_ref, xn_ref, gam_ref, sh_ref, sc_ref, w_ref, cw_ref, cb_ref,
                   g_ref, xc_ref, *, tt):
    i = pl.program_id(1)
    n = pl.num_programs(1)
    xe = jnp.concatenate([xp_ref[0], x_ref[0], xn_ref[0]], axis=0)
    he = _rms_mod(xe, gam_ref[...], sh_ref[0], sc_ref[0]).astype(BF16)
    z = _dot(he, w_ref[...])
    g_ref[0] = _gelu_tanh(z[HALO:HALO + tt, :D_RNN]).astype(BF16)
    z2 = z[:, D_RNN:]
    row = lax.broadcasted_iota(I32, (tt + 2 * HALO, 1), 0)
    valid = jnp.logical_and(jnp.logical_or(row >= HALO, i > 0),
                            jnp.logical_or(row < tt + HALO, i < n - 1))
    z2 = jnp.where(valid, z2, 0.0)
    cw = cw_ref[...]
    base = HALO - CONV_W // 2
    acc = z2[base:base + tt] * cw[0:1]
    for k in range(1, CONV_W):
        acc = acc + z2[base + k:base + k + tt] * cw[k:k + 1]
    xc_ref[0] = (acc + cb_ref[...]).astype(BF16)


def _lru_in(x, gamma, shift, scale, w_in, conv_w, conv_b, tt):
    B, L, D = x.shape
    nt = L // tt
    hb = tt // HALO
    last = L // HALO - 1
    vec = pl.BlockSpec((1, D), lambda b, i: (0, 0))
    mod = pl.BlockSpec((1, 1, D), lambda b, i: (b, 0, 0))
    out = pl.BlockSpec((1, tt, D_RNN), lambda b, i: (b, i, 0))
    return pl.pallas_call(
        functools.partial(_lru_in_kernel, tt=tt),
        grid=(B, nt),
        in_specs=[pl.BlockSpec((1, HALO, D), lambda b, i: (b, jnp.maximum(i * hb - 1, 0), 0)),
                  pl.BlockSpec((1, tt, D), lambda b, i: (b, i, 0)),
                  pl.BlockSpec((1, HALO, D), lambda b, i: (b, jnp.minimum((i + 1) * hb, last), 0)),
                  vec, mod, mod,
                  pl.BlockSpec((D, 2 * D_RNN), lambda b, i: (0, 0)),
                  pl.BlockSpec((CONV_W, D_RNN), lambda b, i: (0, 0)),
                  pl.BlockSpec((1, D_RNN), lambda b, i: (0, 0))],
        out_specs=[out, out],
        out_shape=[jax.ShapeDtypeStruct((B, L, D_RNN), BF16)] * 2,
        compiler_params=_cparams(("arbitrary", "arbitrary")),
        name="lru_in",
    )(x, x, x, gamma, shift, scale, w_in, conv_w, conv_b)


def _scan_kernel(xc_ctx_ref, xc_lat_ref, wg_ref, bg_ref, lam_ref, y_ctx_ref, y_lat_ref,
                 a_s, u_s, y_s, h_s, *, reverse, n_ctx):
    i = pl.program_id(0)
    nb, tt, _ = a_s.shape

    @pl.when(i == 0)
    def _():
        h_s[...] = jnp.zeros_like(h_s)

    def tile(x_ref, y_ref):
        x2 = x_ref[...].reshape(nb * tt, D_RNN)
        nlam = -lam_ref[...]
        sp = jnp.maximum(nlam, 0.0) + jnp.log1p(jnp.exp(-jnp.abs(nlam)))
        k2 = sp * (-0.5 * LRU_C * math.log2(math.e))
        for n in range(LRU_BLOCKS):
            sl = slice(n * LRU_BLOCK, (n + 1) * LRU_BLOCK)
            xb = x2[:, sl]
            g = _dot(xb, wg_ref[n])
            tr = jnp.tanh(g[:, :LRU_BLOCK] + bg_ref[0:1, sl])
            ti = jnp.tanh(g[:, LRU_BLOCK:] + bg_ref[1:2, sl])
            a = jnp.exp2(k2[:, sl] * tr + k2[:, sl])
            v = 1.0 - a * a
            root = jnp.where(v > 0.0, v * lax.rsqrt(v), 0.0)
            u = root * ((ti + 1.0) * xb.astype(F32))
            a_s[:, :, sl] = a.reshape(nb, tt, LRU_BLOCK)
            u_s[:, :, sl] = u.reshape(nb, tt, LRU_BLOCK)
        h = h_s[...]
        steps = range(tt - 1, -1, -1) if reverse else range(tt)
        for t in steps:
            h = a_s[:, t, :] * h + u_s[:, t, :]
            y_s[:, t, :] = h
        h_s[...] = h
        y_ref[...] = y_s[...].astype(BF16)

    @pl.when(i < n_ctx)
    def _():
        tile(xc_ctx_ref, y_ctx_ref)

    @pl.when(i >= n_ctx)
    def _():
        tile(xc_lat_ref, y_lat_ref)


def _lru_scan(xc_ctx, xc_lat, wg, bg, lam, reverse):
    B, Lc, _ = xc_ctx.shape
    L = xc_lat.shape[1]
    tt = SCAN_TT
    n_ctx = Lc // tt
    n_lat = L // tt

    def ctx_map(i):
        j = jnp.minimum(i, n_ctx - 1)
        return (0, (n_ctx - 1 - j) if reverse else j, 0)

    def lat_map(i):
        j = jnp.maximum(i - n_ctx, 0)
        return (0, (n_lat - 1 - j) if reverse else j, 0)

    return pl.pallas_call(
        functools.partial(_scan_kernel, reverse=reverse, n_ctx=n_ctx),
        grid=(n_ctx + n_lat,),
        in_specs=[pl.BlockSpec((B, tt, D_RNN), ctx_map),
                  pl.BlockSpec((B, tt, D_RNN), lat_map),
                  pl.BlockSpec((LRU_BLOCKS, LRU_BLOCK, 2 * LRU_BLOCK), lambda i: (0, 0, 0)),
                  pl.BlockSpec((2, D_RNN), lambda i: (0, 0)),
                  pl.BlockSpec((1, D_RNN), lambda i: (0, 0))],
        out_specs=[pl.BlockSpec((B, tt, D_RNN), ctx_map),
                   pl.BlockSpec((B, tt, D_RNN), lat_map)],
        out_shape=[jax.ShapeDtypeStruct((B, Lc, D_RNN), BF16),
                   jax.ShapeDtypeStruct((B, L, D_RNN), BF16)],
        scratch_shapes=[pltpu.VMEM((B, tt, D_RNN), F32),
                        pltpu.VMEM((B, tt, D_RNN), F32),
                        pltpu.VMEM((B, tt, D_RNN), F32),
                        pltpu.VMEM((B, D_RNN), F32)],
        compiler_params=_cparams(("arbitrary",)),
        name="lru_scan_rev" if reverse else "lru_scan_fwd",
    )(xc_ctx, xc_lat, wg, bg, lam)


def _mix_out_kernel(*refs, lru):
    if lru:
        (yf_ref, yb_ref, g_ref, x_ref, w_ref, g1_ref, gam_ref, sh_ref, sc_ref, r_ref,
         xo_ref, h_ref, aff_ref) = refs
        y = yf_ref[0].astype(F32) + yb_ref[0].astype(F32)
        lhs = (y * g_ref[0].astype(F32)).astype(BF16)
    else:
        (a_ref, x_ref, w_ref, g1_ref, gam_ref, sh_ref, sc_ref, r_ref,
         xo_ref, h_ref, aff_ref) = refs
        lhs = a_ref[0]
    xn = x_ref[0] + g1_ref[0] * _dot(lhs, w_ref[...])
    xo_ref[0] = xn
    hf = _rms_mod(xn, gam_ref[...], sh_ref[0], sc_ref[0])
    h_ref[0] = hf
    logits = _dot(hf.astype(BF16), r_ref[...])
    lane = lax.broadcasted_iota(I32, logits.shape, 1)
    logits = jnp.where(lane < N_EXPERTS, logits, NEG_BIG)
    p = jnp.exp(logits - jnp.max(logits, axis=-1, keepdims=True))
    aff_ref[0] = p / jnp.sum(p, axis=-1, keepdims=True)


def _mix_out(acts, x, w_out, g1, gamma, shift, scale, router_pad, tm):
    B, L, D = x.shape
    row = pl.BlockSpec((1, tm, D), lambda b, i: (b, i, 0))
    vec = pl.BlockSpec((1, D), lambda b, i: (0, 0))
    mod = pl.BlockSpec((1, 1, D), lambda b, i: (b, 0, 0))
    return pl.pallas_call(
        functools.partial(_mix_out_kernel, lru=len(acts) == 3),
        grid=(B, L // tm),
        in_specs=[row] * len(acts) + [
            row,
            pl.BlockSpec((D, D), lambda b, i: (0, 0)),
            mod, vec, mod, mod,
            pl.BlockSpec((D, LANES), lambda b, i: (0, 0))],
        out_specs=[row, row,
                   pl.BlockSpec((1, tm, LANES), lambda b, i: (b, i, 0))],
        out_shape=[jax.ShapeDtypeStruct((B, L, D), F32),
                   jax.ShapeDtypeStruct((B, L, D), F32),
                   jax.ShapeDtypeStruct((B, L, LANES), F32)],
        compiler_params=_cparams(("arbitrary", "arbitrary")),
        name="mix_out_lru" if len(acts) == 3 else "mix_out_attn",
    )(*acts, x, w_out, g1, gamma, shift, scale, router_pad)


def _topk_kernel(a_ref, offs_ref, slot_ref, first_ref, idx_ref, cum_s, *, cap, rb):
    L, ncol = a_ref.shape
    a = a_ref[...]

    def as_f32(bits):
        return lax.bitcast_convert_type(bits, F32)

    def search(k, thr):
        cand = thr | lax.shift_left(jnp.int32(1), 30 - k)
        cnt = jnp.sum((a >= as_f32(cand)).astype(I32), axis=0, keepdims=True)
        return jnp.where(cnt >= cap, cand, thr)

    thr = lax.fori_loop(0, 31, search, jnp.zeros((1, ncol), I32))
    lo = as_f32(thr)
    hi = as_f32(thr + 1)
    need = (cap - jnp.sum((a >= hi).astype(I32), axis=0, keepdims=True)).astype(F32)
    eq_b = jnp.logical_and(a >= lo, a < hi).astype(F32).astype(BF16)
    col = lax.broadcasted_iota(I32, (rb, L), 1)
    rowi = lax.broadcasted_iota(I32, (rb, L), 0)

    def before(r0):
        return (col < rowi + r0).astype(F32).astype(BF16)

    sel_blocks = []
    for r0 in range(0, L, rb):
        blk = a[r0:r0 + rb]
        eq_rank = _dot(before(r0), eq_b)
        take = jnp.logical_or(blk >= hi, jnp.logical_and(blk >= lo, eq_rank < need))
        sel_blocks.append(take.astype(F32))
    sel = jnp.concatenate(sel_blocks, axis=0) if len(sel_blocks) > 1 else sel_blocks[0]
    sel_b = sel.astype(BF16)
    offs = offs_ref[...]
    for k, r0 in enumerate(range(0, L, rb)):
        ahead = _dot(before(r0), sel_b)
        pos = ahead.astype(I32) + offs
        slot_ref[r0:r0 + rb, :] = jnp.where(sel_blocks[k] > 0.0, pos, -1)
        first_ref[k:k + 1, :] = pos[0:1, :]
        cum_s[r0:r0 + rb, :] = ahead + sel_blocks[k]

    def rows_of_slots(i, carry):
        base = i * 8
        cum = cum_s[...]
        out = [jnp.sum((cum <= jnp.asarray(base + j, F32)).astype(I32), axis=0, keepdims=True)
               for j in range(8)]
        idx_ref[pl.ds(pl.multiple_of(base, 8), 8), :] = jnp.concatenate(out, axis=0)
        return carry

    lax.fori_loop(0, cap // 8, rows_of_slots, 0)


def _topk_slots(aff_cols, offs, cap):
    L, ncol = aff_cols.shape
    rb = min(L, COMBINE_ROWS)
    return pl.pallas_call(
        functools.partial(_topk_kernel, cap=cap, rb=rb),
        grid=(1,),
        in_specs=[pl.BlockSpec((L, ncol), lambda i: (0, 0)),
                  pl.BlockSpec((1, ncol), lambda i: (0, 0))],
        out_specs=[pl.BlockSpec((L, ncol), lambda i: (0, 0)),
                   pl.BlockSpec((L // rb, ncol), lambda i: (0, 0)),
                   pl.BlockSpec((cap, ncol), lambda i: (0, 0))],
        out_shape=[jax.ShapeDtypeStruct((L, ncol), I32),
                   jax.ShapeDtypeStruct((L // rb, ncol), I32),
                   jax.ShapeDtypeStruct((cap, ncol), I32)],
        scratch_shapes=[pltpu.VMEM((L, ncol), F32)],
        compiler_params=_cparams(("arbitrary",)),
        name="topk_slots",
    )(aff_cols, offs)


def _moe_ffn_kernel(idx_ref, idx_next_ref, h_hbm, slot_ref, gate_ref, wu_ref, wd_ref, yg_ref,
                    wu_s, wd_s, rows_a, rows_b, sems, *, n_slots):
    n_groups = pl.num_programs(1)
    step = pl.program_id(0) * n_groups + pl.program_id(1)
    last = pl.num_programs(0) * n_groups - 1
    rows = slot_ref.shape[3]

    def gather(ids_ref, buf, sem):
        for s in range(n_slots):
            pltpu.make_async_copy(h_hbm.at[pl.ds(ids_ref[0, 0, 0, s], 1)], buf.at[pl.ds(s, 1)],
                                  sem).start(priority=s % 2)

    def gathered(buf, sem):
        pltpu.make_async_copy(h_hbm.at[pl.ds(0, n_slots)], buf, sem).wait()

    @pl.when(step == 0)
    def _():
        gather(idx_ref, rows_a, sems.at[0])

    @pl.when(pl.program_id(1) == 0)
    def _():
        wu_s[...] = wu_ref[0, 0].astype(BF16)
        wd_s[...] = wd_ref[0, 0].astype(BF16)

    def ffn(cur, cur_sem, nxt, nxt_sem):
        gathered(cur, cur_sem)
        xg = cur[...].astype(BF16)
        gather(idx_next_ref, nxt, nxt_sem)
        hit = lax.broadcasted_iota(I32, (n_slots, rows), 0) == slot_ref[0, 0]
        gate_s = jnp.sum(jnp.where(hit, gate_ref[0, 0], 0.0), axis=1, keepdims=True)
        up = _dot(xg, wu_s[...])
        hid = (_silu(up[:, :D_EXPERT]) * up[:, D_EXPERT:]).astype(BF16)
        yg_ref[0, 0] = (_dot(hid, wd_s[...]) * gate_s).astype(BF16)

        @pl.when(step == last)
        def _():
            gathered(nxt, nxt_sem)

    @pl.when(step % 2 == 0)
    def _():
        ffn(rows_a, sems.at[0], rows_b, sems.at[1])

    @pl.when(step % 2 == 1)
    def _():
        ffn(rows_b, sems.at[1], rows_a, sems.at[0])


def _moe_ffn(idx_g, h_rows, slot_g, gate_g, w_up, w_down, layer, n_slots):
    ng, _, _, rows = slot_g.shape
    D = D_MODEL
    n_steps = N_EXPERTS * ng

    def next_ids(e, g):
        n = jnp.minimum(e * ng + g + 1, n_steps - 1)
        return (n % ng, n // ng, 0, 0)

    route = pl.BlockSpec((1, 1, 1, rows), lambda e, g: (g, e, 0, 0))
    return pl.pallas_call(
        functools.partial(_moe_ffn_kernel, n_slots=n_slots),
        grid=(N_EXPERTS, ng),
        in_specs=[pl.BlockSpec((1, 1, 1, n_slots), lambda e, g: (g, e, 0, 0), memory_space=pltpu.SMEM),
                  pl.BlockSpec((1, 1, 1, n_slots), next_ids, memory_space=pltpu.SMEM),
                  pl.BlockSpec(memory_space=pl.ANY),
                  route, route,
                  pl.BlockSpec((1, 1, D, 2 * D_EXPERT), lambda e, g: (layer, e, 0, 0)),
                  pl.BlockSpec((1, 1, D_EXPERT, D), lambda e, g: (layer, e, 0, 0))],
        out_specs=pl.BlockSpec((1, 1, n_slots, D), lambda e, g: (g, e, 0, 0)),
        out_shape=jax.ShapeDtypeStruct((ng, N_EXPERTS, n_slots, D), BF16),
        scratch_shapes=[pltpu.VMEM((D, 2 * D_EXPERT), BF16),
                        pltpu.VMEM((D_EXPERT, D), BF16),
                        pltpu.VMEM((n_slots, D), F32),
                        pltpu.VMEM((n_slots, D), F32),
                        pltpu.SemaphoreType.DMA((2,))],
        compiler_params=_cparams(("arbitrary", "arbitrary")),
        name="moe_ffn",
    )(idx_g, idx_g, h_rows, slot_g, gate_g, w_up, w_down)


def _moe_combine_kernel(win_ref, fits_ref, yg_ref, slot_ref, x_ref, g2_ref, gam_ref, o_ref, *, final):
    g = pl.program_id(0)
    _, n_e, n_slots, D = yg_ref.shape
    n_sub = x_ref.shape[1] // COMBINE_ROWS

    def combine_chunk(c, rs):
        slots = slot_ref[0, rs, :]

        def finish(comb):
            xn = x_ref[0, rs, :] + g2_ref[0] * comb
            o_ref[0, rs, :] = _rms(xn, gam_ref[...]) if final else xn

        @pl.when(fits_ref[g, c] != 0)
        def _():
            lane = lax.broadcasted_iota(I32, (COMBINE_ROWS, 2 * COMBINE_WIN), 1)
            pieces, windows = [], []
            for e in range(0, n_e, 2):
                rel = []
                for j in range(2):
                    start = win_ref[g, c * n_e + e + j]
                    r = slots[:, e + j:e + j + 1] - start
                    inside = jnp.logical_and(r >= 0, r < COMBINE_WIN)
                    rel.append(jnp.where(inside, r + j * COMBINE_WIN, -1))
                    windows.append(yg_ref[0, e + j, pl.ds(pl.multiple_of(start, SLOT_ALIGN), COMBINE_WIN), :])
                hit = jnp.logical_or(lane == rel[0], lane == rel[1])
                pieces.append(hit.astype(F32).astype(BF16))
            finish(_dot(jnp.concatenate(pieces, axis=1), jnp.concatenate(windows, axis=0)))

        @pl.when(fits_ref[g, c] == 0)
        def _():
            lane = lax.broadcasted_iota(I32, (COMBINE_ROWS, n_slots), 1)
            pieces = [(slots[:, e:e + 1] == lane).astype(F32).astype(BF16) for e in range(n_e)]
            finish(_dot(jnp.concatenate(pieces, axis=1), yg_ref[0].reshape(n_e * n_slots, D)))

    for sub in range(n_sub):
        combine_chunk(pl.program_id(1) * n_sub + sub, slice(sub * COMBINE_ROWS, (sub + 1) * COMBINE_ROWS))


def _moe_combine(win, fits, yg, slot_t, x_g, g2, gamma, L, final):
    ng, n_e, n_slots, D = yg.shape
    rows = x_g.shape[1]
    chunk = min(2 * COMBINE_ROWS, L)
    row = pl.BlockSpec((1, chunk, D), lambda g, c, *_: (g, c, 0))
    return pl.pallas_call(
        functools.partial(_moe_combine_kernel, final=final),
        grid_spec=pltpu.PrefetchScalarGridSpec(
            num_scalar_prefetch=2,
            grid=(ng, rows // chunk),
            in_specs=[pl.BlockSpec((1, n_e, n_slots, D), lambda g, c, *_: (g, 0, 0, 0)),
                      pl.BlockSpec((1, chunk, LANES), lambda g, c, *_: (g, c, 0)),
                      row,
                      pl.BlockSpec((1, 1, D), lambda g, c, *_: ((g * rows + c * chunk) // L, 0, 0)),
                      pl.BlockSpec((1, D), lambda g, c, *_: (0, 0))],
            out_specs=row),
        out_shape=jax.ShapeDtypeStruct((ng, rows, D), F32),
        compiler_params=_cparams(("arbitrary", "arbitrary")),
        name="moe_combine_final" if final else "moe_combine",
    )(win, fits, yg, slot_t, x_g, g2, gamma)


def _combine_windows(first, ng, G, n_slots):
    nblk = first.shape[0]
    E = N_EXPERTS
    s = first.reshape(nblk, ng, G, E).transpose(1, 2, 0, 3).reshape(ng, G * nblk, E)
    s_next = jnp.concatenate([s[:, 1:], jnp.full((ng, 1, E), n_slots, I32)], axis=1)
    start = jnp.minimum((s // SLOT_ALIGN) * SLOT_ALIGN, n_slots - COMBINE_WIN)
    fits = jnp.all(s_next <= start + COMBINE_WIN, axis=2).astype(I32)
    return start.reshape(ng, G * nblk * E), fits


def _ec_moe(x, h, aff, w_up, w_down, layer, g2, gamma, final):
    B, L, D = x.shape
    E = N_EXPERTS
    cap = CAPACITY_FACTOR * L // E
    G = MOE_ROWS // L
    ng = B // G
    cols = aff[:, :, :E].transpose(1, 0, 2).reshape(L, B * E)
    offs = jnp.repeat((jnp.arange(B, dtype=I32) % G) * cap, E).reshape(1, B * E)
    slot, first, idx = _topk_slots(cols, offs, cap)

    def by_lane(a):
        return a.reshape(L, ng, G, E).transpose(1, 3, 2, 0).reshape(ng, E, 1, G * L)

    idx_g = idx.reshape(cap, ng, G, E).transpose(1, 3, 2, 0) + (jnp.arange(B, dtype=I32) * L).reshape(ng, 1, G, 1)
    yg = _moe_ffn(idx_g.reshape(ng, E, 1, G * cap), h.reshape(B * L, D),
                  by_lane(slot), by_lane(cols), w_up, w_down, layer, G * cap)
    slot_t = slot.reshape(L, ng, G, E).transpose(1, 2, 0, 3).reshape(ng, G * L, E)
    slot_t = jnp.pad(slot_t, ((0, 0), (0, 0), (0, LANES - E)), constant_values=-1)
    win, fits = _combine_windows(first, ng, G, G * cap)
    out = _moe_combine(win, fits, yg, slot_t, x.reshape(ng, G * L, D), g2, gamma, L, final)
    return out.reshape(B, L, D)


def _mla_in_kernel(*refs, need_q):
    (x_ref, gam_ref, sh_ref, sc_ref, win_ref, qn_ref, kvn_ref,
     wqa_ref, wqb_ref, wk_ref, wv_ref, vone_ref, cos_ref, sin_ref) = refs[:14]
    outs = refs[14:]
    h = _rms_mod(x_ref[0], gam_ref[...], sh_ref[0], sc_ref[0]).astype(BF16)
    z = _dot(h, win_ref[...])
    cos = cos_ref[...]
    sin = sin_ref[...]
    kv0 = Q_LORA + KV_LORA
    ckv = _rms(z[:, Q_LORA:kv0], kvn_ref[...]).astype(BF16)
    k_rope = z[:, kv0:kv0 + HEAD_PAD] * cos + z[:, kv0 + HEAD_PAD:kv0 + 2 * HEAD_PAD] * sin
    k_nope = _dot(ckv, wk_ref[...])
    if need_q:
        q_ref, k_ref, v_ref = outs
        cq = _rms(z[:, :Q_LORA], qn_ref[...]).astype(BF16)
        qa = _dot(cq, wqa_ref[...])
        qb = _dot(cq, wqb_ref[...])
    else:
        k_ref, v_ref = outs
    for hh in range(MLA_HEADS):
        sl = slice(hh * HEAD_PAD, (hh + 1) * HEAD_PAD)
        k_ref[0, :, sl] = (k_nope[:, sl] + k_rope).astype(BF16)
        if need_q:
            q_ref[0, :, sl] = (qa[:, sl] * cos + qb[:, sl] * sin).astype(BF16)
    v_ref[0] = (_dot(ckv, wv_ref[...]) + vone_ref[...]).astype(BF16)


def _mla_in(x, gamma, shift, scale, w, cos_t, sin_t, need_q, tm):
    B, L, D = x.shape
    row = pl.BlockSpec((1, tm, D), lambda b, i: (b, i, 0))
    mod = pl.BlockSpec((1, 1, D), lambda b, i: (b, 0, 0))

    def full(a):
        return pl.BlockSpec(a.shape, lambda b, i: (0,) * a.ndim)

    hp = MLA_HEADS * HEAD_PAD
    wide = pl.BlockSpec((1, tm, hp), lambda b, i: (b, i, 0))
    tab = pl.BlockSpec((tm, HEAD_PAD), lambda b, i: (i, 0))
    out_specs = [wide, wide]
    out_shape = [jax.ShapeDtypeStruct((B, L, hp), BF16)] * 2
    if need_q:
        out_specs = [wide] + out_specs
        out_shape = [jax.ShapeDtypeStruct((B, L, hp), BF16)] + out_shape
    weights = [w["win"], w["qn"], w["kvn"], w["wqa"], w["wqb"], w["wk"], w["wv"], w["vone"]]
    return pl.pallas_call(
        functools.partial(_mla_in_kernel, need_q=need_q),
        grid=(B, L // tm),
        in_specs=[row, full(gamma), mod, mod] + [full(a) for a in weights] + [tab, tab],
        out_specs=out_specs,
        out_shape=out_shape,
        compiler_params=_cparams(("arbitrary", "arbitrary")),
        name="mla_in_lat" if need_q else "mla_in_ctx",
    )(x, gamma, shift, scale, *weights, cos_t, sin_t)


def _attn_kernel(q_ref, kl_ref, kc_ref, vl_ref, vc_ref, o_ref):
    tq = q_ref.shape[1]
    blocks = [(j, r) for r in range(0, tq, ATTN_QB) for j in range(2)]

    def scores(j, r):
        sl = slice(j * HEAD_PAD, (j + 1) * HEAD_PAD)
        q = q_ref[0, r:r + ATTN_QB, sl]
        return _dot_nt(q, kl_ref[0, :, sl]), _dot_nt(q, kc_ref[0, :, sl])

    def weighted(j, s1, s2):
        sl = slice(j * HEAD_PAD, (j + 1) * HEAD_PAD)
        m = jnp.maximum(jnp.max(s1, axis=-1, keepdims=True), jnp.max(s2, axis=-1, keepdims=True))
        p1 = jnp.exp2((s1 - m) * EXP2_SCALE).astype(BF16)
        p2 = jnp.exp2((s2 - m) * EXP2_SCALE).astype(BF16)
        return _dot(p1, vl_ref[0, :, sl]) + _dot(p2, vc_ref[0, :, sl])

    pending = scores(*blocks[0])
    acc = {}
    for n, (j, r) in enumerate(blocks):
        s1, s2 = pending
        if n + 1 < len(blocks):
            pending = scores(*blocks[n + 1])
        acc[(j, r)] = weighted(j, s1, s2)
    lane = lax.broadcasted_iota(I32, (ATTN_QB, 2 * V_DIM), 1)
    for r in range(0, tq, ATTN_QB):
        even = acc[(0, r)]
        odd = acc[(1, r)]
        even = even / even[:, V_DIM:V_DIM + 1]
        odd = odd / odd[:, 0:1]
        o_ref[0, r:r + ATTN_QB, :] = jnp.where(lane < V_DIM, even, odd).astype(BF16)


def _attention(q, k_lat, k_ctx, v_lat, v_ctx, tq):
    B, L, _ = q.shape
    Lc = k_ctx.shape[1]
    pair = 2 * HEAD_PAD
    return pl.pallas_call(
        _attn_kernel,
        grid=(B, MLA_HEADS // 2, L // tq),
        in_specs=[pl.BlockSpec((1, tq, pair), lambda b, h, i: (b, i, h)),
                  pl.BlockSpec((1, L, pair), lambda b, h, i: (b, 0, h)),
                  pl.BlockSpec((1, Lc, pair), lambda b, h, i: (b, 0, h)),
                  pl.BlockSpec((1, L, pair), lambda b, h, i: (b, 0, h)),
                  pl.BlockSpec((1, Lc, pair), lambda b, h, i: (b, 0, h))],
        out_specs=pl.BlockSpec((1, tq, 2 * V_DIM), lambda b, h, i: (b, i, h)),
        out_shape=jax.ShapeDtypeStruct((B, L, MLA_HEADS * V_DIM), BF16),
        compiler_params=_cparams(("arbitrary", "arbitrary", "arbitrary")),
        name="attention",
    )(q, k_lat, k_ctx, v_lat, v_ctx)


def _rope_tables(rows):
    t = np.arange(rows * GRID_W)
    pos = np.stack([t // GRID_W, t % GRID_W], axis=-1).astype(np.float32)
    freq = jnp.asarray(ROPE_BASE, F32) ** (-jnp.arange(ROPE_FREQS, dtype=F32) / ROPE_FREQS)
    ang = jnp.asarray(pos)[:, :, None] * freq
    cos, sin = jnp.cos(ang), jnp.sin(ang)
    cos_r = jnp.stack([cos, cos], axis=2).reshape(-1, QK_ROPE)
    sin_r = jnp.stack([-sin, sin], axis=2).reshape(-1, QK_ROPE)
    T = cos_r.shape[0]
    pad = HEAD_PAD - QK_DIM
    cos_t = jnp.concatenate([jnp.ones((T, QK_NOPE), F32), cos_r, jnp.zeros((T, pad), F32)], axis=1)
    sin_t = jnp.concatenate([jnp.zeros((T, QK_NOPE), F32), sin_r, jnp.zeros((T, pad), F32)], axis=1)
    return cos_t, sin_t


def _rope_partner():
    p = np.arange(QK_ROPE)
    half = (p % (2 * ROPE_FREQS)) // ROPE_FREQS
    return np.where(half == 0, p + ROPE_FREQS, p - ROPE_FREQS)


def _mla_weights(w_in, q_norm, kv_norm, w_uq, w_ukv):
    partner = _rope_partner()
    pad = HEAD_PAD - QK_DIM
    kv0 = Q_LORA + KV_LORA
    kr = w_in[:, kv0:]
    zl = jnp.zeros((D_MODEL, QK_NOPE), F32)
    zr = jnp.zeros((D_MODEL, pad), F32)
    win = jnp.concatenate([w_in[:, :kv0], zl, kr, zr, zl, kr[:, partner], zr], axis=1)
    wq = w_uq.reshape(Q_LORA, MLA_HEADS, QK_DIM)
    zq = jnp.zeros((Q_LORA, MLA_HEADS, pad), F32)
    wqa = jnp.concatenate([wq, zq], axis=2)
    wqb = jnp.concatenate([jnp.zeros((Q_LORA, MLA_HEADS, QK_NOPE), F32),
                           wq[:, :, QK_NOPE:][:, :, partner], zq], axis=2)
    wkv = w_ukv.reshape(KV_LORA, MLA_HEADS, QK_NOPE + V_DIM)
    wk = jnp.concatenate([wkv[:, :, :QK_NOPE],
                          jnp.zeros((KV_LORA, MLA_HEADS, HEAD_PAD - QK_NOPE), F32)], axis=2)
    zv = jnp.zeros((KV_LORA, MLA_HEADS // 2, HEAD_PAD - V_DIM), F32)
    wv2 = wkv[:, :, QK_NOPE:].reshape(KV_LORA, MLA_HEADS // 2, 2, V_DIM)
    wv = jnp.stack([jnp.concatenate([wv2[:, :, 0], zv], axis=2),
                    jnp.concatenate([zv, wv2[:, :, 1]], axis=2)], axis=2)
    hp = MLA_HEADS * HEAD_PAD
    vone = np.zeros((MLA_HEADS // 2, 2, HEAD_PAD), np.float32)
    vone[:, 0, V_DIM] = 1.0
    vone[:, 1, 0] = 1.0
    return {"win": win.astype(BF16),
            "qn": q_norm.reshape(1, Q_LORA), "kvn": kv_norm.reshape(1, KV_LORA),
            "wqa": wqa.reshape(Q_LORA, hp).astype(BF16),
            "wqb": wqb.reshape(Q_LORA, hp).astype(BF16),
            "wk": wk.reshape(KV_LORA, hp).astype(BF16),
            "wv": wv.reshape(KV_LORA, hp).astype(BF16),
            "vone": jnp.asarray(vone.reshape(1, hp))}


def _pad_router(r):
    return jnp.pad(r, ((0, 0), (0, LANES - N_EXPERTS))).astype(BF16)


def kernel(x, c, ctx, c_ctx, ada_w, ada_b, norm_mix, norm_ffn, norm_final, lru_w_in, lru_conv_w, lru_conv_b, lru_gate_w, lru_gate_b, lru_lambda, lru_w_out, mla_w_in, mla_q_norm, mla_kv_norm, mla_w_uq, mla_w_ukv, mla_w_o, moe_router, moe_w_up, moe_w_down):
    B, T, D = x.shape
    Lc = ctx.shape[1]

    rows = ((B + 1 + 7) // 8) * 8
    cc = jnp.concatenate([c, c_ctx[None, :], jnp.zeros((rows - B - 1, D), F32)], axis=0)
    ada = _adaln(cc, ada_w, ada_b).reshape(DEPTH, rows, 6, D)

    def mods(i):
        lat = [ada[i, :B, k].reshape(B, 1, D) for k in range(6)]
        cx = [jnp.broadcast_to(ada[i, B, k].reshape(1, 1, D), (B, 1, D)) for k in range(6)]
        return lat, cx

    vec = lambda a: a.reshape(1, -1)

    (s1, sc1, g1, s2, sc2, g2), (s1c, sc1c, g1c, s2c, sc2c, g2c) = mods(0)
    w_in = lru_w_in[0].astype(BF16)
    cw, cb = 0.5 * lru_conv_w[0], vec(0.5 * lru_conv_b[0])
    g_lat, xc_lat = _lru_in(x, vec(norm_mix[0]), s1, sc1, w_in, cw, cb, tt=512)
    g_ctx, xc_ctx = _lru_in(ctx, vec(norm_mix[0]), s1c, sc1c, w_in, cw, cb, tt=Lc)
    ys = []
    for d in range(2):
        gw = lru_gate_w[0, d]
        wg = jnp.concatenate([gw[0], gw[1]], axis=-1).astype(BF16)
        ys.append(_lru_scan(xc_ctx, xc_lat, wg, 0.5 * lru_gate_b[0, d], vec(lru_lambda[0, d]),
                            reverse=(d == 1)))
    w_out = lru_w_out[0].astype(BF16)
    router = _pad_router(moe_router[0])
    gf = vec(norm_ffn[0])
    x_a, h_lat, aff_lat = _mix_out((ys[0][1], ys[1][1], g_lat), x, w_out, g1, gf, s2, sc2, router, tm=512)
    c_a, h_ctx, aff_ctx = _mix_out((ys[0][0], ys[1][0], g_ctx), ctx, w_out, g1c, gf, s2c, sc2c, router, tm=Lc)
    x_b = _ec_moe(x_a, h_lat, aff_lat, moe_w_up, moe_w_down, 0, g2, gf, final=False)
    c_b = _ec_moe(c_a, h_ctx, aff_ctx, moe_w_up, moe_w_down, 0, g2c, gf, final=False)

    (s1, sc1, g1, s2, sc2, g2), (s1c, sc1c, _, _, _, _) = mods(1)
    w = _mla_weights(mla_w_in[0], mla_q_norm[0], mla_kv_norm[0], mla_w_uq[0], mla_w_ukv[0])
    cos_t, sin_t = _rope_tables(T // GRID_W)
    one_t = jnp.concatenate([jnp.ones((Lc, QK_DIM), F32), jnp.zeros((Lc, HEAD_PAD - QK_DIM), F32)], axis=1)
    gm = vec(norm_mix[1])
    q, k_lat, v_lat = _mla_in(x_b, gm, s1, sc1, w, cos_t, sin_t, True, tm=512)
    k_ctx, v_ctx = _mla_in(c_b, gm, s1c, sc1c, w, one_t, jnp.zeros_like(one_t), False, tm=Lc)
    attn = _attention(q, k_lat, k_ctx, v_lat, v_ctx, tq=2048)
    x_c, h_lat, aff_lat = _mix_out((attn,), x_b, mla_w_o[0].astype(BF16), g1, vec(norm_ffn[1]),
                                   s2, sc2, _pad_router(moe_router[1]), tm=512)
    return _ec_moe(x_c, h_lat, aff_lat, moe_w_up, moe_w_down, 1, g2, vec(norm_final), final=True)
```

```python
import functools
import math

import numpy as np
import jax
import jax.numpy as jnp
from jax import lax
from jax.experimental import pallas as pl
from jax.experimental.pallas import tpu as pltpu

F32 = jnp.float32
BF16 = jnp.bfloat16
I32 = jnp.int32

D_MODEL = 1024
DEPTH = 2
GRID_W = 64
D_RNN = D_MODEL
CONV_W = 4
LRU_BLOCKS = 8
LRU_BLOCK = D_RNN // LRU_BLOCKS
LRU_C = 8.0
MLA_HEADS = 16
QK_NOPE = 64
QK_ROPE = 32
QK_DIM = QK_NOPE + QK_ROPE
V_DIM = 64
Q_LORA = 384
KV_LORA = 256
ROPE_FREQS = QK_ROPE // 4
ROPE_BASE = 10000.0
ATTN_SCALE = QK_DIM ** -0.5
N_EXPERTS = 16
D_EXPERT = 1024
CAPACITY_FACTOR = 2
EPS = 1e-6

LANES = 128
HEAD_PAD = LANES
VMEM_LIMIT = 56 * 1024 * 1024
NEG_BIG = -1e30
HALO = 8
MOE_ROWS = 2048
SCAN_TT = 64
COMBINE_ROWS = 256
COMBINE_WIN = 64
SLOT_ALIGN = 16
ATTN_QB = 512
EXP2_SCALE = ATTN_SCALE * math.log2(math.e)


def _cparams(sem):
    return pltpu.CompilerParams(dimension_semantics=sem, vmem_limit_bytes=VMEM_LIMIT)


def _sigmoid(x):
    return 0.5 * (jnp.tanh(0.5 * x) + 1.0)


def _silu(x):
    return x * _sigmoid(x)


def _gelu_tanh(x):
    return 0.5 * x * (1.0 + jnp.tanh(0.7978845608028654 * (x + 0.044715 * (x * x * x))))


def _rms(x, gamma):
    return x * lax.rsqrt(jnp.mean(x * x, axis=-1, keepdims=True) + EPS) * gamma


def _rms_mod(x, gamma, shift, scale):
    return _rms(x, gamma) * (1.0 + scale) + shift


def _dot(a, b):
    return jnp.dot(a, b, preferred_element_type=F32)


def _dot_nt(a, b):
    return lax.dot_general(a, b, (((1,), (1,)), ((), ())), preferred_element_type=F32)


def _adaln_kernel(c_ref, w_ref, b_ref, o_ref):
    s = _silu(c_ref[...]).astype(BF16)
    o_ref[0] = _dot(s, w_ref[0].astype(BF16)) + b_ref[0]


def _adaln(cc, ada_w, ada_b):
    rows = cc.shape[0]
    tn = 1024
    return pl.pallas_call(
        _adaln_kernel,
        grid=(DEPTH, 6 * D_MODEL // tn),
        in_specs=[pl.BlockSpec((rows, D_MODEL), lambda l, j: (0, 0)),
                  pl.BlockSpec((1, D_MODEL, tn), lambda l, j: (l, 0, j)),
                  pl.BlockSpec((1, 1, tn), lambda l, j: (l, 0, j))],
        out_specs=pl.BlockSpec((1, rows, tn), lambda l, j: (l, 0, j)),
        out_shape=jax.ShapeDtypeStruct((DEPTH, rows, 6 * D_MODEL), F32),
        compiler_params=_cparams(("arbitrary", "arbitrary")),
        name="adaln",
    )(cc, ada_w, ada_b.reshape(DEPTH, 1, 6 * D_MODEL))


def _lru_in_kernel(xp_ref, x_ref, xn_ref, gam_ref, sh_ref, sc_ref, w_ref, cw_ref, cb_ref,
                   g_ref, xc_ref, *, tt):
    i = pl.program_id(1)
    n = pl.num_programs(1)
    xe = jnp.concatenate([xp_ref[0], x_ref[0], xn_ref[0]], axis=0)
    he = _rms_mod(xe, gam_ref[...], sh_ref[0], sc_ref[0]).astype(BF16)
    z = _dot(he, w_ref[...])
    g_ref[0] = _gelu_tanh(z[HALO:HALO + tt, :D_RNN]).astype(BF16)
    z2 = z[:, D_RNN:]
    row = lax.broadcasted_iota(I32, (tt + 2 * HALO, 1), 0)
    valid = jnp.logical_and(jnp.logical_or(row >= HALO, i > 0),
                            jnp.logical_or(row < tt + HALO, i < n - 1))
    z2 = jnp.where(valid, z2, 0.0)
    cw = cw_ref[...]
    base = HALO - CONV_W // 2
    acc = z2[base:base + tt] * cw[0:1]
    for k in range(1, CONV_W):
        acc = acc + z2[base + k:base + k + tt] * cw[k:k + 1]
    xc_ref[0] = (acc + cb_ref[...]).astype(BF16)


def _lru_in(x, gamma, shift, scale, w_in, conv_w, conv_b, tt):
    B, L, D = x.shape
    nt = L // tt
    hb = tt // HALO
    last = L // HALO - 1
    vec = pl.BlockSpec((1, D), lambda b, i: (0, 0))
    mod = pl.BlockSpec((1, 1, D), lambda b, i: (b, 0, 0))
    out = pl.BlockSpec((1, tt, D_RNN), lambda b, i: (b, i, 0))
    return pl.pallas_call(
        functools.partial(_lru_in_kernel, tt=tt),
        grid=(B, nt),
        in_specs=[pl.BlockSpec((1, HALO, D), lambda b, i: (b, jnp.maximum(i * hb - 1, 0), 0)),
                  pl.BlockSpec((1, tt, D), lambda b, i: (b, i, 0)),
                  pl.BlockSpec((1, HALO, D), lambda b, i: (b, jnp.minimum((i + 1) * hb, last), 0)),
                  vec, mod, mod,
                  pl.BlockSpec((D, 2 * D_RNN), lambda b, i: (0, 0)),
                  pl.BlockSpec((CONV_W, D_RNN), lambda b, i: (0, 0)),
                  pl.BlockSpec((1, D_RNN), lambda b, i: (0, 0))],
        out_specs=[out, out],
        out_shape=[jax.ShapeDtypeStruct((B, L, D_RNN), BF16)] * 2,
        compiler_params=_cparams(("arbitrary", "arbitrary")),
        name="lru_in",
    )(x, x, x, gamma, shift, scale, w_in, conv_w, conv_b)


def _scan_kernel(xc_ctx_ref, xc_lat_ref, wg_ref, bg_ref, lam_ref, y_ctx_ref, y_lat_ref,
                 a_s, u_s, y_s, h_s, *, reverse, n_ctx):
    i = pl.program_id(0)
    nb, tt, _ = a_s.shape

    @pl.when(i == 0)
    def _():
        h_s[...] = jnp.zeros_like(h_s)

    def tile(x_ref, y_ref):
        x2 = x_ref[...].reshape(nb * tt, D_RNN)
        nlam = -lam_ref[...]
        sp = jnp.maximum(nlam, 0.0) + jnp.log1p(jnp.exp(-jnp.abs(nlam)))
        k2 = sp * (-0.5 * LRU_C * math.log2(math.e))
        for n in range(LRU_BLOCKS):
            sl = slice(n * LRU_BLOCK, (n + 1) * LRU_BLOCK)
            xb = x2[:, sl]
            g = _dot(xb, wg_ref[n])
            tr = jnp.tanh(g[:, :LRU_BLOCK] + bg_ref[0:1, sl])
            ti = jnp.tanh(g[:, LRU_BLOCK:] + bg_ref[1:2, sl])
            a = jnp.exp2(k2[:, sl] * tr + k2[:, sl])
            v = 1.0 - a * a
            root = jnp.where(v > 0.0, v * lax.rsqrt(v), 0.0)
            u = root * ((ti + 1.0) * xb.astype(F32))
            a_s[:, :, sl] = a.reshape(nb, tt, LRU_BLOCK)
            u_s[:, :, sl] = u.reshape(nb, tt, LRU_BLOCK)
        h = h_s[...]
        steps = range(tt - 1, -1, -1) if reverse else range(tt)
        for t in steps:
            h = a_s[:, t, :] * h + u_s[:, t, :]
            y_s[:, t, :] = h
        h_s[...] = h
        y_ref[...] = y_s[...].astype(BF16)

    @pl.when(i < n_ctx)
    def _():
        tile(xc_ctx_ref, y_ctx_ref)

    @pl.when(i >= n_ctx)
    def _():
        tile(xc_lat_ref, y_lat_ref)


def _lru_scan(xc_ctx, xc_lat, wg, bg, lam, reverse):
    B, Lc, _ = xc_ctx.shape
    L = xc_lat.shape[1]
    tt = SCAN_TT
    n_ctx = Lc // tt
    n_lat = L // tt

    def ctx_map(i):
        j = jnp.minimum(i, n_ctx - 1)
        return (0, (n_ctx - 1 - j) if reverse else j, 0)

    def lat_map(i):
        j = jnp.maximum(i - n_ctx, 0)
        return (0, (n_lat - 1 - j) if reverse else j, 0)

    return pl.pallas_call(
        functools.partial(_scan_kernel, reverse=reverse, n_ctx=n_ctx),
        grid=(n_ctx + n_lat,),
        in_specs=[pl.BlockSpec((B, tt, D_RNN), ctx_map),
                  pl.BlockSpec((B, tt, D_RNN), lat_map),
                  pl.BlockSpec((LRU_BLOCKS, LRU_BLOCK, 2 * LRU_BLOCK), lambda i: (0, 0, 0)),
                  pl.BlockSpec((2, D_RNN), lambda i: (0, 0)),
                  pl.BlockSpec((1, D_RNN), lambda i: (0, 0))],
        out_specs=[pl.BlockSpec((B, tt, D_RNN), ctx_map),
                   pl.BlockSpec((B, tt, D_RNN), lat_map)],
        out_shape=[jax.ShapeDtypeStruct((B, Lc, D_RNN), BF16),
                   jax.ShapeDtypeStruct((B, L, D_RNN), BF16)],
        scratch_shapes=[pltpu.VMEM((B, tt, D_RNN), F32),
                        pltpu.VMEM((B, tt, D_RNN), F32),
                        pltpu.VMEM((B, tt, D_RNN), F32),
                        pltpu.VMEM((B, D_RNN), F32)],
        compiler_params=_cparams(("arbitrary",)),
        name="lru_scan_rev" if reverse else "lru_scan_fwd",
    )(xc_ctx, xc_lat, wg, bg, lam)


def _mix_out_kernel(*refs, lru):
    if lru:
        (yf_ref, yb_ref, g_ref, x_ref, w_ref, g1_ref, gam_ref, sh_ref, sc_ref, r_ref,
         xo_ref, h_ref, aff_ref) = refs
        y = yf_ref[0].astype(F32) + yb_ref[0].astype(F32)
        lhs = (y * g_ref[0].astype(F32)).astype(BF16)
    else:
        (a_ref, x_ref, w_ref, g1_ref, gam_ref, sh_ref, sc_ref, r_ref,
         xo_ref, h_ref, aff_ref) = refs
        lhs = a_ref[0]
    xn = x_ref[0] + g1_ref[0] * _dot(lhs, w_ref[...])
    xo_ref[0] = xn
    hf = _rms_mod(xn, gam_ref[...], sh_ref[0], sc_ref[0])
    h_ref[0] = hf
    logits = _dot(hf.astype(BF16), r_ref[...])
    lane = lax.broadcasted_iota(I32, logits.shape, 1)
    logits = jnp.where(lane < N_EXPERTS, logits, NEG_BIG)
    p = jnp.exp(logits - jnp.max(logits, axis=-1, keepdims=True))
    aff_ref[0] = p / jnp.sum(p, axis=-1, keepdims=True)


def _mix_out(acts, x, w_out, g1, gamma, shift, scale, router_pad, tm):
    B, L, D = x.shape
    row = pl.BlockSpec((1, tm, D), lambda b, i: (b, i, 0))
    vec = pl.BlockSpec((1, D), lambda b, i: (0, 0))
    mod = pl.BlockSpec((1, 1, D), lambda b, i: (b, 0, 0))
    return pl.pallas_call(
        functools.partial(_mix_out_kernel, lru=len(acts) == 3),
        grid=(B, L // tm),
        in_specs=[row] * len(acts) + [
            row,
            pl.BlockSpec((D, D), lambda b, i: (0, 0)),
            mod, vec, mod, mod,
            pl.BlockSpec((D, LANES), lambda b, i: (0, 0))],
        out_specs=[row, row,
                   pl.BlockSpec((1, tm, LANES), lambda b, i: (b, i, 0))],
        out_shape=[jax.ShapeDtypeStruct((B, L, D), F32),
                   jax.ShapeDtypeStruct((B, L, D), F32),
                   jax.ShapeDtypeStruct((B, L, LANES), F32)],
        compiler_params=_cparams(("arbitrary", "arbitrary")),
        name="mix_out_lru" if len(acts) == 3 else "mix_out_attn",
    )(*acts, x, w_out, g1, gamma, shift, scale, router_pad)


def _topk_kernel(a_ref, offs_ref, slot_ref, first_ref, idx_ref, cum_s, *, cap, rb):
    L, ncol = a_ref.shape
    a = a_ref[...]

    def as_f32(bits):
        return lax.bitcast_convert_type(bits, F32)

    def search(k, thr):
        cand = thr | lax.shift_left(jnp.int32(1), 30 - k)
        cnt = jnp.sum((a >= as_f32(cand)).astype(I32), axis=0, keepdims=True)
        return jnp.where(cnt >= cap, cand, thr)

    thr = lax.fori_loop(0, 31, search, jnp.zeros((1, ncol), I32))
    lo = as_f32(thr)
    hi = as_f32(thr + 1)
    need = (cap - jnp.sum((a >= hi).astype(I32), axis=0, keepdims=True)).astype(F32)
    eq_b = jnp.logical_and(a >= lo, a < hi).astype(F32).astype(BF16)
    col = lax.broadcasted_iota(I32, (rb, L), 1)
    rowi = lax.broadcasted_iota(I32, (rb, L), 0)

    def before(r0):
        return (col < rowi + r0).astype(F32).astype(BF16)

    sel_blocks = []
    for r0 in range(0, L, rb):
        blk = a[r0:r0 + rb]
        eq_rank = _dot(before(r0), eq_b)
        take = jnp.logical_or(blk >= hi, jnp.logical_and(blk >= lo, eq_rank < need))
        sel_blocks.append(take.astype(F32))
    sel = jnp.concatenate(sel_blocks, axis=0) if len(sel_blocks) > 1 else sel_blocks[0]
    sel_b = sel.astype(BF16)
    offs = offs_ref[...]
    for k, r0 in enumerate(range(0, L, rb)):
        ahead = _dot(before(r0), sel_b)
        pos = ahead.astype(I32) + offs
        slot_ref[r0:r0 + rb, :] = jnp.where(sel_blocks[k] > 0.0, pos, -1)
        first_ref[k:k + 1, :] = pos[0:1, :]
        cum_s[r0:r0 + rb, :] = ahead + sel_blocks[k]

    def rows_of_slots(i, carry):
        base = i * 8
        cum = cum_s[...]
        out = [jnp.sum((cum <= jnp.asarray(base + j, F32)).astype(I32), axis=0, keepdims=True)
               for j in range(8)]
        idx_ref[pl.ds(pl.multiple_of(base, 8), 8), :] = jnp.concatenate(out, axis=0)
        return carry

    lax.fori_loop(0, cap // 8, rows_of_slots, 0)


def _topk_slots(aff_cols, offs, cap):
    L, ncol = aff_cols.shape
    rb = min(L, COMBINE_ROWS)
    return pl.pallas_call(
        functools.partial(_topk_kernel, cap=cap, rb=rb),
        grid=(1,),
        in_specs=[pl.BlockSpec((L, ncol), lambda i: (0, 0)),
                  pl.BlockSpec((1, ncol), lambda i: (0, 0))],
        out_specs=[pl.BlockSpec((L, ncol), lambda i: (0, 0)),
                   pl.BlockSpec((L // rb, ncol), lambda i: (0, 0)),
                   pl.BlockSpec((cap, ncol), lambda i: (0, 0))],
        out_shape=[jax.ShapeDtypeStruct((L, ncol), I32),
                   jax.ShapeDtypeStruct((L // rb, ncol), I32),
                   jax.ShapeDtypeStruct((cap, ncol), I32)],
        scratch_shapes=[pltpu.VMEM((L, ncol), F32)],
        compiler_params=_cparams(("arbitrary",)),
        name="topk_slots",
    )(aff_cols, offs)


def _moe_ffn_kernel(idx_ref, idx_next_ref, h_hbm, slot_ref, gate_ref, wu_ref, wd_ref, yg_ref,
                    wu_s, wd_s, rows_a, rows_b, sems, *, n_slots):
    n_groups = pl.num_programs(1)
    step = pl.program_id(0) * n_groups + pl.program_id(1)
    last = pl.num_programs(0) * n_groups - 1
    rows = slot_ref.shape[3]

    def gather(ids_ref, buf, sem):
        for s in range(n_slots):
            pltpu.make_async_copy(h_hbm.at[pl.ds(ids_ref[0, 0, 0, s], 1)], buf.at[pl.ds(s, 1)],
                                  sem).start(priority=s % 2)

    def gathered(buf, sem):
        pltpu.make_async_copy(h_hbm.at[pl.ds(0, n_slots)], buf, sem).wait()

    @pl.when(step == 0)
    def _():
        gather(idx_ref, rows_a, sems.at[0])

    @pl.when(pl.program_id(1) == 0)
    def _():
        wu_s[...] = wu_ref[0, 0].astype(BF16)
        wd_s[...] = wd_ref[0, 0].astype(BF16)

    def ffn(cur, cur_sem, nxt, nxt_sem):
        gathered(cur, cur_sem)
        xg = cur[...].astype(BF16)
        gather(idx_next_ref, nxt, nxt_sem)
        hit = lax.broadcasted_iota(I32, (n_slots, rows), 0) == slot_ref[0, 0]
        gate_s = jnp.sum(jnp.where(hit, gate_ref[0, 0], 0.0), axis=1, keepdims=True)
        up = _dot(xg, wu_s[...])
        hid = (_silu(up[:, :D_EXPERT]) * up[:, D_EXPERT:]).astype(BF16)
        yg_ref[0, 0] = (_dot(hid, wd_s[...]) * gate_s).astype(BF16)

        @pl.when(step == last)
        def _():
            gathered(nxt, nxt_sem)

    @pl.when(step % 2 == 0)
    def _():
        ffn(rows_a, sems.at[0], rows_b, sems.at[1])

    @pl.when(step % 2 == 1)
    def _():
        ffn(rows_b, sems.at[1], rows_a, sems.at[0])


def _moe_ffn(idx_g, h_rows, slot_g, gate_g, w_up, w_down, layer, n_slots):
    ng, _, _, rows = slot_g.shape
    D = D_MODEL
    n_steps = N_EXPERTS * ng

    def next_ids(e, g):
        n = jnp.minimum(e * ng + g + 1, n_steps - 1)
        return (n % ng, n // ng, 0, 0)

    route = pl.BlockSpec((1, 1, 1, rows), lambda e, g: (g, e, 0, 0))
    return pl.pallas_call(
        functools.partial(_moe_ffn_kernel, n_slots=n_slots),
        grid=(N_EXPERTS, ng),
        in_specs=[pl.BlockSpec((1, 1, 1, n_slots), lambda e, g: (g, e, 0, 0), memory_space=pltpu.SMEM),
                  pl.BlockSpec((1, 1, 1, n_slots), next_ids, memory_space=pltpu.SMEM),
                  pl.BlockSpec(memory_space=pl.ANY),
                  route, route,
                  pl.BlockSpec((1, 1, D, 2 * D_EXPERT), lambda e, g: (layer, e, 0, 0)),
                  pl.BlockSpec((1, 1, D_EXPERT, D), lambda e, g: (layer, e, 0, 0))],
        out_specs=pl.BlockSpec((1, 1, n_slots, D), lambda e, g: (g, e, 0, 0)),
        out_shape=jax.ShapeDtypeStruct((ng, N_EXPERTS, n_slots, D), BF16),
        scratch_shapes=[pltpu.VMEM((D, 2 * D_EXPERT), BF16),
                        pltpu.VMEM((D_EXPERT, D), BF16),
                        pltpu.VMEM((n_slots, D), F32),
                        pltpu.VMEM((n_slots, D), F32),
                        pltpu.SemaphoreType.DMA((2,))],
        compiler_params=_cparams(("arbitrary", "arbitrary")),
        name="moe_ffn",
    )(idx_g, idx_g, h_rows, slot_g, gate_g, w_up, w_down)


def _moe_combine_kernel(win_ref, fits_ref, yg_ref, slot_ref, x_ref, g2_ref, gam_ref, o_ref, *, final):
    g = pl.program_id(0)
    _, n_e, n_slots, D = yg_ref.shape
    n_sub = x_ref.shape[1] // COMBINE_ROWS

    def combine_chunk(c, rs):
        slots = slot_ref[0, rs, :]

        def finish(comb):
            xn = x_ref[0, rs, :] + g2_ref[0] * comb
            o_ref[0, rs, :] = _rms(xn, gam_ref[...]) if final else xn

        @pl.when(fits_ref[g, c] != 0)
        def _():
            lane = lax.broadcasted_iota(I32, (COMBINE_ROWS, 2 * COMBINE_WIN), 1)
            pieces, windows = [], []
            for e in range(0, n_e, 2):
                rel = []
                for j in range(2):
                    start = win_ref[g, c * n_e + e + j]
                    r = slots[:, e + j:e + j + 1] - start
                    inside = jnp.logical_and(r >= 0, r < COMBINE_WIN)
                    rel.append(jnp.where(inside, r + j * COMBINE_WIN, -1))
                    windows.append(yg_ref[0, e + j, pl.ds(pl.multiple_of(start, SLOT_ALIGN), COMBINE_WIN), :])
                hit = jnp.logical_or(lane == rel[0], lane == rel[1])
                pieces.append(hit.astype(F32).astype(BF16))
            finish(_dot(jnp.concatenate(pieces, axis=1), jnp.concatenate(windows, axis=0)))

        @pl.when(fits_ref[g, c] == 0)
        def _():
            lane = lax.broadcasted_iota(I32, (COMBINE_ROWS, n_slots), 1)
            pieces = [(slots[:, e:e + 1] == lane).astype(F32).astype(BF16) for e in range(n_e)]
            finish(_dot(jnp.concatenate(pieces, axis=1), yg_ref[0].reshape(n_e * n_slots, D)))

    for sub in range(n_sub):
        combine_chunk(pl.program_id(1) * n_sub + sub, slice(sub * COMBINE_ROWS, (sub + 1) * COMBINE_ROWS))


def _moe_combine(win, fits, yg, slot_t, x_g, g2, gamma, L, final):
    ng, n_e, n_slots, D = yg.shape
    rows = x_g.shape[1]
    chunk = min(2 * COMBINE_ROWS, L)
    row = pl.BlockSpec((1, chunk, D), lambda g, c, *_: (g, c, 0))
    return pl.pallas_call(
        functools.partial(_moe_combine_kernel, final=final),
        grid_spec=pltpu.PrefetchScalarGridSpec(
            num_scalar_prefetch=2,
            grid=(ng, rows // chunk),
            in_specs=[pl.BlockSpec((1, n_e, n_slots, D), lambda g, c, *_: (g, 0, 0, 0)),
                      pl.BlockSpec((1, chunk, LANES), lambda g, c, *_: (g, c, 0)),
                      row,
                      pl.BlockSpec((1, 1, D), lambda g, c, *_: ((g * rows + c * chunk) // L, 0, 0)),
                      pl.BlockSpec((1, D), lambda g, c, *_: (0, 0))],
            out_specs=row),
        out_shape=jax.ShapeDtypeStruct((ng, rows, D), F32),
        compiler_params=_cparams(("arbitrary", "arbitrary")),
        name="moe_combine_final" if final else "moe_combine",
    )(win, fits, yg, slot_t, x_g, g2, gamma)


def _combine_windows(first, ng, G, n_slots):
    nblk = first.shape[0]
    E = N_EXPERTS
    s = first.reshape(nblk, ng, G, E).transpose(1, 2, 0, 3).reshape(ng, G * nblk, E)
    s_next = jnp.concatenate([s[:, 1:], jnp.full((ng, 1, E), n_slots, I32)], axis=1)
    start = jnp.minimum((s // SLOT_ALIGN) * SLOT_ALIGN, n_slots - COMBINE_WIN)
    fits = jnp.all(s_next <= start + COMBINE_WIN, axis=2).astype(I32)
    return start.reshape(ng, G * nblk * E), fits


def _ec_moe(x, h, aff, w_up, w_down, layer, g2, gamma, final):
    B, L, D = x.shape
    E = N_EXPERTS
    cap = CAPACITY_FACTOR * L // E
    G = MOE_ROWS // L
    ng = B // G
    cols = aff[:, :, :E].transpose(1, 0, 2).reshape(L, B * E)
    offs = jnp.repeat((jnp.arange(B, dtype=I32) % G) * cap, E).reshape(1, B * E)
    slot, first, idx = _topk_slots(cols, offs, cap)

    def by_lane(a):
        return a.reshape(L, ng, G, E).transpose(1, 3, 2, 0).reshape(ng, E, 1, G * L)

    idx_g = idx.reshape(cap, ng, G, E).transpose(1, 3, 2, 0) + (jnp.arange(B, dtype=I32) * L).reshape(ng, 1, G, 1)
    yg = _moe_ffn(idx_g.reshape(ng, E, 1, G * cap), h.reshape(B * L, D),
                  by_lane(slot), by_lane(cols), w_up, w_down, layer, G * cap)
    slot_t = slot.reshape(L, ng, G, E).transpose(1, 2, 0, 3).reshape(ng, G * L, E)
    slot_t = jnp.pad(slot_t, ((0, 0), (0, 0), (0, LANES - E)), constant_values=-1)
    win, fits = _combine_windows(first, ng, G, G * cap)
    out = _moe_combine(win, fits, yg, slot_t, x.reshape(ng, G * L, D), g2, gamma, L, final)
    return out.reshape(B, L, D)


def _mla_in_kernel(*refs, need_q):
    (x_ref, gam_ref, sh_ref, sc_ref, win_ref, qn_ref, kvn_ref,
     wqa_ref, wqb_ref, wk_ref, wv_ref, vone_ref, cos_ref, sin_ref) = refs[:14]
    outs = refs[14:]
    h = _rms_mod(x_ref[0], gam_ref[...], sh_ref[0], sc_ref[0]).astype(BF16)
    z = _dot(h, win_ref[...])
    cos = cos_ref[...]
    sin = sin_ref[...]
    kv0 = Q_LORA + KV_LORA
    ckv = _rms(z[:, Q_LORA:kv0], kvn_ref[...]).astype(BF16)
    k_rope = z[:, kv0:kv0 + HEAD_PAD] * cos + z[:, kv0 + HEAD_PAD:kv0 + 2 * HEAD_PAD] * sin
    k_nope = _dot(ckv, wk_ref[...])
    if need_q:
        q_ref, k_ref, v_ref = outs
        cq = _rms(z[:, :Q_LORA], qn_ref[...]).astype(BF16)
        qa = _dot(cq, wqa_ref[...])
        qb = _dot(cq, wqb_ref[...])
    else:
        k_ref, v_ref = outs
    for hh in range(MLA_HEADS):
        sl = slice(hh * HEAD_PAD, (hh + 1) * HEAD_PAD)
        k_ref[0, :, sl] = (k_nope[:, sl] + k_rope).astype(BF16)
        if need_q:
            q_ref[0, :, sl] = (qa[:, sl] * cos + qb[:, sl] * sin).astype(BF16)
    v_ref[0] = (_dot(ckv, wv_ref[...]) + vone_ref[...]).astype(BF16)


def _mla_in(x, gamma, shift, scale, w, cos_t, sin_t, need_q, tm):
    B, L, D = x.shape
    row = pl.BlockSpec((1, tm, D), lambda b, i: (b, i, 0))
    mod = pl.BlockSpec((1, 1, D), lambda b, i: (b, 0, 0))

    def full(a):
        return pl.BlockSpec(a.shape, lambda b, i: (0,) * a.ndim)

    hp = MLA_HEADS * HEAD_PAD
    wide = pl.BlockSpec((1, tm, hp), lambda b, i: (b, i, 0))
    tab = pl.BlockSpec((tm, HEAD_PAD), lambda b, i: (i, 0))
    out_specs = [wide, wide]
    out_shape = [jax.ShapeDtypeStruct((B, L, hp), BF16)] * 2
    if need_q:
        out_specs = [wide] + out_specs
        out_shape = [jax.ShapeDtypeStruct((B, L, hp), BF16)] + out_shape
    weights = [w["win"], w["qn"], w["kvn"], w["wqa"], w["wqb"], w["wk"], w["wv"], w["vone"]]
    return pl.pallas_call(
        functools.partial(_mla_in_kernel, need_q=need_q),
        grid=(B, L // tm),
        in_specs=[row, full(gamma), mod, mod] + [full(a) for a in weights] + [tab, tab],
        out_specs=out_specs,
        out_shape=out_shape,
        compiler_params=_cparams(("arbitrary", "arbitrary")),
        name="mla_in_lat" if need_q else "mla_in_ctx",
    )(x, gamma, shift, scale, *weights, cos_t, sin_t)


def _attn_kernel(q_ref, kl_ref, kc_ref, vl_ref, vc_ref, o_ref):
    tq = q_ref.shape[1]
    blocks = [(j, r) for r in range(0, tq, ATTN_QB) for j in range(2)]

    def scores(j, r):
        sl = slice(j * HEAD_PAD, (j + 1) * HEAD_PAD)
        q = q_ref[0, r:r + ATTN_QB, sl]
        return _dot_nt(q, kl_ref[0, :, sl]), _dot_nt(q, kc_ref[0, :, sl])

    def weighted(j, s1, s2):
        sl = slice(j * HEAD_PAD, (j + 1) * HEAD_PAD)
        m = jnp.maximum(jnp.max(s1, axis=-1, keepdims=True), jnp.max(s2, axis=-1, keepdims=True))
        p1 = jnp.exp2((s1 - m) * EXP2_SCALE).astype(BF16)
        p2 = jnp.exp2((s2 - m) * EXP2_SCALE).astype(BF16)
        return _dot(p1, vl_ref[0, :, sl]) + _dot(p2, vc_ref[0, :, sl])

    pending = scores(*blocks[0])
    acc = {}
    for n, (j, r) in enumerate(blocks):
        s1, s2 = pending
        if n + 1 < len(blocks):
            pending = scores(*blocks[n + 1])
        acc[(j, r)] = weighted(j, s1, s2)
    lane = lax.broadcasted_iota(I32, (ATTN_QB, 2 * V_DIM), 1)
    for r in range(0, tq, ATTN_QB):
        even = acc[(0, r)]
        odd = acc[(1, r)]
        even = even / even[:, V_DIM:V_DIM + 1]
        odd = odd / odd[:, 0:1]
        o_ref[0, r:r + ATTN_QB, :] = jnp.where(lane < V_DIM, even, odd).astype(BF16)


def _attention(q, k_lat, k_ctx, v_lat, v_ctx, tq):
    B, L, _ = q.shape
    Lc = k_ctx.shape[1]
    pair = 2 * HEAD_PAD
    return pl.pallas_call(
        _attn_kernel,
        grid=(B, MLA_HEADS // 2, L // tq),
        in_specs=[pl.BlockSpec((1, tq, pair), lambda b, h, i: (b, i, h)),
                  pl.BlockSpec((1, L, pair), lambda b, h, i: (b, 0, h)),
                  pl.BlockSpec((1, Lc, pair), lambda b, h, i: (b, 0, h)),
                  pl.BlockSpec((1, L, pair), lambda b, h, i: (b, 0, h)),
                  pl.BlockSpec((1, Lc, pair), lambda b, h, i: (b, 0, h))],
        out_specs=pl.BlockSpec((1, tq, 2 * V_DIM), lambda b, h, i: (b, i, h)),
        out_shape=jax.ShapeDtypeStruct((B, L, MLA_HEADS * V_DIM), BF16),
        compiler_params=_cparams(("arbitrary", "arbitrary", "arbitrary")),
        name="attention",
    )(q, k_lat, k_ctx, v_lat, v_ctx)


def _rope_tables(rows):
    t = np.arange(rows * GRID_W)
    pos = np.stack([t // GRID_W, t % GRID_W], axis=-1).astype(np.float32)
    freq = jnp.asarray(ROPE_BASE, F32) ** (-jnp.arange(ROPE_FREQS, dtype=F32) / ROPE_FREQS)
    ang = jnp.asarray(pos)[:, :, None] * freq
    cos, sin = jnp.cos(ang), jnp.sin(ang)
    cos_r = jnp.stack([cos, cos], axis=2).reshape(-1, QK_ROPE)
    sin_r = jnp.stack([-sin, sin], axis=2).reshape(-1, QK_ROPE)
    T = cos_r.shape[0]
    pad = HEAD_PAD - QK_DIM
    cos_t = jnp.concatenate([jnp.ones((T, QK_NOPE), F32), cos_r, jnp.zeros((T, pad), F32)], axis=1)
    sin_t = jnp.concatenate([jnp.zeros((T, QK_NOPE), F32), sin_r, jnp.zeros((T, pad), F32)], axis=1)
    return cos_t, sin_t


def _rope_partner():
    p = np.arange(QK_ROPE)
    half = (p % (2 * ROPE_FREQS)) // ROPE_FREQS
    return np.where(half == 0, p + ROPE_FREQS, p - ROPE_FREQS)


def _mla_weights(w_in, q_norm, kv_norm, w_uq, w_ukv):
    partner = _rope_partner()
    pad = HEAD_PAD - QK_DIM
    kv0 = Q_LORA + KV_LORA
    kr = w_in[:, kv0:]
    zl = jnp.zeros((D_MODEL, QK_NOPE), F32)
    zr = jnp.zeros((D_MODEL, pad), F32)
    win = jnp.concatenate([w_in[:, :kv0], zl, kr, zr, zl, kr[:, partner], zr], axis=1)
    wq = w_uq.reshape(Q_LORA, MLA_HEADS, QK_DIM)
    zq = jnp.zeros((Q_LORA, MLA_HEADS, pad), F32)
    wqa = jnp.concatenate([wq, zq], axis=2)
    wqb = jnp.concatenate([jnp.zeros((Q_LORA, MLA_HEADS, QK_NOPE), F32),
                           wq[:, :, QK_NOPE:][:, :, partner], zq], axis=2)
    wkv = w_ukv.reshape(KV_LORA, MLA_HEADS, QK_NOPE + V_DIM)
    wk = jnp.concatenate([wkv[:, :, :QK_NOPE],
                          jnp.zeros((KV_LORA, MLA_HEADS, HEAD_PAD - QK_NOPE), F32)], axis=2)
    zv = jnp.zeros((KV_LORA, MLA_HEADS // 2, HEAD_PAD - V_DIM), F32)
    wv2 = wkv[:, :, QK_NOPE:].reshape(KV_LORA, MLA_HEADS // 2, 2, V_DIM)
    wv = jnp.stack([jnp.concatenate([wv2[:, :, 0], zv], axis=2),
                    jnp.concatenate([zv, wv2[:, :, 1]], axis=2)], axis=2)
    hp = MLA_HEADS * HEAD_PAD
    vone = np.zeros((MLA_HEADS // 2, 2, HEAD_PAD), np.float32)
    vone[:, 0, V_DIM] = 1.0
    vone[:, 1, 0] = 1.0
    return {"win": win.astype(BF16),
            "qn": q_norm.reshape(1, Q_LORA), "kvn": kv_norm.reshape(1, KV_LORA),
            "wqa": wqa.reshape(Q_LORA, hp).astype(BF16),
            "wqb": wqb.reshape(Q_LORA, hp).astype(BF16),
            "wk": wk.reshape(KV_LORA, hp).astype(BF16),
            "wv": wv.reshape(KV_LORA, hp).astype(BF16),
            "vone": jnp.asarray(vone.reshape(1, hp))}


def _pad_router(r):
    return jnp.pad(r, ((0, 0), (0, LANES - N_EXPERTS))).astype(BF16)


def kernel(x, c, ctx, c_ctx, ada_w, ada_b, norm_mix, norm_ffn, norm_final, lru_w_in, lru_conv_w, lru_conv_b, lru_gate_w, lru_gate_b, lru_lambda, lru_w_out, mla_w_in, mla_q_norm, mla_kv_norm, mla_w_uq, mla_w_ukv, mla_w_o, moe_router, moe_w_up, moe_w_down):
    B, T, D = x.shape
    Lc = ctx.shape[1]

    rows = ((B + 1 + 7) // 8) * 8
    cc = jnp.concatenate([c, c_ctx[None, :], jnp.zeros((rows - B - 1, D), F32)], axis=0)
    ada = _adaln(cc, ada_w, ada_b).reshape(DEPTH, rows, 6, D)

    def mods(i):
        lat = [ada[i, :B, k].reshape(B, 1, D) for k in range(6)]
        cx = [jnp.broadcast_to(ada[i, B, k].reshape(1, 1, D), (B, 1, D)) for k in range(6)]
        return lat, cx

    vec = lambda a: a.reshape(1, -1)

    (s1, sc1, g1, s2, sc2, g2), (s1c, sc1c, g1c, s2c, sc2c, g2c) = mods(0)
    w_in = lru_w_in[0].astype(BF16)
    cw, cb = 0.5 * lru_conv_w[0], vec(0.5 * lru_conv_b[0])
    g_lat, xc_lat = _lru_in(x, vec(norm_mix[0]), s1, sc1, w_in, cw, cb, tt=512)
    g_ctx, xc_ctx = _lru_in(ctx, vec(norm_mix[0]), s1c, sc1c, w_in, cw, cb, tt=Lc)
    ys = []
    for d in range(2):
        gw = lru_gate_w[0, d]
        wg = jnp.concatenate([gw[0], gw[1]], axis=-1).astype(BF16)
        ys.append(_lru_scan(xc_ctx, xc_lat, wg, 0.5 * lru_gate_b[0, d], vec(lru_lambda[0, d]),
                            reverse=(d == 1)))
    w_out = lru_w_out[0].astype(BF16)
    router = _pad_router(moe_router[0])
    gf = vec(norm_ffn[0])
    x_a, h_lat, aff_lat = _mix_out((ys[0][1], ys[1][1], g_lat), x, w_out, g1, gf, s2, sc2, router, tm=512)
    c_a, h_ctx, aff_ctx = _mix_out((ys[0][0], ys[1][0], g_ctx), ctx, w_out, g1c, gf, s2c, sc2c, router, tm=Lc)
    x_b = _ec_moe(x_a, h_lat, aff_lat, moe_w_up, moe_w_down, 0, g2, gf, final=False)
    c_b = _ec_moe(c_a, h_ctx, aff_ctx, moe_w_up, moe_w_down, 0, g2c, gf, final=False)

    (s1, sc1, g1, s2, sc2, g2), (s1c, sc1c, _, _, _, _) = mods(1)
    w = _mla_weights(mla_w_in[0], mla_q_norm[0], mla_kv_norm[0], mla_w_uq[0], mla_w_ukv[0])
    cos_t, sin_t = _rope_tables(T // GRID_W)
    one_t = jnp.concatenate([jnp.ones((Lc, QK_DIM), F32), jnp.zeros((Lc, HEAD_PAD - QK_DIM), F32)], axis=1)
    gm = vec(norm_mix[1])
    q, k_lat, v_lat = _mla_in(x_b, gm, s1, sc1, w, cos_t, sin_t, True, tm=512)
    k_ctx, v_ctx = _mla_in(c_b, gm, s1c, sc1c, w, one_t, jnp.zeros_like(one_t), False, tm=Lc)
    attn = _attention(q, k_lat, k_ctx, v_lat, v_ctx, tq=2048)
    x_c, h_lat, aff_lat = _mix_out((attn,), x_b, mla_w_o[0].astype(BF16), g1, vec(norm_ffn[1]),
                                   s2, sc2, _pad_router(moe_router[1]), tm=512)
    return _ec_moe(x_c, h_lat, aff_lat, moe_w_up, moe_w_down, 1, g2, vec(norm_final), final=True)
```
